```python
import math
import jax
import jax.numpy as jnp
from jax import lax
import numpy as np

D_MODEL = 2048
BATCH = 1
SEQ = 16384
DEPTH = 2

CTX_LEN = 256
GRID_W = 64
N_MOD = 6
EPS = 1e-6
CHUNK = 64
CONV_W = 4

DN_HEADS = D_MODEL // 256
DN_DK = 128
DN_DV = 128
DN_QK = DN_HEADS * DN_DK
DN_V = DN_HEADS * DN_DV
LRU_WIDTH = D_MODEL // 2
LRU_BLOCKS = 8
LRU_BW = LRU_WIDTH // LRU_BLOCKS
LRU_C = 8.0
EV_IN = 2 * DN_QK + 2 * DN_V + 4 * DN_HEADS + 2 * LRU_WIDTH
EV_MIX = DN_V + LRU_WIDTH
GLA_HEADS = 4
GLA_QK = D_MODEL // 2
GLA_V = D_MODEL
GLA_DK = GLA_QK // GLA_HEADS
GLA_DV = GLA_V // GLA_HEADS
GLA_RANK = 16
GLA_TAU = 16.0
OD_IN = 2 * GLA_QK + 2 * GLA_V + 2 * GLA_RANK
FFN_HIDDEN = 11 * D_MODEL // 4
N_EXPERTS = 8
TOP_K = 2
MOE_BLOCK = 128

N_EVEN = (DEPTH + 1) // 2
N_ODD = DEPTH // 2

kernel_name = 'hybrid_deltanet_rglru_gla_moe_dit'


def rms_norm(x, g):
    x32 = x.astype(jnp.float32)
    y = x32 * lax.rsqrt(jnp.mean(x32 * x32, axis=-1, keepdims=True) + EPS)
    return y.astype(x.dtype) * g


def l2_normalize(t):
    t32 = t.astype(jnp.float32)
    return (t32 * lax.rsqrt(jnp.sum(t32 * t32, axis=-1, keepdims=True) + EPS)).astype(t.dtype)


def adaln(cond, w, b):
    m = jax.nn.silu(cond) @ w + b
    m = m.reshape(m.shape[0], 1, N_MOD, -1)
    return tuple(m[:, :, j] for j in range(N_MOD))


def modulate(h, shift, scale):
    return h * (1.0 + scale) + shift


def split_cols(p, sizes):
    out, start = [], 0
    for s in sizes:
        out.append(p[..., start:start + s])
        start += s
    return out


def centred_conv(x, w):
    k_w, length = w.shape[0], x.shape[1]
    left = k_w // 2
    xp = jnp.pad(x, ((0, 0), (left, k_w - 1 - left), (0, 0)))
    out = xp[:, 0:length] * w[0]
    for j in range(1, k_w):
        out = out + xp[:, j:j + length] * w[j]
    return out


def raster_transpose(t, rows, cols):
    b, n, d = t.shape
    return t.reshape(b, rows, cols, d).swapaxes(1, 2).reshape(b, n, d)


def gated_delta_chunked(q, k, v, g, beta, s0):
    b_, h_, length, _ = q.shape
    dv = v.shape[-1]
    out_dtype = v.dtype
    n = length // CHUNK
    f32 = jnp.float32
    chunks = lambda t: t.astype(f32).reshape(b_, h_, n, CHUNK, *t.shape[3:])
    q, k, v, beta = chunks(q), chunks(k), chunks(v), chunks(beta)
    gc = jnp.cumsum(chunks(g), axis=-1)
    lower = jnp.tril(jnp.ones((CHUNK, CHUNK), bool))
    strict = jnp.tril(jnp.ones((CHUNK, CHUNK), bool), -1)
    decay = jnp.exp(jnp.where(lower, gc[..., :, None] - gc[..., None, :], -jnp.inf))
    kb = k * beta[..., None]
    a = jnp.where(strict, jnp.einsum('bhnid,bhnjd->bhnij', kb, k) * decay, 0.0)
    eye = jnp.eye(CHUNK, dtype=f32)
    t_inv = lax.linalg.triangular_solve(eye + a, jnp.broadcast_to(eye, a.shape), left_side=True, lower=True, unit_diagonal=True)
    u = t_inv @ (v * beta[..., None])
    w = t_inv @ (kb * jnp.exp(gc)[..., None])
    qk = jnp.einsum('bhnid,bhnjd->bhnij', q, k) * decay
    q_dec = q * jnp.exp(gc)[..., None]
    k_dec = k * jnp.exp(gc[..., -1:] - gc)[..., None]
    g_last = jnp.exp(gc[..., -1])

    def step(s, xs):
        u_c, w_c, q_c, qk_c, k_c, gl_c = xs
        v_new = u_c - w_c @ s
        o = q_c @ s + qk_c @ v_new
        s = s * gl_c[..., None, None] + jnp.swapaxes(k_c, -1, -2) @ v_new
        return s, o

    xs = tuple(jnp.moveaxis(t, 2, 0) for t in (u, w, q_dec, qk, k_dec, g_last))
    s, o = lax.scan(step, s0.astype(f32), xs)
    return jnp.moveaxis(o, 0, 2).reshape(b_, h_, length, dv).astype(out_dtype), s.astype(s0.dtype)


def gla_chunked(q, k, v, g, s0):
    b_, h_, length, _ = q.shape
    dv = v.shape[-1]
    out_dtype = v.dtype
    n = length // CHUNK
    f32 = jnp.float32
    chunks = lambda t: jnp.moveaxis(t.astype(f32).reshape(b_, h_, n, CHUNK, t.shape[-1]), 2, 0)
    gc = jnp.cumsum(chunks(g), axis=3)
    causal = jnp.tril(jnp.ones((CHUNK, CHUNK), bool))[:, :, None]

    def step(s, xs):
        q_c, k_c, v_c, g_c = xs
        rel = jnp.exp(jnp.where(causal, g_c[:, :, :, None] - g_c[:, :, None], -jnp.inf))
        att = jnp.einsum('bhid,bhjd,bhijd->bhij', q_c, k_c, rel)
        o = (q_c * jnp.exp(g_c)) @ s + att @ v_c
        s = s * jnp.exp(g_c[:, :, -1])[..., None] + jnp.swapaxes(k_c * jnp.exp(g_c[:, :, -1:] - g_c), -1, -2) @ v_c
        return s, o

    s, o = lax.scan(step, s0.astype(f32), (chunks(q), chunks(k), chunks(v), gc))
    return jnp.moveaxis(o, 0, 2).reshape(b_, h_, length, dv).astype(out_dtype), s.astype(s0.dtype)


def linear_recurrence(a, b, h0):
    b = b.at[:, 0].add(a[:, 0] * h0)

    def combine(l, r):
        return l[0] * r[0], r[0] * l[1] + r[1]

    _, h = lax.associative_scan(combine, (a, b), axis=1)
    return h, h[:, -1]


def prefix_scan(core, ctx_args, lat_args, s0, axis, reverse):
    if reverse:
        ctx_args = tuple(jnp.flip(t, axis) for t in ctx_args)
        lat_args = tuple(jnp.flip(t, axis) for t in lat_args)
    o_ctx, s_ctx = core(*ctx_args, s0)
    o_lat, _ = core(*lat_args, s_ctx)
    if reverse:
        o_ctx, o_lat = jnp.flip(o_ctx, axis), jnp.flip(o_lat, axis)
    return o_ctx, o_lat


def bidirectional(core, ctx_shared, ctx_dir, lat_shared, lat_dir, s0, axis):
    outs = []
    for d, rev in ((0, False), (1, True)):
        ca = tuple(ctx_shared) + tuple(t[d] for t in ctx_dir)
        la = tuple(lat_shared) + tuple(t[d] for t in lat_dir)
        outs.append(prefix_scan(core, ca, la, s0, axis, rev))
    (oc_f, ol_f), (oc_b, ol_b) = outs
    return oc_f + oc_b, ol_f + ol_b


def even_mixer(h_ctx, h_lat, w_in, conv_qkv, dn_a_log, dn_dt_bias, dn_norm_g, lru_conv_w, lru_conv_b,
               lru_wa, lru_ba, lru_wx, lru_bx, lru_lambda, w_out, ctx_out):
    f32 = jnp.float32

    def prepare(h):
        bn, ln, _ = h.shape
        qkv, z, a_raw, b_raw, xr, gr = split_cols(
            h @ w_in, (2 * DN_QK + DN_V, DN_V, 2 * DN_HEADS, 2 * DN_HEADS, LRU_WIDTH, LRU_WIDTH))
        q, k, v = split_cols(jax.nn.silu(centred_conv(qkv, conv_qkv)), (DN_QK, DN_QK, DN_V))
        heads = lambda t: t.reshape(bn, ln, DN_HEADS, -1).transpose(0, 2, 1, 3)
        q = l2_normalize(heads(q)) * DN_DK ** -0.5
        k = l2_normalize(heads(k))
        v = heads(v)
        a_raw = a_raw.reshape(bn, ln, 2, DN_HEADS).transpose(2, 0, 3, 1)
        b_raw = b_raw.reshape(bn, ln, 2, DN_HEADS).transpose(2, 0, 3, 1)
        g = -jnp.exp(dn_a_log)[:, None, :, None] * jax.nn.softplus(a_raw + dn_dt_bias[:, None, :, None])
        beta = jax.nn.sigmoid(b_raw)
        xc = centred_conv(xr, lru_conv_w) + lru_conv_b
        xb = xc.reshape(bn, ln, LRU_BLOCKS, LRU_BW)
        r = jax.nn.sigmoid(jnp.einsum('blnc,dncm->dblnm', xb, lru_wa).reshape(2, bn, ln, LRU_WIDTH) + lru_ba[:, None, None])
        gi = jax.nn.sigmoid(jnp.einsum('blnc,dncm->dblnm', xb, lru_wx).reshape(2, bn, ln, LRU_WIDTH) + lru_bx[:, None, None])
        log_a = -LRU_C * r.astype(f32) * jax.nn.softplus(-lru_lambda.astype(f32))[:, None, None]
        a = jnp.exp(log_a)
        b_in = jnp.sqrt(-jnp.expm1(2.0 * log_a)) * (gi * xc).astype(f32)
        return (q, k, v), (g, beta), (a, b_in), (z, gr)

    qkv_c, gb_c, ab_c, zg_c = prepare(h_ctx)
    qkv_l, gb_l, ab_l, zg_l = prepare(h_lat)
    bn = h_lat.shape[0]
    dn_ctx, dn_lat = bidirectional(gated_delta_chunked, qkv_c, gb_c, qkv_l, gb_l,
                                   jnp.zeros((bn, DN_HEADS, DN_DK, DN_DV), h_lat.dtype), 2)
    lru_ctx, lru_lat = bidirectional(linear_recurrence, (), ab_c, (), ab_l, jnp.zeros((bn, LRU_WIDTH), f32), 1)

    def finish(o_dn, h_lru, zg):
        z, gr = zg
        bn_, ln_ = z.shape[:2]
        o = rms_norm(o_dn.transpose(0, 2, 1, 3), dn_norm_g) * jax.nn.silu(z.reshape(bn_, ln_, DN_HEADS, DN_DV))
        y_lru = h_lru.astype(z.dtype) * jax.nn.gelu(gr)
        return jnp.concatenate([o.reshape(bn_, ln_, DN_V), y_lru], axis=-1) @ w_out

    y_lat = finish(dn_lat, lru_lat, zg_l)
    y_ctx = finish(dn_ctx, lru_ctx, zg_c) if ctx_out else None
    return y_ctx, y_lat


def odd_mixer(h_ctx, h_lat, w_in, gla_wg2, gla_bg, gla_norm_g, w_out, ctx_out):
    rows = h_lat.shape[1] // GRID_W
    h_lat = raster_transpose(h_lat, rows, GRID_W)

    def prepare(h):
        bn, ln, _ = h.shape
        q, k, v, gout, gdown = split_cols(h @ w_in, (GLA_QK, GLA_QK, GLA_V, GLA_V, 2 * GLA_RANK))
        heads = lambda t: t.reshape(bn, ln, GLA_HEADS, -1).transpose(0, 2, 1, 3)
        logit = jnp.einsum('bldr,drk->dblk', gdown.reshape(bn, ln, 2, GLA_RANK), gla_wg2) + gla_bg[:, None, None]
        glog = jax.nn.log_sigmoid(logit.astype(jnp.float32)) / GLA_TAU
        glog = glog.reshape(2, bn, ln, GLA_HEADS, GLA_DK).transpose(0, 1, 3, 2, 4)
        return (heads(q) * GLA_DK ** -0.5, heads(k), heads(v)), (glog,), gout

    qkv_c, g_c, gout_c = prepare(h_ctx)
    qkv_l, g_l, gout_l = prepare(h_lat)
    bn = h_lat.shape[0]
    o_ctx, o_lat = bidirectional(gla_chunked, qkv_c, g_c, qkv_l, g_l,
                                 jnp.zeros((bn, GLA_HEADS, GLA_DK, GLA_DV), h_lat.dtype), 2)

    def finish(o, gout):
        bn_, ln_ = gout.shape[:2]
        o = rms_norm(o.transpose(0, 2, 1, 3), gla_norm_g) * jax.nn.silu(gout.reshape(bn_, ln_, GLA_HEADS, GLA_DV))
        return o.reshape(bn_, ln_, GLA_V) @ w_out

    y_lat = raster_transpose(finish(o_lat, gout_l), GRID_W, rows)
    y_ctx = finish(o_ctx, gout_c) if ctx_out else None
    return y_ctx, y_lat


def swiglu(h, w_gate, w_up, w_down):
    return (jax.nn.silu(h @ w_gate) * (h @ w_up)) @ w_down


def moe_swiglu(x, router_w, router_b, w_gate, w_up, w_down):
    n_tok, d = x.shape
    logits = (x @ router_w + router_b).astype(jnp.float32)
    top_logit, top_idx = lax.top_k(logits, TOP_K)
    top_w = jax.nn.softmax(top_logit, axis=-1).astype(x.dtype)
    n_assign = n_tok * TOP_K
    e_flat = top_idx.reshape(-1)
    tok_flat = jnp.arange(n_assign, dtype=jnp.int32) // TOP_K
    w_flat = top_w.reshape(-1)
    order = jnp.argsort(e_flat)
    e_sorted = e_flat[order]
    counts = jnp.bincount(e_flat, length=N_EXPERTS)
    padded = (counts + MOE_BLOCK - 1) // MOE_BLOCK * MOE_BLOCK
    pad_end = jnp.cumsum(padded)
    pad_start = pad_end - padded
    raw_start = jnp.cumsum(counts) - counts
    dest = pad_start[e_sorted] + jnp.arange(n_assign, dtype=jnp.int32) - raw_start[e_sorted]
    n_blocks = -(-n_assign // MOE_BLOCK) + N_EXPERTS
    n_slots = n_blocks * MOE_BLOCK
    slot_tok = jnp.full((n_slots,), n_tok, jnp.int32).at[dest].set(tok_flat[order])
    slot_w = jnp.zeros((n_slots,), x.dtype).at[dest].set(w_flat[order])
    block_expert = jnp.minimum(
        jnp.searchsorted(pad_end, jnp.arange(n_blocks, dtype=jnp.int32) * MOE_BLOCK, side='right'), N_EXPERTS - 1)
    x_pad = jnp.concatenate([x, jnp.zeros((1, d), x.dtype)], axis=0)
    xb = x_pad[slot_tok].reshape(n_blocks, MOE_BLOCK, d)

    def expert_block(args):
        xb_i, e = args
        return swiglu(xb_i, w_gate[e], w_up[e], w_down[e])

    yb = lax.map(expert_block, (xb, block_expert)).reshape(n_slots, d)
    y = jnp.zeros((n_tok + 1, d), x.dtype).at[slot_tok].add(yb * slot_w[:, None])
    return y[:n_tok]


def setup_inputs(seed: int = 0) -> dict:
    key = jax.random.key(seed)
    ks = iter(jax.random.split(key, 48))
    D = D_MODEL
    f32 = jnp.float32
    nrm = lambda shape, scale: jax.random.normal(next(ks), shape, f32) * scale
    gain = lambda shape: 1.0 + nrm(shape, 0.02)
    u_lam = jax.random.uniform(next(ks), (N_EVEN, 2, LRU_WIDTH), f32, 0.9, 0.999)
    a_lam = u_lam ** (1.0 / LRU_C)
    lam = jnp.log(a_lam) - jnp.log1p(-a_lam)
    dt = jnp.exp(jax.random.uniform(next(ks), (N_EVEN, 2, DN_HEADS), f32, math.log(1e-3), math.log(1e-1)))
    dt_bias = dt + jnp.log(-jnp.expm1(-dt))
    a_log = jnp.log(jax.random.uniform(next(ks), (N_EVEN, 2, DN_HEADS), f32, 1.0, 16.0))
    return {
        'x': nrm((BATCH, SEQ, D), 1.0),
        'c': nrm((BATCH, D), 1.0),
        'ctx': nrm((BATCH, CTX_LEN, D), 1.0),
        'c_ctx': nrm((D,), 1.0),
        'mod_w': nrm((DEPTH, D, N_MOD * D), 0.5 * D ** -0.5),
        'mod_b': nrm((DEPTH, N_MOD * D), 0.01),
        'norm1_g': gain((DEPTH, D)),
        'norm2_g': gain((DEPTH, D)),
        'ev_w_in': nrm((N_EVEN, D, EV_IN), D ** -0.5),
        'ev_conv_qkv': nrm((N_EVEN, CONV_W, 2 * DN_QK + DN_V), CONV_W ** -0.5),
        'ev_dn_a_log': a_log,
        'ev_dn_dt_bias': dt_bias,
        'ev_dn_norm_g': gain((N_EVEN, DN_DV)),
        'ev_lru_conv_w': nrm((N_EVEN, CONV_W, LRU_WIDTH), CONV_W ** -0.5),
        'ev_lru_conv_b': nrm((N_EVEN, LRU_WIDTH), 0.01),
        'ev_lru_wa': nrm((N_EVEN, 2, LRU_BLOCKS, LRU_BW, LRU_BW), LRU_BW ** -0.5),
        'ev_lru_ba': nrm((N_EVEN, 2, LRU_WIDTH), 0.01),
        'ev_lru_wx': nrm((N_EVEN, 2, LRU_BLOCKS, LRU_BW, LRU_BW), LRU_BW ** -0.5),
        'ev_lru_bx': nrm((N_EVEN, 2, LRU_WIDTH), 0.01),
        'ev_lru_lambda': lam,
        'ev_w_out': nrm((N_EVEN, EV_MIX, D), EV_MIX ** -0.5),
        'ev_ffn_w_gate': nrm((N_EVEN, D, FFN_HIDDEN), D ** -0.5),
        'ev_ffn_w_up': nrm((N_EVEN, D, FFN_HIDDEN), D ** -0.5),
        'ev_ffn_w_down': nrm((N_EVEN, FFN_HIDDEN, D), FFN_HIDDEN ** -0.5),
        'od_w_in': nrm((N_ODD, D, OD_IN), D ** -0.5),
        'od_gla_wg2': nrm((N_ODD, 2, GLA_RANK, GLA_QK), GLA_RANK ** -0.5),
        'od_gla_bg': nrm((N_ODD, 2, GLA_QK), 0.01),
        'od_gla_norm_g': gain((N_ODD, GLA_DV)),
        'od_w_out': nrm((N_ODD, GLA_V, D), GLA_V ** -0.5),
        'od_router_w': nrm((N_ODD, D, N_EXPERTS), D ** -0.5),
        'od_router_b': nrm((N_ODD, N_EXPERTS), 0.01),
        'od_exp_w_gate': nrm((N_ODD, N_EXPERTS, D, FFN_HIDDEN), D ** -0.5),
        'od_exp_w_up': nrm((N_ODD, N_EXPERTS, D, FFN_HIDDEN), D ** -0.5),
        'od_exp_w_down': nrm((N_ODD, N_EXPERTS, FFN_HIDDEN, D), FFN_HIDDEN ** -0.5),
        'final_norm_g': gain((D,)),
    }


def reference(x, c, ctx, c_ctx, mod_w, mod_b, norm1_g, norm2_g, ev_w_in, ev_conv_qkv, ev_dn_a_log, ev_dn_dt_bias,
              ev_dn_norm_g, ev_lru_conv_w, ev_lru_conv_b, ev_lru_wa, ev_lru_ba, ev_lru_wx, ev_lru_bx, ev_lru_lambda,
              ev_w_out, ev_ffn_w_gate, ev_ffn_w_up, ev_ffn_w_down, od_w_in, od_gla_wg2, od_gla_bg, od_gla_norm_g,
              od_w_out, od_router_w, od_router_b, od_exp_w_gate, od_exp_w_up, od_exp_w_down, final_norm_g):
    b_, length, d = x.shape
    for layer in range(DEPTH):
        i = layer // 2
        last = layer == DEPTH - 1
        sh1, sc1, gt1, sh2, sc2, gt2 = adaln(c, mod_w[layer], mod_b[layer])
        csh1, csc1, cgt1, csh2, csc2, cgt2 = adaln(c_ctx[None], mod_w[layer], mod_b[layer])
        h_lat = modulate(rms_norm(x, norm1_g[layer]), sh1, sc1)
        h_ctx = modulate(rms_norm(ctx, norm1_g[layer]), csh1, csc1)
        if layer % 2 == 0:
            y_ctx, y_lat = even_mixer(h_ctx, h_lat, ev_w_in[i], ev_conv_qkv[i], ev_dn_a_log[i], ev_dn_dt_bias[i],
                                      ev_dn_norm_g[i], ev_lru_conv_w[i], ev_lru_conv_b[i], ev_lru_wa[i], ev_lru_ba[i],
                                      ev_lru_wx[i], ev_lru_bx[i], ev_lru_lambda[i], ev_w_out[i], not last)
            ffn = lambda t: swiglu(t, ev_ffn_w_gate[i], ev_ffn_w_up[i], ev_ffn_w_down[i])
        else:
            y_ctx, y_lat = odd_mixer(h_ctx, h_lat, od_w_in[i], od_gla_wg2[i], od_gla_bg[i], od_gla_norm_g[i],
                                     od_w_out[i], not last)
            ffn = lambda t: moe_swiglu(t.reshape(-1, d), od_router_w[i], od_router_b[i], od_exp_w_gate[i],
                                       od_exp_w_up[i], od_exp_w_down[i]).reshape(t.shape)
        x = x + gt1 * y_lat
        x = x + gt2 * ffn(modulate(rms_norm(x, norm2_g[layer]), sh2, sc2))
        if not last:
            ctx = ctx + cgt1 * y_ctx
            ctx = ctx + cgt2 * ffn(modulate(rms_norm(ctx, norm2_g[layer]), csh2, csc2))
    return rms_norm(x, final_norm_g)
```

```python
import functools
import math

import jax
import jax.numpy as jnp
from jax import lax
from jax.experimental import pallas as pl
from jax.experimental.pallas import tpu as pltpu

F32 = jnp.float32
BF16 = jnp.bfloat16
I32 = jnp.int32
HI = lax.Precision.HIGHEST

EPS = 1e-6
N_MOD = 6
GRID_W = 64
CHUNK = 64
SUB = 16
LANES = 128
ROW_TILE = 256
DN_HEADS = 8
DN_DH = 128
LRU_BLOCKS = 8
LRU_BW = 128
LRU_C = 8.0
GLA_HEADS = 4
GLA_RANK = 16
GLA_TAU = 16.0
N_EXPERTS = 8
MOE_ROWS = 256
VMEM_LIMIT = 56 * 1024 * 1024

NT_DIMS = (((1,), (1,)), ((), ()))
TN_DIMS = (((0,), (0,)), ((), ()))


def _params(*sem):
    return pltpu.CompilerParams(dimension_semantics=sem, vmem_limit_bytes=VMEM_LIMIT)


def _softplus(x):
    return jnp.maximum(x, 0.0) + jnp.log1p(jnp.exp(-jnp.abs(x)))


def _silu(x):
    return x * jax.nn.sigmoid(x)


def _bdot(a, b):
    return jnp.dot(a.astype(BF16), b.astype(BF16), preferred_element_type=F32)


def _adaln_kernel(cond_ref, w_ref, b_ref, o_ref):
    s = _silu(cond_ref[...])
    o_ref[...] = jnp.dot(s, w_ref[...], precision=HI, preferred_element_type=F32) + b_ref[...]


def adaln_all(cond8, mod_w, mod_b):
    n_layers, d, n6 = mod_w.shape
    tn = 1024
    return pl.pallas_call(
        _adaln_kernel,
        grid=(n_layers, n6 // tn),
        in_specs=[pl.BlockSpec((8, d), lambda l, j: (0, 0)),
                  pl.BlockSpec((None, d, tn), lambda l, j: (l, 0, j)),
                  pl.BlockSpec((None, 1, tn), lambda l, j: (l, 0, j))],
        out_specs=pl.BlockSpec((None, 8, tn), lambda l, j: (l, 0, j)),
        out_shape=jax.ShapeDtypeStruct((n_layers, 8, n6), F32),
        compiler_params=_params("arbitrary", "arbitrary"),
        name="adaln",
    )(cond8, mod_w, mod_b.reshape(n_layers, 1, n6))


def _norm_mod_kernel(x_ref, g_ref, sh_ref, sc_ref, o_ref):
    x = x_ref[...]
    y = x * lax.rsqrt(jnp.mean(x * x, axis=-1, keepdims=True) + EPS)
    o_ref[...] = ((y * g_ref[...]) * (1.0 + sc_ref[...]) + sh_ref[...]).astype(o_ref.dtype)


def _raster_spec(rows, d):
    return pl.BlockSpec((rows, d), lambda c: (0, c))


def norm_mod(x, g, sh, sc, *, raster=False, out_dtype=BF16):
    n, d = x.shape
    vec = pl.BlockSpec((1, d), lambda i: (0, 0))
    if raster:
        rows = n // GRID_W
        x_in, x_spec, grid = x.reshape(rows, GRID_W * d), _raster_spec(rows, d), (GRID_W,)
        o_spec = pl.BlockSpec((rows, d), lambda c: (c, 0))
    else:
        tm = min(ROW_TILE, n)
        x_in, x_spec, grid = x, pl.BlockSpec((tm, d), lambda i: (i, 0)), (n // tm,)
        o_spec = pl.BlockSpec((tm, d), lambda i: (i, 0))
    return pl.pallas_call(
        _norm_mod_kernel, grid=grid, in_specs=[x_spec, vec, vec, vec], out_specs=o_spec,
        out_shape=jax.ShapeDtypeStruct((n, d), out_dtype),
        compiler_params=_params("parallel"), name="norm_mod",
    )(x_in, g.reshape(1, d), sh.reshape(1, d), sc.reshape(1, d))


def _final_norm_kernel(x_ref, g_ref, o_ref):
    x = x_ref[...]
    o_ref[...] = x * lax.rsqrt(jnp.mean(x * x, axis=-1, keepdims=True) + EPS) * g_ref[...]


def final_norm(x, g):
    n, d = x.shape
    tm = min(ROW_TILE, n)
    return pl.pallas_call(
        _final_norm_kernel, grid=(n // tm,),
        in_specs=[pl.BlockSpec((tm, d), lambda i: (i, 0)), pl.BlockSpec((1, d), lambda i: (0, 0))],
        out_specs=pl.BlockSpec((tm, d), lambda i: (i, 0)),
        out_shape=jax.ShapeDtypeStruct((n, d), F32),
        compiler_params=_params("parallel"), name="final_norm",
    )(x, g.reshape(1, d))


def _mm_kernel(x_ref, w_ref, o_ref):
    o_ref[...] = jnp.dot(x_ref[...], w_ref[...], preferred_element_type=F32).astype(o_ref.dtype)


def matmul(x, w, *, tn, out_dtype=F32):
    r, k = x.shape
    n = w.shape[1]
    tm = min(ROW_TILE, r)
    return pl.pallas_call(
        _mm_kernel, grid=(n // tn, r // tm),
        in_specs=[pl.BlockSpec((tm, k), lambda j, i: (i, 0)),
                  pl.BlockSpec((k, tn), lambda j, i: (0, j))],
        out_specs=pl.BlockSpec((tm, tn), lambda j, i: (i, j)),
        out_shape=jax.ShapeDtypeStruct((r, n), out_dtype),
        compiler_params=_params("parallel", "parallel"), name="matmul",
    )(x, w)


def _evprep_kernel(p_ref, prev_ref, next_ref, pg_ref, cw_ref, cb_ref, alog_ref, dtb_ref,
                   q_ref, k_ref, v_ref, xc_ref, gb_ref, ext_ref, *, n_tiles):
    i = pl.program_id(0)
    tm, width = p_ref.shape
    ext_ref[8:8 + tm, :] = p_ref[...]
    ext_ref[0:8, :] = jnp.where(i > 0, prev_ref[...], 0.0)
    ext_ref[8 + tm:16 + tm, :] = jnp.where(i < n_tiles - 1, next_ref[...], 0.0)
    acc = ext_ref[pl.ds(6, tm), :] * cw_ref[0:1, :]
    for j in range(1, 4):
        acc = acc + ext_ref[pl.ds(6 + j, tm), :] * cw_ref[j:j + 1, :]
    acc = acc + cb_ref[...]
    qk_w = DN_HEADS * DN_DH
    for hd in range(DN_HEADS):
        for part, ref, scale in ((0, q_ref, DN_DH ** -0.5), (1, k_ref, 1.0)):
            lo = part * qk_w + hd * DN_DH
            t = _silu(acc[:, lo:lo + DN_DH])
            t = t * lax.rsqrt(jnp.sum(t * t, axis=-1, keepdims=True) + EPS)
            ref[:, hd * DN_DH:(hd + 1) * DN_DH] = t * scale
    v_ref[...] = _silu(acc[:, 2 * qk_w:3 * qk_w])
    xc_ref[...] = acc[:, 3 * qk_w:]
    pg = pg_ref[...]
    lane = lax.broadcasted_iota(I32, pg.shape, 1)
    g = -jnp.exp(alog_ref[...]) * _softplus(pg + dtb_ref[...])
    gb_ref[...] = jnp.where(lane < 2 * DN_HEADS, g, jax.nn.sigmoid(pg))


def even_prep(p, pg, conv_w, conv_b, a_log, dt_bias):
    r = p.shape[0]
    width = conv_w.shape[1]
    tm = min(ROW_TILE, r)
    n_tiles = r // tm
    hb = tm // 8
    out_w = DN_HEADS * DN_DH
    row = lambda w: pl.BlockSpec((tm, w), lambda i: (i, 0))
    vec = lambda w: pl.BlockSpec((1, w), lambda i: (0, 0))
    return pl.pallas_call(
        functools.partial(_evprep_kernel, n_tiles=n_tiles),
        grid=(n_tiles,),
        in_specs=[row(width),
                  pl.BlockSpec((8, width), lambda i: (jnp.maximum(i * hb - 1, 0), 0)),
                  pl.BlockSpec((8, width), lambda i: (jnp.minimum((i + 1) * hb, r // 8 - 1), 0)),
                  row(LANES),
                  pl.BlockSpec((4, width), lambda i: (0, 0)), vec(width), vec(LANES), vec(LANES)],
        out_specs=[row(out_w), row(out_w), row(out_w), row(out_w), row(LANES)],
        out_shape=[jax.ShapeDtypeStruct((r, out_w), F32)] * 4 + [jax.ShapeDtypeStruct((r, LANES), F32)],
        scratch_shapes=[pltpu.VMEM((tm + 16, width), F32)],
        compiler_params=_params("parallel"), name="even_prep",
    )(p, p, p, pg, conv_w, conv_b, a_log, dt_bias)


def _dnprep_kernel(q_ref, k_ref, v_ref, gb_ref, u_ref, w_ref, qd_ref, kd_ref, qk_ref, gl_ref, *, n_chunks):
    hd = pl.program_id(1)
    c = CHUNK
    ri = lax.broadcasted_iota(I32, (c, c), 0)
    ci = lax.broadcasted_iota(I32, (c, c), 1)
    eye = (ri == ci).astype(F32)
    ones = jnp.ones((c, c), F32)
    lane = lax.broadcasted_iota(I32, (c, LANES), 1)
    for d in range(2):
        incl = (ci >= ri) if d else (ci <= ri)
        strict = (ci > ri) if d else (ci < ri)
        incl_t = (ri >= ci) if d else (ri <= ci)
        tri = incl.astype(F32)
        for n in range(n_chunks):
            rows = slice(n * c, (n + 1) * c)
            gbc = gb_ref[rows, :]
            gcol = jnp.sum(jnp.where(lane == d * DN_HEADS + hd, gbc, 0.0), axis=1, keepdims=True)
            bcol = jnp.sum(jnp.where(lane == (2 + d) * DN_HEADS + hd, gbc, 0.0), axis=1, keepdims=True)
            gmat = jnp.broadcast_to(gcol, (c, c))
            cum_i = jnp.dot(tri, gmat, precision=HI, preferred_element_type=F32)
            cum_j = jnp.dot(ones, jnp.where(incl_t, gmat, 0.0), precision=HI,
                            preferred_element_type=F32)
            decay = jnp.where(incl, jnp.exp(cum_i - cum_j), 0.0)
            gc = cum_i[:, 0:1]
            tot = jnp.sum(gcol, axis=0, keepdims=True)
            q = q_ref[rows, :]
            k = k_ref[rows, :]
            v = v_ref[rows, :]
            kb = k * bcol
            a = jnp.where(strict, lax.dot_general(kb.astype(BF16), k.astype(BF16), NT_DIMS,
                                                  preferred_element_type=F32) * decay, 0.0)
            t_inv = eye - a
            a_pow = a
            for _ in range(5):
                a_pow = jnp.dot(a_pow, a_pow, precision=HI, preferred_element_type=F32)
                t_inv = t_inv + jnp.dot(t_inv, a_pow, precision=HI, preferred_element_type=F32)
            eg = jnp.exp(gc)
            u_ref[d, rows, :] = _bdot(t_inv, v * bcol)
            w_ref[d, rows, :] = _bdot(t_inv, kb * eg).astype(BF16)
            qk = lax.dot_general(q.astype(BF16), k.astype(BF16), NT_DIMS, preferred_element_type=F32) * decay
            qk_ref[d, rows, :] = qk.astype(BF16)
            qd_ref[d, rows, :] = (q * eg).astype(BF16)
            kd_ref[d, rows, :] = (k * jnp.exp(tot - gc)).astype(BF16)
            gl_ref[d, n] = jnp.broadcast_to(jnp.exp(tot), (8, LANES))


def deltanet_prep(q, k, v, gb):
    r = q.shape[0]
    tm = min(ROW_TILE, r)
    n_chunks = tm // CHUNK
    hw = DN_HEADS * DN_DH
    head = pl.BlockSpec((tm, DN_DH), lambda i, h: (i, h))
    dhead = pl.BlockSpec((2, tm, DN_DH), lambda i, h: (0, i, h))
    return pl.pallas_call(
        functools.partial(_dnprep_kernel, n_chunks=n_chunks),
        grid=(r // tm, DN_HEADS),
        in_specs=[head, head, head, pl.BlockSpec((tm, LANES), lambda i, h: (i, 0))],
        out_specs=[dhead, dhead, dhead, dhead,
                   pl.BlockSpec((2, None, tm, CHUNK), lambda i, h: (0, h, i, 0)),
                   pl.BlockSpec((2, None, n_chunks, 8, LANES), lambda i, h: (0, h, i, 0, 0))],
        out_shape=[jax.ShapeDtypeStruct((2, r, hw), F32),
                   jax.ShapeDtypeStruct((2, r, hw), BF16),
                   jax.ShapeDtypeStruct((2, r, hw), BF16),
                   jax.ShapeDtypeStruct((2, r, hw), BF16),
                   jax.ShapeDtypeStruct((2, DN_HEADS, r, CHUNK), BF16),
                   jax.ShapeDtypeStruct((2, DN_HEADS, r // CHUNK, 8, LANES), F32)],
        compiler_params=_params("parallel", "parallel"), name="deltanet_prep",
    )(q, k, v, gb)


def _dnscan_kernel(u_ref, w_ref, qd_ref, kd_ref, qk_ref, gl_ref, s0_ref, o_ref, sf_ref, s_ref, *, rev, n_steps):
    step = pl.program_id(0)

    @pl.when(step == 0)
    def _():
        s_ref[...] = s0_ref[...]

    c = CHUNK
    n_sub = u_ref.shape[0] // c
    order = range(n_sub - 1, -1, -1) if rev else range(n_sub)
    for n in order:
        rows = slice(n * c, (n + 1) * c)
        for hd in range(DN_HEADS):
            cols = slice(hd * DN_DH, (hd + 1) * DN_DH)
            s = s_ref[hd]
            sb = s.astype(BF16)
            v_new = u_ref[rows, cols] - jnp.dot(w_ref[rows, cols], sb, preferred_element_type=F32)
            vb = v_new.astype(BF16)
            o_ref[rows, cols] = (jnp.dot(qd_ref[rows, cols], sb, preferred_element_type=F32)
                                 + jnp.dot(qk_ref[hd, rows, :], vb, preferred_element_type=F32))
            s_ref[hd] = s * gl_ref[hd, n, 0:1, :] + lax.dot_general(
                kd_ref[rows, cols], vb, TN_DIMS, preferred_element_type=F32)

    @pl.when(step == n_steps - 1)
    def _():
        sf_ref[...] = s_ref[...]


def deltanet_scan(u, w, qd, kd, qk, gl, s0, *, d):
    r = u.shape[1]
    hw = DN_HEADS * DN_DH
    tm = 2 * CHUNK
    n_steps = r // tm
    idx = (lambda i: n_steps - 1 - i) if d else (lambda i: i)
    big = pl.BlockSpec((None, tm, hw), lambda i: (d, idx(i), 0))
    state = pl.BlockSpec((DN_HEADS, DN_DH, DN_DH), lambda i: (0, 0, 0))
    return pl.pallas_call(
        functools.partial(_dnscan_kernel, rev=bool(d), n_steps=n_steps),
        grid=(n_steps,),
        in_specs=[big, big, big, big,
                  pl.BlockSpec((None, DN_HEADS, tm, CHUNK), lambda i: (d, 0, idx(i), 0)),
                  pl.BlockSpec((None, DN_HEADS, tm // CHUNK, 8, LANES), lambda i: (d, 0, idx(i), 0, 0)),
                  state],
        out_specs=[pl.BlockSpec((tm, hw), lambda i: (idx(i), 0)), state],
        out_shape=[jax.ShapeDtypeStruct((r, hw), F32),
                   jax.ShapeDtypeStruct((DN_HEADS, DN_DH, DN_DH), F32)],
        scratch_shapes=[pltpu.VMEM((DN_HEADS, DN_DH, DN_DH), F32)],
        compiler_params=_params("arbitrary"), name="deltanet_scan",
    )(u, w, qd, kd, qk, gl, s0)


def _lru_kernel(xc_ref, wa_ref, wx_ref, ba_ref, bx_ref, lam_ref, h0_ref, h_ref, hf_ref,
                a_s, b_s, carry_s, *, rev, n_steps):
    step = pl.program_id(0)

    @pl.when(step == 0)
    def _():
        carry_s[...] = h0_ref[...]

    tm = xc_ref.shape[0]
    sp = _softplus(-lam_ref[...])
    for n in range(LRU_BLOCKS):
        cols = slice(n * LRU_BW, (n + 1) * LRU_BW)
        xb = xc_ref[:, cols]
        xbb = xb.astype(BF16)
        r = jax.nn.sigmoid(jnp.dot(xbb, wa_ref[n], preferred_element_type=F32) + ba_ref[:, cols])
        gi = jax.nn.sigmoid(jnp.dot(xbb, wx_ref[n], preferred_element_type=F32) + bx_ref[:, cols])
        log_a = -LRU_C * r * sp[:, cols]
        a = jnp.exp(log_a)
        a_s[:, cols] = a
        b_s[:, cols] = jnp.sqrt(-jnp.tanh(log_a) * (a * a + 1.0)) * (gi * xb)

    rid = lax.broadcasted_iota(I32, (8, a_s.shape[1]), 0)
    n_groups = tm // 8

    def group(gidx, carry):
        g = (n_groups - 1 - gidx) if rev else gidx
        base = pl.multiple_of(g * 8, 8)
        a = a_s[pl.ds(base, 8), :]
        b = b_s[pl.ds(base, 8), :]
        for sh in (1, 2, 4):
            if rev:
                keep = rid < 8 - sh
                a_n = jnp.where(keep, pltpu.roll(a, 8 - sh, 0), 1.0)
                b_n = jnp.where(keep, pltpu.roll(b, 8 - sh, 0), 0.0)
            else:
                keep = rid >= sh
                a_n = jnp.where(keep, pltpu.roll(a, sh, 0), 1.0)
                b_n = jnp.where(keep, pltpu.roll(b, sh, 0), 0.0)
            b = a * b_n + b
            a = a * a_n
        h = a * carry + b
        h_ref[pl.ds(base, 8), :] = h
        edge = h[0:1, :] if rev else h[7:8, :]
        return jnp.broadcast_to(edge, h.shape)

    carry = lax.fori_loop(0, n_groups, group, carry_s[...])
    carry_s[...] = carry

    @pl.when(step == n_steps - 1)
    def _():
        hf_ref[...] = carry


def lru_scan(xc, wa, wx, ba, bx, lam, h0, *, d):
    r, width = xc.shape
    tm = min(ROW_TILE, r)
    n_steps = r // tm
    idx = (lambda i: n_steps - 1 - i) if d else (lambda i: i)
    vec = pl.BlockSpec((1, width), lambda i: (0, 0))
    wspec = pl.BlockSpec((LRU_BLOCKS, LRU_BW, LRU_BW), lambda i: (0, 0, 0))
    st = pl.BlockSpec((8, width), lambda i: (0, 0))
    return pl.pallas_call(
        functools.partial(_lru_kernel, rev=bool(d), n_steps=n_steps),
        grid=(n_steps,),
        in_specs=[pl.BlockSpec((tm, width), lambda i: (idx(i), 0)), wspec, wspec, vec, vec, vec, st],
        out_specs=[pl.BlockSpec((tm, width), lambda i: (idx(i), 0)), st],
        out_shape=[jax.ShapeDtypeStruct((r, width), F32), jax.ShapeDtypeStruct((8, width), F32)],
        scratch_shapes=[pltpu.VMEM((tm, width), F32), pltpu.VMEM((tm, width), F32), pltpu.VMEM((8, width), F32)],
        compiler_params=_params("arbitrary"), name="lru_scan",
    )(xc, wa, wx, ba, bx, lam, h0)


def _gelu_tanh(x):
    return 0.5 * x * (1.0 + jnp.tanh(math.sqrt(2.0 / math.pi) * (x + 0.044715 * (x * x * x))))


def _evfin_kernel(of_ref, ob_ref, hf_ref, hb_ref, zg_ref, ng_ref, wout_ref, x_ref, gt_ref, o_ref, mix_s):
    hw = DN_HEADS * DN_DH
    for hd in range(DN_HEADS):
        cols = slice(hd * DN_DH, (hd + 1) * DN_DH)
        o = of_ref[:, cols] + ob_ref[:, cols]
        y = o * lax.rsqrt(jnp.mean(o * o, axis=-1, keepdims=True) + EPS) * ng_ref[...]
        mix_s[:, cols] = (y * _silu(zg_ref[:, cols])).astype(BF16)
    mix_s[:, hw:] = ((hf_ref[...] + hb_ref[...]) * _gelu_tanh(zg_ref[:, hw:])).astype(BF16)
    y = jnp.dot(mix_s[...], wout_ref[...], preferred_element_type=F32)
    o_ref[...] = x_ref[...] + gt_ref[...] * y


def even_finish(o_f, o_b, h_f, h_b, p, norm_g, w_out, x, gate):
    r, d = x.shape
    tm = min(ROW_TILE, r)
    hw = DN_HEADS * DN_DH
    row = lambda w: pl.BlockSpec((tm, w), lambda i: (i, 0))
    return pl.pallas_call(
        _evfin_kernel, grid=(r // tm,),
        in_specs=[row(hw), row(hw), row(hw), row(hw),
                  pl.BlockSpec((tm, 2 * hw), lambda i: (i, 2)),
                  pl.BlockSpec((1, DN_DH), lambda i: (0, 0)),
                  pl.BlockSpec(w_out.shape, lambda i: (0, 0)),
                  row(d), pl.BlockSpec((1, d), lambda i: (0, 0))],
        out_specs=row(d),
        out_shape=jax.ShapeDtypeStruct((r, d), F32),
        scratch_shapes=[pltpu.VMEM((tm, 2 * hw), BF16)],
        compiler_params=_params("parallel"), name="even_finish",
    )(o_f, o_b, h_f, h_b, p, norm_g.reshape(1, DN_DH), w_out, x, gate.reshape(1, d))


def _odfin_kernel(of_ref, ob_ref, go_ref, ng_ref, wout_ref, x_ref, gt_ref, o_ref, mix_s, *, dv):
    for hd in range(GLA_HEADS):
        cols = slice(hd * dv, (hd + 1) * dv)
        o = of_ref[:, cols] + ob_ref[:, cols]
        y = o * lax.rsqrt(jnp.mean(o * o, axis=-1, keepdims=True) + EPS) * ng_ref[...]
        mix_s[:, cols] = (y * _silu(go_ref[:, cols])).astype(BF16)
    y = jnp.dot(mix_s[...], wout_ref[...], preferred_element_type=F32)
    o_ref[...] = x_ref[...] + gt_ref[...] * y


def odd_finish(o_f, o_b, p, norm_g, w_out, x, gate):
    r, d = x.shape
    rows = r // GRID_W
    vw = o_f.shape[1]
    dv = vw // GLA_HEADS
    row = lambda w: pl.BlockSpec((rows, w), lambda c: (c, 0))
    return pl.pallas_call(
        functools.partial(_odfin_kernel, dv=dv), grid=(GRID_W,),
        in_specs=[row(vw), row(vw),
                  pl.BlockSpec((rows, vw), lambda c: (c, 2)),
                  pl.BlockSpec((1, dv), lambda c: (0, 0)),
                  pl.BlockSpec(w_out.shape, lambda c: (0, 0)),
                  _raster_spec(rows, d), pl.BlockSpec((1, d), lambda c: (0, 0))],
        out_specs=_raster_spec(rows, d),
        out_shape=jax.ShapeDtypeStruct((rows, GRID_W * d), F32),
        scratch_shapes=[pltpu.VMEM((rows, vw), BF16)],
        compiler_params=_params("parallel"), name="odd_finish",
    )(o_f, o_b, p, norm_g.reshape(1, dv), w_out, x.reshape(rows, GRID_W * d), gate.reshape(1, d)).reshape(r, d)


def _gla_kernel(q_ref, k_ref, v_ref, gd_ref, wg_ref, bg_ref, s0_ref, o_ref, sf_ref, s_ref, *, rev, n_steps, dk):
    step = pl.program_id(1)

    @pl.when(step == 0)
    def _():
        s_ref[...] = s0_ref[...]

    c = CHUNK
    ri = lax.broadcasted_iota(I32, (c, c), 0)
    ci = lax.broadcasted_iota(I32, (c, c), 1)
    tri = ((ci >= ri) if rev else (ci <= ri)).astype(F32)
    logit = jnp.dot(gd_ref[...], wg_ref[...], precision=HI, preferred_element_type=F32) + bg_ref[...]
    g = -_softplus(-logit) * (1.0 / GLA_TAU)
    gc = jnp.dot(tri, g, precision=HI, preferred_element_type=F32)
    tot = gc[0:1, :] if rev else gc[c - 1:c, :]
    q = q_ref[...] * dk ** -0.5
    k = k_ref[...]
    vb = v_ref[...].astype(BF16)
    s = s_ref[...]
    o_inter = lax.dot_general((q * jnp.exp(gc)).astype(BF16), s.astype(BF16), NT_DIMS,
                              preferred_element_type=F32)
    lane = lax.broadcasted_iota(I32, (SUB, c), 1)
    rsub = lax.broadcasted_iota(I32, (SUB, 1), 0)
    for a in range(c // SUB):
        lo = a * SUB
        rows = slice(lo, lo + SUB)
        qa, gca = q[rows], gc[rows]
        edge = (gc[lo + SUB - 1:lo + SUB] - g[lo + SUB - 1:lo + SUB]) if rev else (gc[lo:lo + 1] - g[lo:lo + 1])
        k_far = (k * jnp.exp(jnp.minimum(edge - gc, 0.0))).astype(BF16)
        att = lax.dot_general((qa * jnp.exp(gca - edge)).astype(BF16), k_far, NT_DIMS,
                              preferred_element_type=F32)
        att = jnp.where((lane >= lo + SUB) if rev else (lane < lo), att, 0.0)
        for jj in range(SUB):
            j = lo + jj
            ok = (rsub <= jj) if rev else (rsub >= jj)
            e = jnp.exp(jnp.where(ok, gca - gc[j:j + 1], -jnp.inf))
            col = jnp.sum(qa * k[j:j + 1] * e, axis=-1, keepdims=True)
            att = jnp.where(lane == j, col, att)
        o_ref[rows, :] = o_inter[rows] + jnp.dot(att.astype(BF16), vb, preferred_element_type=F32)
    k_dec = (k * jnp.exp(tot - gc)).astype(BF16)
    s_ref[...] = s * jnp.exp(tot) + lax.dot_general(vb, k_dec, TN_DIMS, preferred_element_type=F32)

    @pl.when(step == n_steps - 1)
    def _():
        sf_ref[...] = s_ref[...]


def gla_scan(p, pg, wg_pad, bg, s0, *, d, dk, dv):
    r = p.shape[0]
    n_steps = r // CHUNK
    idx = (lambda i: n_steps - 1 - i) if d else (lambda i: i)
    qk_blocks = GLA_HEADS
    v_block0 = 2 * GLA_HEADS * dk // dv
    state = pl.BlockSpec((None, dv, dk), lambda h, i: (h, 0, 0))
    return pl.pallas_call(
        functools.partial(_gla_kernel, rev=bool(d), n_steps=n_steps, dk=dk),
        grid=(GLA_HEADS, n_steps),
        in_specs=[pl.BlockSpec((CHUNK, dk), lambda h, i: (idx(i), h)),
                  pl.BlockSpec((CHUNK, dk), lambda h, i: (idx(i), qk_blocks + h)),
                  pl.BlockSpec((CHUNK, dv), lambda h, i: (idx(i), v_block0 + h)),
                  pl.BlockSpec((CHUNK, LANES), lambda h, i: (idx(i), 0)),
                  pl.BlockSpec((LANES, dk), lambda h, i: (0, h)),
                  pl.BlockSpec((1, dk), lambda h, i: (0, h)),
                  state],
        out_specs=[pl.BlockSpec((CHUNK, dv), lambda h, i: (idx(i), h)), state],
        out_shape=[jax.ShapeDtypeStruct((r, GLA_HEADS * dv), F32),
                   jax.ShapeDtypeStruct((GLA_HEADS, dv, dk), F32)],
        scratch_shapes=[pltpu.VMEM((dv, dk), F32)],
        compiler_params=_params("parallel", "arbitrary"), name="gla_scan",
    )(p, p, p, pg, wg_pad, bg, s0)


def _ffn1_kernel(be_ref, x_ref, wg_ref, wu_ref, o_ref):
    x = x_ref[...]
    a = jnp.dot(x, wg_ref[...], preferred_element_type=F32)
    b = jnp.dot(x, wu_ref[...], preferred_element_type=F32)
    o_ref[...] = (_silu(a) * b).astype(o_ref.dtype)


def ffn_up(x, w_gate, w_up, block_expert, *, tn=1408):
    r, d = x.shape
    hidden = w_gate.shape[2]
    tm = min(MOE_ROWS, r)
    wspec = pl.BlockSpec((None, d, tn), lambda j, i, be: (be[i], 0, j))
    return pl.pallas_call(
        _ffn1_kernel,
        grid_spec=pltpu.PrefetchScalarGridSpec(
            num_scalar_prefetch=1, grid=(hidden // tn, r // tm),
            in_specs=[pl.BlockSpec((tm, d), lambda j, i, be: (i, 0)), wspec, wspec],
            out_specs=pl.BlockSpec((tm, tn), lambda j, i, be: (i, j))),
        out_shape=jax.ShapeDtypeStruct((r, hidden), BF16),
        compiler_params=_params("parallel", "arbitrary"), name="ffn_up",
    )(block_expert, x, w_gate, w_up)


def _ffn2_res_kernel(be_ref, h_ref, w_ref, x_ref, gt_ref, o_ref):
    y = jnp.dot(h_ref[...], w_ref[...], preferred_element_type=F32)
    o_ref[...] = x_ref[...] + gt_ref[...] * y


def _ffn2_scale_kernel(be_ref, h_ref, w_ref, sw_ref, o_ref):
    y = jnp.dot(h_ref[...], w_ref[...], preferred_element_type=F32)
    o_ref[...] = (y * sw_ref[...]).astype(o_ref.dtype)


def ffn_down_residual(h, w_down, block_expert, x, gate, *, tn=512):
    r, hidden = h.shape
    d = w_down.shape[2]
    tm = min(MOE_ROWS, r)
    return pl.pallas_call(
        _ffn2_res_kernel,
        grid_spec=pltpu.PrefetchScalarGridSpec(
            num_scalar_prefetch=1, grid=(d // tn, r // tm),
            in_specs=[pl.BlockSpec((tm, hidden), lambda j, i, be: (i, 0)),
                      pl.BlockSpec((None, hidden, tn), lambda j, i, be: (be[i], 0, j)),
                      pl.BlockSpec((tm, tn), lambda j, i, be: (i, j)),
                      pl.BlockSpec((1, tn), lambda j, i, be: (0, j))],
            out_specs=pl.BlockSpec((tm, tn), lambda j, i, be: (i, j))),
        out_shape=jax.ShapeDtypeStruct((r, d), F32),
        compiler_params=_params("parallel", "arbitrary"), name="ffn_down_residual",
    )(block_expert, h, w_down, x, gate.reshape(1, d))


def ffn_down_scaled(h, w_down, block_expert, slot_w, *, tn=512):
    r, hidden = h.shape
    d = w_down.shape[2]
    tm = min(MOE_ROWS, r)
    return pl.pallas_call(
        _ffn2_scale_kernel,
        grid_spec=pltpu.PrefetchScalarGridSpec(
            num_scalar_prefetch=1, grid=(d // tn, r // tm),
            in_specs=[pl.BlockSpec((tm, hidden), lambda j, i, be: (i, 0)),
                      pl.BlockSpec((None, hidden, tn), lambda j, i, be: (be[i], 0, j)),
                      pl.BlockSpec((tm, 1), lambda j, i, be: (i, 0))],
            out_specs=pl.BlockSpec((tm, tn), lambda j, i, be: (i, j))),
        out_shape=jax.ShapeDtypeStruct((r, d), BF16),
        compiler_params=_params("parallel", "arbitrary"), name="ffn_down_scaled",
    )(block_expert, h, w_down, slot_w.reshape(r, 1))


def _router_kernel(x_ref, g_ref, sh_ref, sc_ref, rw_ref, rb_ref, h_ref, idx_ref, wt_ref):
    x = x_ref[...]
    y = x * lax.rsqrt(jnp.mean(x * x, axis=-1, keepdims=True) + EPS)
    h = (y * g_ref[...]) * (1.0 + sc_ref[...]) + sh_ref[...]
    hb = h.astype(BF16)
    h_ref[...] = hb
    logits = jnp.dot(hb, rw_ref[...], preferred_element_type=F32) + rb_ref[...]
    lane = lax.broadcasted_iota(I32, logits.shape, 1)
    neg = jnp.float32(-jnp.inf)
    logits = jnp.where(lane < N_EXPERTS, logits, neg)
    m0 = jnp.max(logits, axis=-1, keepdims=True)
    i0 = jnp.min(jnp.where(logits == m0, lane, LANES), axis=-1, keepdims=True)
    rest = jnp.where(lane == i0, neg, logits)
    m1 = jnp.max(rest, axis=-1, keepdims=True)
    i1 = jnp.min(jnp.where(rest == m1, lane, LANES), axis=-1, keepdims=True)
    e1 = jnp.exp(m1 - m0)
    w0 = 1.0 / (1.0 + e1)
    idx_ref[...] = jnp.where(lane == 0, i0, jnp.where(lane == 1, i1, 0))
    wt_ref[...] = jnp.where(lane == 0, w0, jnp.where(lane == 1, e1 * w0, 0.0))


def route(x, g, sh, sc, router_w, router_b):
    t, d = x.shape
    tm = min(ROW_TILE, t)
    vec = pl.BlockSpec((1, d), lambda i: (0, 0))
    row = lambda w: pl.BlockSpec((tm, w), lambda i: (i, 0))
    rw = jnp.zeros((d, LANES), BF16).at[:, :N_EXPERTS].set(router_w.astype(BF16))
    rb = jnp.zeros((1, LANES), F32).at[0, :N_EXPERTS].set(router_b)
    h, idx, wt = pl.pallas_call(
        _router_kernel, grid=(t // tm,),
        in_specs=[row(d), vec, vec, vec, pl.BlockSpec((d, LANES), lambda i: (0, 0)),
                  pl.BlockSpec((1, LANES), lambda i: (0, 0))],
        out_specs=[row(d), row(LANES), row(LANES)],
        out_shape=[jax.ShapeDtypeStruct((t, d), BF16), jax.ShapeDtypeStruct((t, LANES), I32),
                   jax.ShapeDtypeStruct((t, LANES), F32)],
        compiler_params=_params("parallel"), name="route",
    )(x, g.reshape(1, d), sh.reshape(1, d), sc.reshape(1, d), rw, rb)
    return h, idx[:, :2], wt[:, :2]


def _gather_kernel(pb_ref, pt_ref, pf_ref, h_ref, st_ref, o_ref, acc_ref):
    i = pl.program_id(0)
    flags = pf_ref[i]
    tm = h_ref.shape[0]

    @pl.when((flags & 1) != 0)
    def _():
        acc_ref[...] = jnp.zeros_like(acc_ref)

    @pl.when((flags & 4) != 0)
    def _():
        tok = pt_ref[i] * tm + lax.broadcasted_iota(I32, (st_ref.shape[0], tm), 1)
        onehot = (st_ref[...] == tok).astype(BF16)
        acc_ref[...] += jnp.dot(onehot, h_ref[...], preferred_element_type=F32)

    @pl.when((flags & 2) != 0)
    def _():
        o_ref[...] = acc_ref[...].astype(o_ref.dtype)


def moe_gather(h, slot_tok, pair_block, pair_tile, pair_flags):
    t, d = h.shape
    n_slots = slot_tok.shape[0]
    tm = min(ROW_TILE, t)
    n_pairs = pair_block.shape[0]
    return pl.pallas_call(
        _gather_kernel,
        grid_spec=pltpu.PrefetchScalarGridSpec(
            num_scalar_prefetch=3, grid=(n_pairs,),
            in_specs=[pl.BlockSpec((tm, d), lambda i, pb, pt, pf: (pt[i], 0)),
                      pl.BlockSpec((MOE_ROWS, 1), lambda i, pb, pt, pf: (pb[i], 0))],
            out_specs=pl.BlockSpec((MOE_ROWS, d), lambda i, pb, pt, pf: (pb[i], 0)),
            scratch_shapes=[pltpu.VMEM((MOE_ROWS, d), F32)]),
        out_shape=jax.ShapeDtypeStruct((n_slots, d), BF16),
        compiler_params=_params("arbitrary"), name="moe_gather",
    )(pair_block, pair_tile, pair_flags, h, slot_tok.reshape(n_slots, 1))


def _combine_kernel(pb_ref, pt_ref, pf_ref, yb_ref, dest_ref, x_ref, gt_ref, o_ref, acc_ref):
    i = pl.program_id(0)
    flags = pf_ref[i]

    @pl.when((flags & 1) != 0)
    def _():
        acc_ref[...] = jnp.zeros_like(acc_ref)

    @pl.when((flags & 4) != 0)
    def _():
        slot = pb_ref[i] * MOE_ROWS + lax.broadcasted_iota(I32, (dest_ref.shape[0], MOE_ROWS), 1)
        hit = (dest_ref[:, 0:1] == slot) | (dest_ref[:, 1:2] == slot)
        acc_ref[...] += jnp.dot(hit.astype(BF16), yb_ref[...], preferred_element_type=F32)

    @pl.when((flags & 2) != 0)
    def _():
        o_ref[...] = x_ref[...] + gt_ref[...] * acc_ref[...]


def moe_combine(yb, dest, pair_block, pair_tile, pair_flags, x, gate):
    t, d = x.shape
    tm = min(ROW_TILE, t)
    n_pairs = pair_block.shape[0]
    return pl.pallas_call(
        _combine_kernel,
        grid_spec=pltpu.PrefetchScalarGridSpec(
            num_scalar_prefetch=3, grid=(n_pairs,),
            in_specs=[pl.BlockSpec((MOE_ROWS, d), lambda i, pb, pt, pf: (pb[i], 0)),
                      pl.BlockSpec((tm, 2), lambda i, pb, pt, pf: (pt[i], 0)),
                      pl.BlockSpec((tm, d), lambda i, pb, pt, pf: (pt[i], 0)),
                      pl.BlockSpec((1, d), lambda i, pb, pt, pf: (0, 0))],
            out_specs=pl.BlockSpec((tm, d), lambda i, pb, pt, pf: (pt[i], 0)),
            scratch_shapes=[pltpu.VMEM((tm, d), F32)]),
        out_shape=jax.ShapeDtypeStruct((t, d), F32),
        compiler_params=_params("arbitrary"), name="moe_combine",
    )(pair_block, pair_tile, pair_flags, yb, dest, x, gate.reshape(1, d))


def _pair_lists(lo, hi, nonempty, n_pairs):
    cnt = jnp.where(nonempty, hi - lo + 1, 1)
    end = jnp.cumsum(cnt)
    start = end - cnt
    i = jnp.arange(n_pairs, dtype=I32)
    ic = jnp.minimum(i, end[-1] - 1)
    grp = jnp.searchsorted(end, ic, side="right").astype(I32)
    off = ic - start[grp]
    member = jnp.where(nonempty[grp], lo[grp] + off, 0).astype(I32)
    return grp, member, off == 0, off == cnt[grp] - 1, nonempty[grp], i < end[-1]


def _pair_flags(first, last, data, valid):
    flags = jnp.where(first, 1, 0) | jnp.where(last, 2, 0) | jnp.where(data, 4, 0)
    return jnp.where(valid, flags, 0).astype(I32)


def moe_plan(top_idx, top_w, n_tok):
    tm = min(ROW_TILE, n_tok)
    n_tiles = n_tok // tm
    n_assign = 2 * n_tok
    n_blocks = n_assign // MOE_ROWS + N_EXPERTS
    n_slots = n_blocks * MOE_ROWS
    e_flat = top_idx.reshape(-1)
    onehot = (e_flat[:, None] == jnp.arange(N_EXPERTS, dtype=I32)[None, :]).astype(I32)
    cum = jnp.cumsum(onehot, axis=0)
    rank = jnp.sum((cum - onehot) * onehot, axis=1)
    counts = cum[-1]
    padded = (counts + MOE_ROWS - 1) // MOE_ROWS * MOE_ROWS
    pad_end = jnp.cumsum(padded)
    pad_start = pad_end - padded
    dest = (pad_start[e_flat] + rank).astype(I32)
    tok_flat = jnp.arange(n_assign, dtype=I32) // 2
    slot_tok = jnp.full((n_slots,), n_tok, I32).at[dest].set(tok_flat)
    slot_w = jnp.zeros((n_slots,), F32).at[dest].set(top_w.reshape(-1))
    block_expert = jnp.minimum(
        jnp.searchsorted(pad_end, jnp.arange(n_blocks, dtype=I32) * MOE_ROWS, side="right"), N_EXPERTS - 1).astype(I32)
    n_pairs = N_EXPERTS * n_tiles + n_blocks
    st = slot_tok.reshape(n_blocks, MOE_ROWS)
    real = st < n_tok
    t_lo = jnp.min(jnp.where(real, st, n_tok), axis=1) // tm
    t_hi = jnp.max(jnp.where(real, st, -1), axis=1) // tm
    g_block, g_tile, *g_bits = _pair_lists(t_lo.astype(I32), t_hi.astype(I32), jnp.any(real, axis=1), n_pairs)
    g_flags = _pair_flags(*g_bits)
    d_tile = dest.reshape(n_tiles, 2 * tm)
    e_tile = e_flat.reshape(n_tiles, 2 * tm)
    sel = e_tile[:, :, None] == jnp.arange(N_EXPERTS, dtype=I32)[None, None, :]
    b_lo = jnp.min(jnp.where(sel, d_tile[:, :, None], n_slots), axis=1) // MOE_ROWS
    b_hi = jnp.max(jnp.where(sel, d_tile[:, :, None], -1), axis=1) // MOE_ROWS
    c_grp, c_block, first, last, data, valid = _pair_lists(
        b_lo.reshape(-1).astype(I32), b_hi.reshape(-1).astype(I32), jnp.any(sel, axis=1).reshape(-1), n_pairs)
    c_flags = _pair_flags(first & (c_grp % N_EXPERTS == 0), last & (c_grp % N_EXPERTS == N_EXPERTS - 1), data, valid)
    return dict(dest=dest.reshape(n_tok, 2), slot_tok=slot_tok, slot_w=slot_w, block_expert=block_expert,
                gather=(g_block, g_tile, g_flags), combine=(c_block, c_grp // N_EXPERTS, c_flags))


def _mods(mod_all, layer, row, d):
    m = mod_all[layer, row]
    return tuple(m[j * d:(j + 1) * d] for j in range(N_MOD))


def _even_layer(x_lat, x_ctx, mods_lat, mods_ctx, norm1_g, norm2_g, w):
    d = x_lat.shape[1]
    hw = DN_HEADS * DN_DH
    states = dict(dn=[jnp.zeros((DN_HEADS, DN_DH, DN_DH), F32)] * 2, lru=[jnp.zeros((8, hw), F32)] * 2)
    outs = []
    for x, mods in ((x_ctx, mods_ctx), (x_lat, mods_lat)):
        sh1, sc1, gt1, sh2, sc2, gt2 = mods
        h = norm_mod(x, norm1_g, sh1, sc1)
        p = matmul(h, w["w_main"], tn=1024)
        pg = matmul(h, w["w_gate"], tn=LANES)
        q, k, v, xc, gb = even_prep(p, pg, w["conv_w"], w["conv_b"], w["a_log"], w["dt_bias"])
        u, wm, qd, kd, qk, gl = deltanet_prep(q, k, v, gb)
        o, hl = [], []
        for dirn in range(2):
            o_d, states["dn"][dirn] = deltanet_scan(u, wm, qd, kd, qk, gl, states["dn"][dirn], d=dirn)
            h_d, states["lru"][dirn] = lru_scan(xc, w["lru_wa"][dirn], w["lru_wx"][dirn], w["lru_ba"][dirn],
                                                w["lru_bx"][dirn], w["lru_lam"][dirn], states["lru"][dirn], d=dirn)
            o.append(o_d)
            hl.append(h_d)
        x = even_finish(o[0], o[1], hl[0], hl[1], p, w["dn_norm_g"], w["w_out"], x, gt1)
        h2 = norm_mod(x, norm2_g, sh2, sc2)
        be = jnp.zeros((h2.shape[0] // min(MOE_ROWS, h2.shape[0]),), I32)
        hh = ffn_up(h2, w["ffn_gate"], w["ffn_up"], be)
        x = ffn_down_residual(hh, w["ffn_down"], be, x, gt2)
        outs.append(x)
    return outs[1], outs[0]


def _odd_layer_last(x_lat, x_ctx, mods_lat, mods_ctx, norm1_g, norm2_g, w):
    d = x_lat.shape[1]
    dk, dv = w["dk"], w["dv"]
    states = [jnp.zeros((GLA_HEADS, dv, dk), F32)] * 2
    sh1, sc1 = mods_ctx[0], mods_ctx[1]
    h = norm_mod(x_ctx, norm1_g, sh1, sc1)
    p = matmul(h, w["w_main"], tn=1024)
    pg = matmul(h, w["w_gate"], tn=LANES, out_dtype=F32)
    for dirn in range(2):
        _, states[dirn] = gla_scan(p, pg, w["wg_pad"][dirn], w["bg"][dirn], states[dirn], d=dirn, dk=dk, dv=dv)
    sh1, sc1, gt1, sh2, sc2, gt2 = mods_lat
    h = norm_mod(x_lat, norm1_g, sh1, sc1, raster=True)
    p = matmul(h, w["w_main"], tn=1024)
    pg = matmul(h, w["w_gate"], tn=LANES, out_dtype=F32)
    o = [gla_scan(p, pg, w["wg_pad"][dirn], w["bg"][dirn], states[dirn], d=dirn, dk=dk, dv=dv)[0]
         for dirn in range(2)]
    x = odd_finish(o[0], o[1], p, w["gla_norm_g"], w["w_out"], x_lat, gt1)
    n_tok = x.shape[0]
    h2, top_idx, top_w = route(x, norm2_g, sh2, sc2, w["router_w"], w["router_b"])
    plan = moe_plan(top_idx, top_w, n_tok)
    xb = moe_gather(h2, plan["slot_tok"], *plan["gather"])
    hh = ffn_up(xb, w["exp_gate"], w["exp_up"], plan["block_expert"])
    yb = ffn_down_scaled(hh, w["exp_down"], plan["block_expert"], plan["slot_w"])
    return moe_combine(yb, plan["dest"], *plan["combine"], x, gt2)


def kernel(x, c, ctx, c_ctx, mod_w, mod_b, norm1_g, norm2_g, ev_w_in, ev_conv_qkv, ev_dn_a_log, ev_dn_dt_bias,
           ev_dn_norm_g, ev_lru_conv_w, ev_lru_conv_b, ev_lru_wa, ev_lru_ba, ev_lru_wx, ev_lru_bx, ev_lru_lambda,
           ev_w_out, ev_ffn_w_gate, ev_ffn_w_up, ev_ffn_w_down, od_w_in, od_gla_wg2, od_gla_bg, od_gla_norm_g,
           od_w_out, od_router_w, od_router_b, od_exp_w_gate, od_exp_w_up, od_exp_w_down, final_norm_g):
    b_, length, d = x.shape
    assert b_ == 1 and mod_w.shape[0] == 2, "this kernel implements the batch-1, depth-2 configuration"
    hw = DN_HEADS * DN_DH
    x_lat, x_ctx = x[0], ctx[0]

    cond8 = jnp.zeros((8, d), F32).at[0].set(c[0]).at[1].set(c_ctx)
    mod_all = adaln_all(cond8, mod_w, mod_b)

    w_in = ev_w_in[0]
    qkv_w, z0, ab0, xr0, gr0 = 3 * hw, 3 * hw, 4 * hw, 4 * hw + 4 * DN_HEADS, 5 * hw + 4 * DN_HEADS
    w_main = jnp.concatenate([w_in[:, :qkv_w], w_in[:, xr0:xr0 + hw], w_in[:, z0:z0 + hw], w_in[:, gr0:gr0 + hw]],
                             axis=1).astype(BF16)
    w_gate = jnp.zeros((d, LANES), BF16).at[:, :4 * DN_HEADS].set(w_in[:, ab0:ab0 + 4 * DN_HEADS].astype(BF16))
    pad16 = lambda t: jnp.zeros((1, LANES), F32).at[0, :2 * DN_HEADS].set(t.reshape(-1))
    ev = dict(
        w_main=w_main, w_gate=w_gate,
        conv_w=jnp.concatenate([ev_conv_qkv[0], ev_lru_conv_w[0]], axis=1),
        conv_b=jnp.concatenate([jnp.zeros((qkv_w,), F32), ev_lru_conv_b[0]]).reshape(1, -1),
        a_log=pad16(ev_dn_a_log[0]), dt_bias=pad16(ev_dn_dt_bias[0]),
        dn_norm_g=ev_dn_norm_g[0],
        lru_wa=ev_lru_wa[0].astype(BF16), lru_wx=ev_lru_wx[0].astype(BF16),
        lru_ba=ev_lru_ba[0].reshape(2, 1, hw), lru_bx=ev_lru_bx[0].reshape(2, 1, hw),
        lru_lam=ev_lru_lambda[0].reshape(2, 1, hw),
        w_out=ev_w_out[0].astype(BF16),
        ffn_gate=ev_ffn_w_gate.astype(BF16), ffn_up=ev_ffn_w_up.astype(BF16), ffn_down=ev_ffn_w_down.astype(BF16),
    )
    x_lat, x_ctx = _even_layer(x_lat, x_ctx, _mods(mod_all, 0, 0, d), _mods(mod_all, 0, 1, d),
                               norm1_g[0], norm2_g[0], ev)

    w_in = od_w_in[0]
    qk_w = od_gla_wg2.shape[-1]
    v_w = od_w_out.shape[1]
    main_w = 2 * qk_w + 2 * v_w
    wg_pad = jnp.zeros((2, LANES, qk_w), F32)
    for dirn in range(2):
        wg_pad = wg_pad.at[dirn, dirn * GLA_RANK:(dirn + 1) * GLA_RANK].set(od_gla_wg2[0, dirn])
    od = dict(
        w_main=w_in[:, :main_w].astype(BF16),
        w_gate=jnp.zeros((d, LANES), BF16).at[:, :2 * GLA_RANK].set(w_in[:, main_w:].astype(BF16)),
        wg_pad=wg_pad, bg=od_gla_bg[0].reshape(2, 1, qk_w), gla_norm_g=od_gla_norm_g[0],
        w_out=od_w_out[0].astype(BF16), router_w=od_router_w[0], router_b=od_router_b[0],
        exp_gate=od_exp_w_gate[0].astype(BF16), exp_up=od_exp_w_up[0].astype(BF16),
        exp_down=od_exp_w_down[0].astype(BF16),
        dk=qk_w // GLA_HEADS, dv=v_w // GLA_HEADS,
    )
    x_lat = _odd_layer_last(x_lat, x_ctx, _mods(mod_all, 1, 0, d), _mods(mod_all, 1, 1, d),
                            norm1_g[1], norm2_g[1], od)
    return final_norm(x_lat, final_norm_g)[None]
```

```python
import functools
import math

import jax
import jax.numpy as jnp
from jax import lax
from jax.experimental import pallas as pl
from jax.experimental.pallas import tpu as pltpu

F32 = jnp.float32
BF16 = jnp.bfloat16
I32 = jnp.int32
HI = lax.Precision.HIGHEST

EPS = 1e-6
N_MOD = 6
GRID_W = 64
CHUNK = 64
SUB = 8
LANES = 128
ROW_TILE = 256
DN_HEADS = 8
DN_DH = 128
LRU_BLOCKS = 8
LRU_BW = 128
LRU_C = 8.0
GLA_HEADS = 4
GLA_RANK = 16
GLA_TAU = 16.0
N_EXPERTS = 8
MOE_ROWS = 256
VMEM_LIMIT = 56 * 1024 * 1024

NT_DIMS = (((1,), (1,)), ((), ()))
TN_DIMS = (((0,), (0,)), ((), ()))


def _params(*sem):
    return pltpu.CompilerParams(dimension_semantics=sem, vmem_limit_bytes=VMEM_LIMIT)


def _softplus(x):
    return jnp.maximum(x, 0.0) + jnp.log1p(jnp.exp(-jnp.abs(x)))


def _silu(x):
    return x * jax.nn.sigmoid(x)


def _bdot(a, b):
    return jnp.dot(a.astype(BF16), b.astype(BF16), preferred_element_type=F32)


def _adaln_kernel(cond_ref, w_ref, b_ref, o_ref):
    s = _silu(cond_ref[...])
    o_ref[...] = jnp.dot(s, w_ref[...], precision=HI, preferred_element_type=F32) + b_ref[...]


def adaln_all(cond8, mod_w, mod_b):
    n_layers, d, n6 = mod_w.shape
    tn = 1024
    return pl.pallas_call(
        _adaln_kernel,
        grid=(n_layers, n6 // tn),
        in_specs=[pl.BlockSpec((8, d), lambda l, j: (0, 0)),
                  pl.BlockSpec((None, d, tn), lambda l, j: (l, 0, j)),
                  pl.BlockSpec((None, 1, tn), lambda l, j: (l, 0, j))],
        out_specs=pl.BlockSpec((None, 8, tn), lambda l, j: (l, 0, j)),
        out_shape=jax.ShapeDtypeStruct((n_layers, 8, n6), F32),
        compiler_params=_params("arbitrary", "arbitrary"),
        name="adaln",
    )(cond8, mod_w, mod_b.reshape(n_layers, 1, n6))


def _norm_mod_kernel(x_ref, g_ref, sh_ref, sc_ref, o_ref):
    x = x_ref[...]
    y = x * lax.rsqrt(jnp.mean(x * x, axis=-1, keepdims=True) + EPS)
    o_ref[...] = ((y * g_ref[...]) * (1.0 + sc_ref[...]) + sh_ref[...]).astype(o_ref.dtype)


def _raster_spec(rows, d):
    return pl.BlockSpec((rows, d), lambda c: (0, c))


def norm_mod(x, g, sh, sc, *, raster=False, out_dtype=BF16):
    n, d = x.shape
    vec = pl.BlockSpec((1, d), lambda i: (0, 0))
    if raster:
        rows = n // GRID_W
        x_in, x_spec, grid = x.reshape(rows, GRID_W * d), _raster_spec(rows, d), (GRID_W,)
        o_spec = pl.BlockSpec((rows, d), lambda c: (c, 0))
    else:
        tm = min(ROW_TILE, n)
        x_in, x_spec, grid = x, pl.BlockSpec((tm, d), lambda i: (i, 0)), (n // tm,)
        o_spec = pl.BlockSpec((tm, d), lambda i: (i, 0))
    return pl.pallas_call(
        _norm_mod_kernel, grid=grid, in_specs=[x_spec, vec, vec, vec], out_specs=o_spec,
        out_shape=jax.ShapeDtypeStruct((n, d), out_dtype),
        compiler_params=_params("parallel"), name="norm_mod",
    )(x_in, g.reshape(1, d), sh.reshape(1, d), sc.reshape(1, d))


def _final_norm_kernel(x_ref, g_ref, o_ref):
    x = x_ref[...]
    o_ref[...] = x * lax.rsqrt(jnp.mean(x * x, axis=-1, keepdims=True) + EPS) * g_ref[...]


def final_norm(x, g):
    n, d = x.shape
    tm = min(ROW_TILE, n)
    return pl.pallas_call(
        _final_norm_kernel, grid=(n // tm,),
        in_specs=[pl.BlockSpec((tm, d), lambda i: (i, 0)), pl.BlockSpec((1, d), lambda i: (0, 0))],
        out_specs=pl.BlockSpec((tm, d), lambda i: (i, 0)),
        out_shape=jax.ShapeDtypeStruct((n, d), F32),
        compiler_params=_params("parallel"), name="final_norm",
    )(x, g.reshape(1, d))


def _mm_kernel(x_ref, w_ref, o_ref):
    o_ref[...] = jnp.dot(x_ref[...], w_ref[...], preferred_element_type=F32).astype(o_ref.dtype)


def _dense_rows(r):
    return 2 * ROW_TILE if r % (2 * ROW_TILE) == 0 else min(ROW_TILE, r)


def matmul(x, w, *, tn, out_dtype=F32):
    r, k = x.shape
    n = w.shape[1]
    tm = _dense_rows(r)
    return pl.pallas_call(
        _mm_kernel, grid=(n // tn, r // tm),
        in_specs=[pl.BlockSpec((tm, k), lambda j, i: (i, 0)),
                  pl.BlockSpec((k, tn), lambda j, i: (0, j))],
        out_specs=pl.BlockSpec((tm, tn), lambda j, i: (i, j)),
        out_shape=jax.ShapeDtypeStruct((r, n), out_dtype),
        compiler_params=_params("parallel", "parallel"), name="matmul",
    )(x, w)


def _evprep_kernel(p_ref, prev_ref, next_ref, pg_ref, cw_ref, cb_ref, alog_ref, dtb_ref,
                   q_ref, k_ref, v_ref, xc_ref, gx_ref, gxt_ref, ext_ref, *, n_tiles):
    i = pl.program_id(0)
    tm, width = p_ref.shape
    ext_ref[8:8 + tm, :] = p_ref[...]
    ext_ref[0:8, :] = jnp.where(i > 0, prev_ref[...], 0.0)
    ext_ref[8 + tm:16 + tm, :] = jnp.where(i < n_tiles - 1, next_ref[...], 0.0)
    acc = ext_ref[pl.ds(6, tm), :] * cw_ref[0:1, :]
    for j in range(1, 4):
        acc = acc + ext_ref[pl.ds(6 + j, tm), :] * cw_ref[j:j + 1, :]
    acc = acc + cb_ref[...]
    qk_w = DN_HEADS * DN_DH
    for hd in range(DN_HEADS):
        for part, ref, scale in ((0, q_ref, DN_DH ** -0.5), (1, k_ref, 1.0)):
            lo = part * qk_w + hd * DN_DH
            t = _silu(acc[:, lo:lo + DN_DH])
            t = t * lax.rsqrt(jnp.sum(t * t, axis=-1, keepdims=True) + EPS)
            ref[:, hd * DN_DH:(hd + 1) * DN_DH] = t * scale
    v_ref[...] = _silu(acc[:, 2 * qk_w:3 * qk_w])
    xc_ref[...] = acc[:, 3 * qk_w:]
    pg = pg_ref[...]
    lane = lax.broadcasted_iota(I32, pg.shape, 1)
    g = -jnp.exp(alog_ref[...]) * _softplus(pg + dtb_ref[...])
    ri = lax.broadcasted_iota(I32, (tm, tm), 0)
    ci = lax.broadcasted_iota(I32, (tm, tm), 1)
    same = (ri // CHUNK) == (ci // CHUNK)
    cum_f = _dot01((same & (ci <= ri)).astype(BF16), g)
    cum_b = _dot01((same & (ci >= ri)).astype(BF16), g)
    tot = _dot01(same.astype(BF16), g)
    gc = jnp.where(lane < DN_HEADS, cum_f, cum_b)
    gx = jnp.where(lane < 2 * DN_HEADS, gc,
                   jnp.where(lane < 4 * DN_HEADS, jax.nn.sigmoid(pg), pltpu.roll(tot, 4 * DN_HEADS, 1)))
    gx_ref[...] = gx
    gxt_ref[...] = gx.T


def _dot01(m01, x):
    x1 = x.astype(BF16)
    r1 = x - x1.astype(F32)
    x2 = r1.astype(BF16)
    x3 = (r1 - x2.astype(F32)).astype(BF16)
    dot = lambda t: jnp.dot(m01, t, preferred_element_type=F32)
    return dot(x1) + dot(x2) + dot(x3)


def even_prep(p, pg, conv_w, conv_b, a_log, dt_bias):
    r = p.shape[0]
    width = conv_w.shape[1]
    tm = min(ROW_TILE, r)
    n_tiles = r // tm
    hb = tm // 8
    out_w = DN_HEADS * DN_DH
    row = lambda w: pl.BlockSpec((tm, w), lambda i: (i, 0))
    vec = lambda w: pl.BlockSpec((1, w), lambda i: (0, 0))
    return pl.pallas_call(
        functools.partial(_evprep_kernel, n_tiles=n_tiles),
        grid=(n_tiles,),
        in_specs=[row(width),
                  pl.BlockSpec((8, width), lambda i: (jnp.maximum(i * hb - 1, 0), 0)),
                  pl.BlockSpec((8, width), lambda i: (jnp.minimum((i + 1) * hb, r // 8 - 1), 0)),
                  row(LANES),
                  pl.BlockSpec((4, width), lambda i: (0, 0)), vec(width), vec(LANES), vec(LANES)],
        out_specs=[row(out_w), row(out_w), row(out_w), row(out_w), row(LANES),
                   pl.BlockSpec((LANES, tm), lambda i: (0, i))],
        out_shape=[jax.ShapeDtypeStruct((r, out_w), F32)] * 4
        + [jax.ShapeDtypeStruct((r, LANES), F32), jax.ShapeDtypeStruct((LANES, r), F32)],
        scratch_shapes=[pltpu.VMEM((tm + 16, width), F32)],
        compiler_params=_params("parallel"), name="even_prep",
    )(p, p, p, pg, conv_w, conv_b, a_log, dt_bias)


DN_BASE = 16
DN_HEADS_PER_STEP = 2


def _dnprep_kernel(q_ref, k_ref, v_ref, gx_ref, gxt_ref, u_ref, w_ref, qd_ref, kd_ref, qk_ref, gl_ref, *, n_chunks):
    c = CHUNK
    tm = q_ref.shape[0]
    ri = lax.broadcasted_iota(I32, (tm, tm), 0)
    ci = lax.broadcasted_iota(I32, (tm, tm), 1)
    same = lambda s: (ri // s) == (ci // s)
    eye = (ri == ci).astype(F32)
    lane = lax.broadcasted_iota(I32, (tm, LANES), 1)
    gx = gx_ref[...]
    pick = lambda idx: jnp.sum(jnp.where(lane == idx, gx, 0.0), axis=1, keepdims=True)
    chains = []
    for hh in range(DN_HEADS_PER_STEP):
        hd = pl.program_id(1) * DN_HEADS_PER_STEP + hh
        cols = slice(hh * DN_DH, (hh + 1) * DN_DH)
        q = q_ref[:, cols]
        k = k_ref[:, cols]
        kb16 = k.astype(BF16)
        gram_k = lax.dot_general(kb16, kb16, NT_DIMS, preferred_element_type=F32)
        gram_q = lax.dot_general(q.astype(BF16), kb16, NT_DIMS, preferred_element_type=F32)
        for d in range(2):
            incl = same(c) & ((ci >= ri) if d else (ci <= ri))
            gcol = pick(d * DN_HEADS + hd)
            bcol = pick((2 + d) * DN_HEADS + hd)
            tcol = pick((4 + d) * DN_HEADS + hd)
            grow = gxt_ref[pl.ds(d * DN_HEADS + hd, 1), :]
            decay = jnp.where(incl, jnp.exp(gcol - grow), 0.0)
            a = jnp.where(ri == ci, 0.0, gram_k * bcol * decay)
            eg = jnp.exp(gcol)
            qk = (gram_q * decay).astype(BF16)
            for n in range(n_chunks):
                qk_ref[d, hh, n * c:(n + 1) * c, :] = qk[n * c:(n + 1) * c, n * c:(n + 1) * c]
                gl_ref[d, hh, n] = jnp.broadcast_to(jnp.exp(tcol[n * c:n * c + 1, :]), (8, LANES))
            qd_ref[d, :, cols] = (q * eg).astype(BF16)
            kd_ref[d, :, cols] = (k * jnp.exp(tcol - gcol)).astype(BF16)
            diag = jnp.where(same(DN_BASE), a, 0.0)
            chains.append(dict(d=d, cols=cols, a=a, t=eye - diag, p=diag, scale=bcol, scale_k=bcol * eg))
    size = 2
    while size < DN_BASE:
        for ch in chains:
            ch["p"] = _bdot(ch["p"], ch["p"])
        for ch in chains:
            ch["t"] = ch["t"] + _bdot(ch["t"], ch["p"])
        size *= 2
    size = DN_BASE
    while size < c:
        couple = same(2 * size) & ~same(size)
        for ch in chains:
            ch["et"] = _bdot(jnp.where(couple, ch["a"], 0.0), ch["t"])
        for ch in chains:
            ch["t"] = ch["t"] - _bdot(ch["t"], ch["et"])
        size *= 2
    for ch in chains:
        d, cols = ch["d"], ch["cols"]
        x = _bdot(ch["t"], jnp.concatenate([v_ref[:, cols] * ch["scale"], k_ref[:, cols] * ch["scale_k"]], axis=1))
        u_ref[d, :, cols] = x[:, :DN_DH]
        w_ref[d, :, cols] = x[:, DN_DH:].astype(BF16)


def deltanet_prep(q, k, v, gx, gxt):
    r = q.shape[0]
    tm = min(ROW_TILE, r)
    n_chunks = tm // CHUNK
    hw = DN_HEADS * DN_DH
    hps = DN_HEADS_PER_STEP
    head = pl.BlockSpec((tm, hps * DN_DH), lambda i, h: (i, h))
    dhead = pl.BlockSpec((2, tm, hps * DN_DH), lambda i, h: (0, i, h))
    return pl.pallas_call(
        functools.partial(_dnprep_kernel, n_chunks=n_chunks),
        grid=(r // tm, DN_HEADS // hps),
        in_specs=[head, head, head, pl.BlockSpec((tm, LANES), lambda i, h: (i, 0)),
                  pl.BlockSpec((LANES, tm), lambda i, h: (0, i))],
        out_specs=[dhead, dhead, dhead, dhead,
                   pl.BlockSpec((2, hps, tm, CHUNK), lambda i, h: (0, h, i, 0)),
                   pl.BlockSpec((2, hps, n_chunks, 8, LANES), lambda i, h: (0, h, i, 0, 0))],
        out_shape=[jax.ShapeDtypeStruct((2, r, hw), F32),
                   jax.ShapeDtypeStruct((2, r, hw), BF16),
                   jax.ShapeDtypeStruct((2, r, hw), BF16),
                   jax.ShapeDtypeStruct((2, r, hw), BF16),
                   jax.ShapeDtypeStruct((2, DN_HEADS, r, CHUNK), BF16),
                   jax.ShapeDtypeStruct((2, DN_HEADS, r // CHUNK, 8, LANES), F32)],
        compiler_params=_params("parallel", "parallel"), name="deltanet_prep",
    )(q, k, v, gx, gxt)


def _dnscan_kernel(u_ref, w_ref, qd_ref, kd_ref, qk_ref, gl_ref, s0_ref, o_ref, sf_ref, s_ref, *, rev, n_steps):
    step = pl.program_id(0)

    @pl.when(step == 0)
    def _():
        s_ref[...] = s0_ref[...]

    c = CHUNK
    n_sub = u_ref.shape[0] // c
    order = range(n_sub - 1, -1, -1) if rev else range(n_sub)
    heads = range(DN_HEADS)
    col = lambda hd: slice(hd * DN_DH, (hd + 1) * DN_DH)
    dot = lambda a, b: jnp.dot(a, b, preferred_element_type=F32)
    for n in order:
        rows = slice(n * c, (n + 1) * c)
        s = [s_ref[hd] for hd in heads]
        sb = [t.astype(BF16) for t in s]
        ws = [dot(w_ref[rows, col(hd)], sb[hd]) for hd in heads]
        qs = [dot(qd_ref[rows, col(hd)], sb[hd]) for hd in heads]
        vb = [(u_ref[rows, col(hd)] - ws[hd]).astype(BF16) for hd in heads]
        for hd in heads:
            o_ref[rows, col(hd)] = qs[hd] + dot(qk_ref[hd, rows, :], vb[hd])
        ds = [lax.dot_general(kd_ref[rows, col(hd)], vb[hd], TN_DIMS, preferred_element_type=F32) for hd in heads]
        for hd in heads:
            s_ref[hd] = s[hd] * gl_ref[hd, n, 0:1, :] + ds[hd]

    @pl.when(step == n_steps - 1)
    def _():
        sf_ref[...] = s_ref[...]


def deltanet_scan(u, w, qd, kd, qk, gl, s0, *, d):
    r = u.shape[1]
    hw = DN_HEADS * DN_DH
    tm = 2 * CHUNK
    n_steps = r // tm
    idx = (lambda i: n_steps - 1 - i) if d else (lambda i: i)
    big = pl.BlockSpec((None, tm, hw), lambda i: (d, idx(i), 0))
    state = pl.BlockSpec((DN_HEADS, DN_DH, DN_DH), lambda i: (0, 0, 0))
    return pl.pallas_call(
        functools.partial(_dnscan_kernel, rev=bool(d), n_steps=n_steps),
        grid=(n_steps,),
        in_specs=[big, big, big, big,
                  pl.BlockSpec((None, DN_HEADS, tm, CHUNK), lambda i: (d, 0, idx(i), 0)),
                  pl.BlockSpec((None, DN_HEADS, tm // CHUNK, 8, LANES), lambda i: (d, 0, idx(i), 0, 0)),
                  state],
        out_specs=[pl.BlockSpec((tm, hw), lambda i: (idx(i), 0)), state],
        out_shape=[jax.ShapeDtypeStruct((r, hw), F32),
                   jax.ShapeDtypeStruct((DN_HEADS, DN_DH, DN_DH), F32)],
        scratch_shapes=[pltpu.VMEM((DN_HEADS, DN_DH, DN_DH), F32)],
        compiler_params=_params("arbitrary"), name="deltanet_scan",
    )(u, w, qd, kd, qk, gl, s0)


def _lru_kernel(xc_ref, wa_ref, wx_ref, ba_ref, bx_ref, lam_ref, h0_ref, h_ref, hf_ref,
                a_s, b_s, carry_s, *, rev, n_steps):
    step = pl.program_id(0)

    @pl.when(step == 0)
    def _():
        carry_s[...] = h0_ref[...]

    tm = xc_ref.shape[0]
    sp = _softplus(-lam_ref[...])
    for n in range(LRU_BLOCKS):
        cols = slice(n * LRU_BW, (n + 1) * LRU_BW)
        xb = xc_ref[:, cols]
        xbb = xb.astype(BF16)
        r = jax.nn.sigmoid(jnp.dot(xbb, wa_ref[n], preferred_element_type=F32) + ba_ref[:, cols])
        gi = jax.nn.sigmoid(jnp.dot(xbb, wx_ref[n], preferred_element_type=F32) + bx_ref[:, cols])
        log_a = -LRU_C * r * sp[:, cols]
        a = jnp.exp(log_a)
        a_s[:, cols] = a
        b_s[:, cols] = jnp.sqrt(-jnp.tanh(log_a) * (a * a + 1.0)) * (gi * xb)

    rid = lax.broadcasted_iota(I32, (8, a_s.shape[1]), 0)
    n_groups = tm // 8

    def group(gidx, carry):
        g = (n_groups - 1 - gidx) if rev else gidx
        base = pl.multiple_of(g * 8, 8)
        a = a_s[pl.ds(base, 8), :]
        b = b_s[pl.ds(base, 8), :]
        for sh in (1, 2, 4):
            if rev:
                keep = rid < 8 - sh
                a_n = jnp.where(keep, pltpu.roll(a, 8 - sh, 0), 1.0)
                b_n = jnp.where(keep, pltpu.roll(b, 8 - sh, 0), 0.0)
            else:
                keep = rid >= sh
                a_n = jnp.where(keep, pltpu.roll(a, sh, 0), 1.0)
                b_n = jnp.where(keep, pltpu.roll(b, sh, 0), 0.0)
            b = a * b_n + b
            a = a * a_n
        h = a * carry + b
        h_ref[pl.ds(base, 8), :] = h
        edge = h[0:1, :] if rev else h[7:8, :]
        return jnp.broadcast_to(edge, h.shape)

    carry = lax.fori_loop(0, n_groups, group, carry_s[...])
    carry_s[...] = carry

    @pl.when(step == n_steps - 1)
    def _():
        hf_ref[...] = carry


def lru_scan(xc, wa, wx, ba, bx, lam, h0, *, d):
    r, width = xc.shape
    tm = min(ROW_TILE, r)
    n_steps = r // tm
    idx = (lambda i: n_steps - 1 - i) if d else (lambda i: i)
    vec = pl.BlockSpec((1, width), lambda i: (0, 0))
    wspec = pl.BlockSpec((LRU_BLOCKS, LRU_BW, LRU_BW), lambda i: (0, 0, 0))
    st = pl.BlockSpec((8, width), lambda i: (0, 0))
    return pl.pallas_call(
        functools.partial(_lru_kernel, rev=bool(d), n_steps=n_steps),
        grid=(n_steps,),
        in_specs=[pl.BlockSpec((tm, width), lambda i: (idx(i), 0)), wspec, wspec, vec, vec, vec, st],
        out_specs=[pl.BlockSpec((tm, width), lambda i: (idx(i), 0)), st],
        out_shape=[jax.ShapeDtypeStruct((r, width), F32), jax.ShapeDtypeStruct((8, width), F32)],
        scratch_shapes=[pltpu.VMEM((tm, width), F32), pltpu.VMEM((tm, width), F32), pltpu.VMEM((8, width), F32)],
        compiler_params=_params("arbitrary"), name="lru_scan",
    )(xc, wa, wx, ba, bx, lam, h0)


def _gelu_tanh(x):
    return 0.5 * x * (1.0 + jnp.tanh(math.sqrt(2.0 / math.pi) * (x + 0.044715 * (x * x * x))))


def _evfin_kernel(of_ref, ob_ref, hf_ref, hb_ref, zg_ref, ng_ref, wout_ref, x_ref, gt_ref, o_ref, mix_s):
    hw = DN_HEADS * DN_DH
    for hd in range(DN_HEADS):
        cols = slice(hd * DN_DH, (hd + 1) * DN_DH)
        o = of_ref[:, cols] + ob_ref[:, cols]
        y = o * lax.rsqrt(jnp.mean(o * o, axis=-1, keepdims=True) + EPS) * ng_ref[...]
        mix_s[:, cols] = (y * _silu(zg_ref[:, cols])).astype(BF16)
    mix_s[:, hw:] = ((hf_ref[...] + hb_ref[...]) * _gelu_tanh(zg_ref[:, hw:])).astype(BF16)
    y = jnp.dot(mix_s[...], wout_ref[...], preferred_element_type=F32)
    o_ref[...] = x_ref[...] + gt_ref[...] * y


def even_finish(o_f, o_b, h_f, h_b, p, norm_g, w_out, x, gate):
    r, d = x.shape
    tm = min(ROW_TILE, r)
    hw = DN_HEADS * DN_DH
    row = lambda w: pl.BlockSpec((tm, w), lambda i: (i, 0))
    return pl.pallas_call(
        _evfin_kernel, grid=(r // tm,),
        in_specs=[row(hw), row(hw), row(hw), row(hw),
                  pl.BlockSpec((tm, 2 * hw), lambda i: (i, 2)),
                  pl.BlockSpec((1, DN_DH), lambda i: (0, 0)),
                  pl.BlockSpec(w_out.shape, lambda i: (0, 0)),
                  row(d), pl.BlockSpec((1, d), lambda i: (0, 0))],
        out_specs=row(d),
        out_shape=jax.ShapeDtypeStruct((r, d), F32),
        scratch_shapes=[pltpu.VMEM((tm, 2 * hw), BF16)],
        compiler_params=_params("parallel"), name="even_finish",
    )(o_f, o_b, h_f, h_b, p, norm_g.reshape(1, DN_DH), w_out, x, gate.reshape(1, d))


def _odfin_kernel(of_ref, ob_ref, go_ref, ng_ref, wout_ref, x_ref, gt_ref, o_ref, mix_s, *, dv):
    for hd in range(GLA_HEADS):
        cols = slice(hd * dv, (hd + 1) * dv)
        o = of_ref[:, cols] + ob_ref[:, cols]
        y = o * lax.rsqrt(jnp.mean(o * o, axis=-1, keepdims=True) + EPS) * ng_ref[...]
        mix_s[:, cols] = (y * _silu(go_ref[:, cols])).astype(BF16)
    y = jnp.dot(mix_s[...], wout_ref[...], preferred_element_type=F32)
    o_ref[...] = x_ref[...] + gt_ref[...] * y


def odd_finish(o_f, o_b, p, norm_g, w_out, x, gate):
    r, d = x.shape
    rows = r // GRID_W
    vw = o_f.shape[1]
    dv = vw // GLA_HEADS
    row = lambda w: pl.BlockSpec((rows, w), lambda c: (c, 0))
    return pl.pallas_call(
        functools.partial(_odfin_kernel, dv=dv), grid=(GRID_W,),
        in_specs=[row(vw), row(vw),
                  pl.BlockSpec((rows, vw), lambda c: (c, 2)),
                  pl.BlockSpec((1, dv), lambda c: (0, 0)),
                  pl.BlockSpec(w_out.shape, lambda c: (0, 0)),
                  _raster_spec(rows, d), pl.BlockSpec((1, d), lambda c: (0, 0))],
        out_specs=_raster_spec(rows, d),
        out_shape=jax.ShapeDtypeStruct((rows, GRID_W * d), F32),
        scratch_shapes=[pltpu.VMEM((rows, vw), BF16)],
        compiler_params=_params("parallel"), name="odd_finish",
    )(o_f, o_b, p, norm_g.reshape(1, dv), w_out, x.reshape(rows, GRID_W * d), gate.reshape(1, d)).reshape(r, d)


def _gla_kernel(q_ref, k_ref, v_ref, gd_ref, wg_ref, bg_ref, s0_ref, o_ref, sf_ref, s_ref, gc_s,
                *, rev, n_steps, dk):
    step = pl.program_id(1)

    @pl.when(step == 0)
    def _():
        s_ref[...] = s0_ref[...]

    c = CHUNK
    tm = q_ref.shape[0]
    ri = lax.broadcasted_iota(I32, (tm, tm), 0)
    ci = lax.broadcasted_iota(I32, (tm, tm), 1)
    tri = (((ri // c) == (ci // c)) & ((ci >= ri) if rev else (ci <= ri))).astype(BF16)
    logit = jnp.dot(gd_ref[...], wg_ref[...], precision=HI, preferred_element_type=F32) + bg_ref[...]
    g = -_softplus(-logit) * (1.0 / GLA_TAU)
    gc_all = _dot01(tri, g)
    gc_s[...] = gc_all
    q_all = q_ref[...] * dk ** -0.5
    r64 = lax.broadcasted_iota(I32, (c, c), 0)
    c64 = lax.broadcasted_iota(I32, (c, c), 1)
    rr = lax.broadcasted_iota(I32, (c, 1), 0) % SUB
    n_sub = c // SUB
    chunks = range(tm // c)
    rows = [slice(n * c, (n + 1) * c) for n in chunks]
    q = [q_all[r] for r in rows]
    k = [k_ref[r, :] for r in rows]
    gc = [gc_all[r] for r in rows]
    att = [jnp.zeros((c, c), F32) for _ in chunks]
    size = c // 2
    while size >= SUB:
        r_hi, c_hi = (r64 & size) != 0, (c64 & size) != 0
        pair = ((r64 // (2 * size)) == (c64 // (2 * size))) & ((~r_hi & c_hi) if rev else (r_hi & ~c_hi))
        for n in chunks:
            pieces = []
            for b in range(c // (2 * size)):
                mid = b * 2 * size + size
                edge = gc[n][mid:mid + 1] if rev else gc[n][mid - 1:mid]
                pieces.append(jnp.broadcast_to(edge, (2 * size, dk)))
            edge = pieces[0] if len(pieces) == 1 else jnp.concatenate(pieces, axis=0)
            qs = (q[n] * jnp.exp(jnp.minimum(gc[n] - edge, 0.0))).astype(BF16)
            ks = (k[n] * jnp.exp(jnp.minimum(edge - gc[n], 0.0))).astype(BF16)
            att[n] = att[n] + jnp.where(pair, lax.dot_general(qs, ks, NT_DIMS, preferred_element_type=F32), 0.0)
        size //= 2
    for jj in range(SUB):
        seen = ((rr <= jj) if rev else (rr >= jj)) & (c64 == (r64 // SUB) * SUB + jj)
        for n in chunks:
            rep = lambda ref: jnp.concatenate(
                [jnp.broadcast_to(ref[pl.ds(n * c + b * SUB + jj, 1), :], (SUB, dk)) for b in range(n_sub)], axis=0)
            col = jnp.sum(q[n] * rep(k_ref) * jnp.exp(gc[n] - rep(gc_s)), axis=-1, keepdims=True)
            att[n] = jnp.where(seen, col, att[n])
    vb = [v_ref[r, :].astype(BF16) for r in rows]
    o_intra = [jnp.dot(att[n].astype(BF16), vb[n], preferred_element_type=F32) for n in chunks]
    tot = [gc[n][0:1] if rev else gc[n][c - 1:c] for n in chunks]
    q_dec = [(q[n] * jnp.exp(gc[n])).astype(BF16) for n in chunks]
    ds = [lax.dot_general(vb[n], (k[n] * jnp.exp(tot[n] - gc[n])).astype(BF16), TN_DIMS, preferred_element_type=F32)
          for n in chunks]
    for n in (reversed(chunks) if rev else chunks):
        s = s_ref[...]
        o_ref[rows[n], :] = o_intra[n] + lax.dot_general(q_dec[n], s.astype(BF16), NT_DIMS,
                                                         preferred_element_type=F32)
        s_ref[...] = s * jnp.exp(tot[n]) + ds[n]

    @pl.when(step == n_steps - 1)
    def _():
        sf_ref[...] = s_ref[...]


def gla_scan(p, pg, wg_pad, bg, s0, *, d, dk, dv):
    r = p.shape[0]
    tm = min(ROW_TILE, r)
    n_steps = r // tm
    idx = (lambda i: n_steps - 1 - i) if d else (lambda i: i)
    qk_blocks = GLA_HEADS
    v_block0 = 2 * GLA_HEADS * dk // dv
    state = pl.BlockSpec((None, dv, dk), lambda h, i: (h, 0, 0))
    return pl.pallas_call(
        functools.partial(_gla_kernel, rev=bool(d), n_steps=n_steps, dk=dk),
        grid=(GLA_HEADS, n_steps),
        in_specs=[pl.BlockSpec((tm, dk), lambda h, i: (idx(i), h)),
                  pl.BlockSpec((tm, dk), lambda h, i: (idx(i), qk_blocks + h)),
                  pl.BlockSpec((tm, dv), lambda h, i: (idx(i), v_block0 + h)),
                  pl.BlockSpec((tm, LANES), lambda h, i: (idx(i), 0)),
                  pl.BlockSpec((LANES, dk), lambda h, i: (0, h)),
                  pl.BlockSpec((1, dk), lambda h, i: (0, h)),
                  state],
        out_specs=[pl.BlockSpec((tm, dv), lambda h, i: (idx(i), h)), state],
        out_shape=[jax.ShapeDtypeStruct((r, GLA_HEADS * dv), F32),
                   jax.ShapeDtypeStruct((GLA_HEADS, dv, dk), F32)],
        scratch_shapes=[pltpu.VMEM((dv, dk), F32), pltpu.VMEM((tm, dk), F32)],
        compiler_params=_params("parallel", "arbitrary"), name="gla_scan",
    )(p, p, p, pg, wg_pad, bg, s0)


def _ffn1_kernel(be_ref, x_ref, wg_ref, wu_ref, o_ref):
    x = x_ref[...]
    a = jnp.dot(x, wg_ref[...], preferred_element_type=F32)
    b = jnp.dot(x, wu_ref[...], preferred_element_type=F32)
    o_ref[...] = (_silu(a) * b).astype(o_ref.dtype)


def ffn_up(x, w_gate, w_up, block_expert, *, tm, tn=1408):
    r, d = x.shape
    hidden = w_gate.shape[2]
    wspec = pl.BlockSpec((None, d, tn), lambda j, i, be: (be[i], 0, j))
    return pl.pallas_call(
        _ffn1_kernel,
        grid_spec=pltpu.PrefetchScalarGridSpec(
            num_scalar_prefetch=1, grid=(hidden // tn, r // tm),
            in_specs=[pl.BlockSpec((tm, d), lambda j, i, be: (i, 0)), wspec, wspec],
            out_specs=pl.BlockSpec((tm, tn), lambda j, i, be: (i, j))),
        out_shape=jax.ShapeDtypeStruct((r, hidden), BF16),
        compiler_params=_params("parallel", "arbitrary"), name="ffn_up",
    )(block_expert, x, w_gate, w_up)


def _ffn2_res_kernel(be_ref, h_ref, w_ref, x_ref, gt_ref, o_ref):
    y = jnp.dot(h_ref[...], w_ref[...], preferred_element_type=F32)
    o_ref[...] = x_ref[...] + gt_ref[...] * y


def _ffn2_scale_kernel(be_ref, h_ref, w_ref, sw_ref, o_ref):
    y = jnp.dot(h_ref[...], w_ref[...], preferred_element_type=F32)
    o_ref[...] = (y * sw_ref[...]).astype(o_ref.dtype)


def ffn_down_residual(h, w_down, block_expert, x, gate, *, tm, tn=512):
    r, hidden = h.shape
    d = w_down.shape[2]
    return pl.pallas_call(
        _ffn2_res_kernel,
        grid_spec=pltpu.PrefetchScalarGridSpec(
            num_scalar_prefetch=1, grid=(d // tn, r // tm),
            in_specs=[pl.BlockSpec((tm, hidden), lambda j, i, be: (i, 0)),
                      pl.BlockSpec((None, hidden, tn), lambda j, i, be: (be[i], 0, j)),
                      pl.BlockSpec((tm, tn), lambda j, i, be: (i, j)),
                      pl.BlockSpec((1, tn), lambda j, i, be: (0, j))],
            out_specs=pl.BlockSpec((tm, tn), lambda j, i, be: (i, j))),
        out_shape=jax.ShapeDtypeStruct((r, d), F32),
        compiler_params=_params("parallel", "arbitrary"), name="ffn_down_residual",
    )(block_expert, h, w_down, x, gate.reshape(1, d))


def ffn_down_scaled(h, w_down, block_expert, slot_w, *, tn=512):
    r, hidden = h.shape
    d = w_down.shape[2]
    tm = min(MOE_ROWS, r)
    return pl.pallas_call(
        _ffn2_scale_kernel,
        grid_spec=pltpu.PrefetchScalarGridSpec(
            num_scalar_prefetch=1, grid=(d // tn, r // tm),
            in_specs=[pl.BlockSpec((tm, hidden), lambda j, i, be: (i, 0)),
                      pl.BlockSpec((None, hidden, tn), lambda j, i, be: (be[i], 0, j)),
                      pl.BlockSpec((tm, 1), lambda j, i, be: (i, 0))],
            out_specs=pl.BlockSpec((tm, tn), lambda j, i, be: (i, j))),
        out_shape=jax.ShapeDtypeStruct((r, d), BF16),
        compiler_params=_params("parallel", "arbitrary"), name="ffn_down_scaled",
    )(block_expert, h, w_down, slot_w.reshape(r, 1))


def _router_kernel(x_ref, g_ref, sh_ref, sc_ref, rw_ref, rb_ref, h_ref, idx_ref, wt_ref, rank_ref, cnt_ref):
    x = x_ref[...]
    y = x * lax.rsqrt(jnp.mean(x * x, axis=-1, keepdims=True) + EPS)
    h = (y * g_ref[...]) * (1.0 + sc_ref[...]) + sh_ref[...]
    hb = h.astype(BF16)
    h_ref[...] = hb
    logits = jnp.dot(hb, rw_ref[...], preferred_element_type=F32) + rb_ref[...]
    lane = lax.broadcasted_iota(I32, logits.shape, 1)
    neg = jnp.float32(-jnp.inf)
    logits = jnp.where(lane < N_EXPERTS, logits, neg)
    m0 = jnp.max(logits, axis=-1, keepdims=True)
    i0 = jnp.min(jnp.where(logits == m0, lane, LANES), axis=-1, keepdims=True)
    rest = jnp.where(lane == i0, neg, logits)
    m1 = jnp.max(rest, axis=-1, keepdims=True)
    i1 = jnp.min(jnp.where(rest == m1, lane, LANES), axis=-1, keepdims=True)
    e1 = jnp.exp(m1 - m0)
    w0 = 1.0 / (1.0 + e1)
    idx_ref[...] = jnp.where(lane == 0, i0, jnp.where(lane == 1, i1, 0))
    wt_ref[...] = jnp.where(lane == 0, w0, jnp.where(lane == 1, e1 * w0, 0.0))
    tm = x.shape[0]
    hot = (lane == i0) | (lane == i1)
    earlier = (lax.broadcasted_iota(I32, (tm, tm), 1) < lax.broadcasted_iota(I32, (tm, tm), 0)).astype(BF16)
    before = jnp.dot(earlier, hot.astype(BF16), preferred_element_type=F32)
    r0 = jnp.sum(jnp.where(lane == i0, before, 0.0), axis=-1, keepdims=True)
    r1 = jnp.sum(jnp.where(lane == i1, before, 0.0), axis=-1, keepdims=True)
    rank_ref[...] = jnp.where(lane == 0, r0, jnp.where(lane == 1, r1, 0.0)).astype(I32)
    cnt_ref[...] = jnp.broadcast_to(jnp.sum(hot.astype(F32), axis=0, keepdims=True), cnt_ref.shape).astype(I32)


def route(x, g, sh, sc, router_w, router_b):
    t, d = x.shape
    tm = min(ROW_TILE, t)
    vec = pl.BlockSpec((1, d), lambda i: (0, 0))
    row = lambda w: pl.BlockSpec((tm, w), lambda i: (i, 0))
    rw = jnp.zeros((d, LANES), BF16).at[:, :N_EXPERTS].set(router_w.astype(BF16))
    rb = jnp.zeros((1, LANES), F32).at[0, :N_EXPERTS].set(router_b)
    h, idx, wt, rank, cnt = pl.pallas_call(
        _router_kernel, grid=(t // tm,),
        in_specs=[row(d), vec, vec, vec, pl.BlockSpec((d, LANES), lambda i: (0, 0)),
                  pl.BlockSpec((1, LANES), lambda i: (0, 0))],
        out_specs=[row(d), row(LANES), row(LANES), row(LANES), pl.BlockSpec((8, LANES), lambda i: (i, 0))],
        out_shape=[jax.ShapeDtypeStruct((t, d), BF16), jax.ShapeDtypeStruct((t, LANES), I32),
                   jax.ShapeDtypeStruct((t, LANES), F32), jax.ShapeDtypeStruct((t, LANES), I32),
                   jax.ShapeDtypeStruct((t // tm * 8, LANES), I32)],
        compiler_params=_params("parallel"), name="route",
    )(x, g.reshape(1, d), sh.reshape(1, d), sc.reshape(1, d), rw, rb)
    return h, idx[:, :2], wt[:, :2], rank[:, :2], cnt.reshape(t // tm, 8, LANES)[:, 0, :N_EXPERTS]


def _gather_kernel(pb_ref, pt_ref, pf_ref, h_ref, dest_ref, wt_ref, o_ref, sw_ref, acc_ref, swacc_ref):
    i = pl.program_id(0)
    flags = pf_ref[i]

    @pl.when((flags & 1) != 0)
    def _():
        acc_ref[...] = jnp.zeros_like(acc_ref)
        swacc_ref[...] = jnp.zeros_like(swacc_ref)

    @pl.when((flags & 4) != 0)
    def _():
        slot = pb_ref[i] * MOE_ROWS + lax.broadcasted_iota(I32, (MOE_ROWS, 1), 0)
        hit0 = dest_ref[0:1, :] == slot
        hit1 = dest_ref[1:2, :] == slot
        acc_ref[...] += jnp.dot((hit0 | hit1).astype(BF16), h_ref[...], preferred_element_type=F32)
        swacc_ref[...] += jnp.sum(jnp.where(hit0, wt_ref[0:1, :], 0.0) + jnp.where(hit1, wt_ref[1:2, :], 0.0),
                                  axis=1, keepdims=True)

    @pl.when((flags & 2) != 0)
    def _():
        o_ref[...] = acc_ref[...].astype(o_ref.dtype)
        sw_ref[...] = swacc_ref[...]


def moe_gather(h, dest_t, wt_t, n_slots, pair_block, pair_tile, pair_flags):
    t, d = h.shape
    tm = min(ROW_TILE, t)
    n_pairs = pair_block.shape[0]
    tok = pl.BlockSpec((2, tm), lambda i, pb, pt, pf: (0, pt[i]))
    return pl.pallas_call(
        _gather_kernel,
        grid_spec=pltpu.PrefetchScalarGridSpec(
            num_scalar_prefetch=3, grid=(n_pairs,),
            in_specs=[pl.BlockSpec((tm, d), lambda i, pb, pt, pf: (pt[i], 0)), tok, tok],
            out_specs=[pl.BlockSpec((MOE_ROWS, d), lambda i, pb, pt, pf: (pb[i], 0)),
                       pl.BlockSpec((MOE_ROWS, 1), lambda i, pb, pt, pf: (pb[i], 0))],
            scratch_shapes=[pltpu.VMEM((MOE_ROWS, d), F32), pltpu.VMEM((MOE_ROWS, 1), F32)]),
        out_shape=[jax.ShapeDtypeStruct((n_slots, d), BF16), jax.ShapeDtypeStruct((n_slots, 1), F32)],
        compiler_params=_params("arbitrary"), name="moe_gather",
    )(pair_block, pair_tile, pair_flags, h, dest_t, wt_t)


def _combine_kernel(pb_ref, pt_ref, pf_ref, yb_ref, dest_ref, x_ref, gt_ref, o_ref, acc_ref):
    i = pl.program_id(0)
    flags = pf_ref[i]

    @pl.when((flags & 1) != 0)
    def _():
        acc_ref[...] = jnp.zeros_like(acc_ref)

    @pl.when((flags & 4) != 0)
    def _():
        slot = pb_ref[i] * MOE_ROWS + lax.broadcasted_iota(I32, (dest_ref.shape[0], MOE_ROWS), 1)
        hit = (dest_ref[:, 0:1] == slot) | (dest_ref[:, 1:2] == slot)
        acc_ref[...] += jnp.dot(hit.astype(BF16), yb_ref[...], preferred_element_type=F32)

    @pl.when((flags & 2) != 0)
    def _():
        o_ref[...] = x_ref[...] + gt_ref[...] * acc_ref[...]


def moe_combine(yb, dest, pair_block, pair_tile, pair_flags, x, gate):
    t, d = x.shape
    tm = min(ROW_TILE, t)
    n_pairs = pair_block.shape[0]
    return pl.pallas_call(
        _combine_kernel,
        grid_spec=pltpu.PrefetchScalarGridSpec(
            num_scalar_prefetch=3, grid=(n_pairs,),
            in_specs=[pl.BlockSpec((MOE_ROWS, d), lambda i, pb, pt, pf: (pb[i], 0)),
                      pl.BlockSpec((tm, 2), lambda i, pb, pt, pf: (pt[i], 0)),
                      pl.BlockSpec((tm, d), lambda i, pb, pt, pf: (pt[i], 0)),
                      pl.BlockSpec((1, d), lambda i, pb, pt, pf: (0, 0))],
            out_specs=pl.BlockSpec((tm, d), lambda i, pb, pt, pf: (pt[i], 0)),
            scratch_shapes=[pltpu.VMEM((tm, d), F32)]),
        out_shape=jax.ShapeDtypeStruct((t, d), F32),
        compiler_params=_params("arbitrary"), name="moe_combine",
    )(pair_block, pair_tile, pair_flags, yb, dest, x, gate.reshape(1, d))


def _pair_lists(lo, hi, nonempty, n_pairs):
    cnt = jnp.where(nonempty, hi - lo + 1, 1)
    end = jnp.cumsum(cnt)
    start = end - cnt
    i = jnp.arange(n_pairs, dtype=I32)
    ic = jnp.minimum(i, end[-1] - 1)
    grp = jnp.sum((end[None, :] <= ic[:, None]).astype(I32), axis=1)
    off = ic - start[grp]
    member = jnp.where(nonempty[grp], lo[grp] + off, 0).astype(I32)
    return grp, member, off == 0, off == cnt[grp] - 1, nonempty[grp], i < end[-1]


def _pair_flags(first, last, data, valid):
    flags = jnp.where(first, 1, 0) | jnp.where(last, 2, 0) | jnp.where(data, 4, 0)
    return jnp.where(valid, flags, 0).astype(I32)


def moe_plan(top_idx, rank, tile_cnt, n_tok):
    tm = min(ROW_TILE, n_tok)
    n_tiles = n_tok // tm
    n_blocks = 2 * n_tok // MOE_ROWS + N_EXPERTS
    experts = jnp.arange(N_EXPERTS, dtype=I32)
    tile_off = jnp.cumsum(tile_cnt, axis=0) - tile_cnt
    counts = jnp.sum(tile_cnt, axis=0)
    padded = (counts + MOE_ROWS - 1) // MOE_ROWS * MOE_ROWS
    pad_end = jnp.cumsum(padded)
    base = (pad_end - padded)[None, :] + tile_off
    hot = top_idx[:, :, None] == experts[None, None, :]
    dest = (jnp.sum(jnp.where(hot, jnp.repeat(base, tm, axis=0)[:, None, :], 0), axis=-1) + rank).astype(I32)
    blk = jnp.arange(n_blocks, dtype=I32)
    block_expert = jnp.minimum(jnp.sum((pad_end[None, :] <= (blk * MOE_ROWS)[:, None]).astype(I32), axis=1),
                               N_EXPERTS - 1)
    n_pairs = N_EXPERTS * n_tiles + n_blocks
    c_grp, c_block, first, last, data, valid = _pair_lists(
        (base // MOE_ROWS).reshape(-1), ((base + tile_cnt - 1) // MOE_ROWS).reshape(-1), (tile_cnt > 0).reshape(-1),
        n_pairs)
    c_flags = _pair_flags(first & (c_grp % N_EXPERTS == 0), last & (c_grp % N_EXPERTS == N_EXPERTS - 1), data, valid)
    mine = block_expert[:, None, None] == experts[None, None, :]
    base_b = jnp.sum(jnp.where(mine, base[None], 0), axis=-1)
    cnt_b = jnp.sum(jnp.where(mine, tile_cnt[None], 0), axis=-1)
    sends = (cnt_b > 0) & (base_b + cnt_b > (blk * MOE_ROWS)[:, None]) & (base_b < ((blk + 1) * MOE_ROWS)[:, None])
    tiles = jnp.arange(n_tiles, dtype=I32)[None, :]
    g_block, g_tile, *g_bits = _pair_lists(jnp.min(jnp.where(sends, tiles, n_tiles), axis=1),
                                           jnp.max(jnp.where(sends, tiles, -1), axis=1), jnp.any(sends, axis=1), n_pairs)
    return dict(dest=dest, block_expert=block_expert.astype(I32), n_slots=n_blocks * MOE_ROWS,
                gather=(g_block, g_tile, _pair_flags(*g_bits)), combine=(c_block, c_grp // N_EXPERTS, c_flags))


def _mods(mod_all, layer, row, d):
    m = mod_all[layer, row]
    return tuple(m[j * d:(j + 1) * d] for j in range(N_MOD))


def _even_layer(x_lat, x_ctx, mods_lat, mods_ctx, norm1_g, norm2_g, w):
    d = x_lat.shape[1]
    hw = DN_HEADS * DN_DH
    states = dict(dn=[jnp.zeros((DN_HEADS, DN_DH, DN_DH), F32)] * 2, lru=[jnp.zeros((8, hw), F32)] * 2)
    outs = []
    for x, mods in ((x_ctx, mods_ctx), (x_lat, mods_lat)):
        sh1, sc1, gt1, sh2, sc2, gt2 = mods
        h = norm_mod(x, norm1_g, sh1, sc1)
        p = matmul(h, w["w_main"], tn=1024)
        pg = matmul(h, w["w_gate"], tn=LANES)
        q, k, v, xc, gx, gxt = even_prep(p, pg, w["conv_w"], w["conv_b"], w["a_log"], w["dt_bias"])
        u, wm, qd, kd, qk, gl = deltanet_prep(q, k, v, gx, gxt)
        o, hl = [], []
        for dirn in range(2):
            o_d, states["dn"][dirn] = deltanet_scan(u, wm, qd, kd, qk, gl, states["dn"][dirn], d=dirn)
            h_d, states["lru"][dirn] = lru_scan(xc, w["lru_wa"][dirn], w["lru_wx"][dirn], w["lru_ba"][dirn],
                                                w["lru_bx"][dirn], w["lru_lam"][dirn], states["lru"][dirn], d=dirn)
            o.append(o_d)
            hl.append(h_d)
        x = even_finish(o[0], o[1], hl[0], hl[1], p, w["dn_norm_g"], w["w_out"], x, gt1)
        h2 = norm_mod(x, norm2_g, sh2, sc2)
        tm = _dense_rows(h2.shape[0])
        be = jnp.zeros((h2.shape[0] // tm,), I32)
        hh = ffn_up(h2, w["ffn_gate"], w["ffn_up"], be, tm=tm)
        x = ffn_down_residual(hh, w["ffn_down"], be, x, gt2, tm=tm)
        outs.append(x)
    return outs[1], outs[0]


def _odd_layer_last(x_lat, x_ctx, mods_lat, mods_ctx, norm1_g, norm2_g, w):
    d = x_lat.shape[1]
    dk, dv = w["dk"], w["dv"]
    states = [jnp.zeros((GLA_HEADS, dv, dk), F32)] * 2
    sh1, sc1 = mods_ctx[0], mods_ctx[1]
    h = norm_mod(x_ctx, norm1_g, sh1, sc1)
    p = matmul(h, w["w_main"], tn=1024)
    pg = matmul(h, w["w_gate"], tn=LANES, out_dtype=F32)
    for dirn in range(2):
        _, states[dirn] = gla_scan(p, pg, w["wg_pad"][dirn], w["bg"][dirn], states[dirn], d=dirn, dk=dk, dv=dv)
    sh1, sc1, gt1, sh2, sc2, gt2 = mods_lat
    h = norm_mod(x_lat, norm1_g, sh1, sc1, raster=True)
    p = matmul(h, w["w_main"], tn=1024)
    pg = matmul(h, w["w_gate"], tn=LANES, out_dtype=F32)
    o = [gla_scan(p, pg, w["wg_pad"][dirn], w["bg"][dirn], states[dirn], d=dirn, dk=dk, dv=dv)[0]
         for dirn in range(2)]
    x = odd_finish(o[0], o[1], p, w["gla_norm_g"], w["w_out"], x_lat, gt1)
    n_tok = x.shape[0]
    h2, top_idx, top_w, rank, tile_cnt = route(x, norm2_g, sh2, sc2, w["router_w"], w["router_b"])
    plan = moe_plan(top_idx, rank, tile_cnt, n_tok)
    xb, slot_w = moe_gather(h2, plan["dest"].T, top_w.T, plan["n_slots"], *plan["gather"])
    hh = ffn_up(xb, w["exp_gate"], w["exp_up"], plan["block_expert"], tm=MOE_ROWS)
    yb = ffn_down_scaled(hh, w["exp_down"], plan["block_expert"], slot_w)
    return moe_combine(yb, plan["dest"], *plan["combine"], x, gt2)


def kernel(x, c, ctx, c_ctx, mod_w, mod_b, norm1_g, norm2_g, ev_w_in, ev_conv_qkv, ev_dn_a_log, ev_dn_dt_bias,
           ev_dn_norm_g, ev_lru_conv_w, ev_lru_conv_b, ev_lru_wa, ev_lru_ba, ev_lru_wx, ev_lru_bx, ev_lru_lambda,
           ev_w_out, ev_ffn_w_gate, ev_ffn_w_up, ev_ffn_w_down, od_w_in, od_gla_wg2, od_gla_bg, od_gla_norm_g,
           od_w_out, od_router_w, od_router_b, od_exp_w_gate, od_exp_w_up, od_exp_w_down, final_norm_g):
    b_, length, d = x.shape
    assert b_ == 1 and mod_w.shape[0] == 2, "this kernel implements the batch-1, depth-2 configuration"
    hw = DN_HEADS * DN_DH
    x_lat, x_ctx = x[0], ctx[0]

    cond8 = jnp.zeros((8, d), F32).at[0].set(c[0]).at[1].set(c_ctx)
    mod_all = adaln_all(cond8, mod_w, mod_b)

    w_in = ev_w_in[0]
    qkv_w, z0, ab0, xr0, gr0 = 3 * hw, 3 * hw, 4 * hw, 4 * hw + 4 * DN_HEADS, 5 * hw + 4 * DN_HEADS
    w_main = jnp.concatenate([w_in[:, :qkv_w], w_in[:, xr0:xr0 + hw], w_in[:, z0:z0 + hw], w_in[:, gr0:gr0 + hw]],
                             axis=1).astype(BF16)
    w_gate = jnp.zeros((d, LANES), BF16).at[:, :4 * DN_HEADS].set(w_in[:, ab0:ab0 + 4 * DN_HEADS].astype(BF16))
    pad16 = lambda t: jnp.zeros((1, LANES), F32).at[0, :2 * DN_HEADS].set(t.reshape(-1))
    ev = dict(
        w_main=w_main, w_gate=w_gate,
        conv_w=jnp.concatenate([ev_conv_qkv[0], ev_lru_conv_w[0]], axis=1),
        conv_b=jnp.concatenate([jnp.zeros((qkv_w,), F32), ev_lru_conv_b[0]]).reshape(1, -1),
        a_log=pad16(ev_dn_a_log[0]), dt_bias=pad16(ev_dn_dt_bias[0]),
        dn_norm_g=ev_dn_norm_g[0],
        lru_wa=ev_lru_wa[0].astype(BF16), lru_wx=ev_lru_wx[0].astype(BF16),
        lru_ba=ev_lru_ba[0].reshape(2, 1, hw), lru_bx=ev_lru_bx[0].reshape(2, 1, hw),
        lru_lam=ev_lru_lambda[0].reshape(2, 1, hw),
        w_out=ev_w_out[0].astype(BF16),
        ffn_gate=ev_ffn_w_gate.astype(BF16), ffn_up=ev_ffn_w_up.astype(BF16), ffn_down=ev_ffn_w_down.astype(BF16),
    )
    x_lat, x_ctx = _even_layer(x_lat, x_ctx, _mods(mod_all, 0, 0, d), _mods(mod_all, 0, 1, d),
                               norm1_g[0], norm2_g[0], ev)

    w_in = od_w_in[0]
    qk_w = od_gla_wg2.shape[-1]
    v_w = od_w_out.shape[1]
    main_w = 2 * qk_w + 2 * v_w
    wg_pad = jnp.zeros((2, LANES, qk_w), F32)
    for dirn in range(2):
        wg_pad = wg_pad.at[dirn, dirn * GLA_RANK:(dirn + 1) * GLA_RANK].set(od_gla_wg2[0, dirn])
    od = dict(
        w_main=w_in[:, :main_w].astype(BF16),
        w_gate=jnp.zeros((d, LANES), BF16).at[:, :2 * GLA_RANK].set(w_in[:, main_w:].astype(BF16)),
        wg_pad=wg_pad, bg=od_gla_bg[0].reshape(2, 1, qk_w), gla_norm_g=od_gla_norm_g[0],
        w_out=od_w_out[0].astype(BF16), router_w=od_router_w[0], router_b=od_router_b[0],
        exp_gate=od_exp_w_gate[0].astype(BF16), exp_up=od_exp_w_up[0].astype(BF16),
        exp_down=od_exp_w_down[0].astype(BF16),
        dk=qk_w // GLA_HEADS, dv=v_w // GLA_HEADS,
    )
    x_lat = _odd_layer_last(x_lat, x_ctx, _mods(mod_all, 1, 0, d), _mods(mod_all, 1, 1, d),
                            norm1_g[1], norm2_g[1], od)
    return final_norm(x_lat, final_norm_g)[None]
```

```python
import functools
import math

import jax
import jax.numpy as jnp
from jax import lax
from jax.experimental import pallas as pl
from jax.experimental.pallas import tpu as pltpu

F32 = jnp.float32
BF16 = jnp.bfloat16
I32 = jnp.int32
HI = lax.Precision.HIGHEST

EPS = 1e-6
N_MOD = 6
GRID_W = 64
CHUNK = 64
SUB = 8
LANES = 128
ROW_TILE = 256
DN_HEADS = 8
DN_DH = 128
LRU_BLOCKS = 8
LRU_BW = 128
LRU_C = 8.0
GLA_HEADS = 4
GLA_RANK = 16
GLA_TAU = 16.0
GLA_SAFE_STEP = 7.5
N_EXPERTS = 8
MOE_ROWS = 256
VMEM_LIMIT = 56 * 1024 * 1024

NT_DIMS = (((1,), (1,)), ((), ()))
TN_DIMS = (((0,), (0,)), ((), ()))


def _params(*sem):
    return pltpu.CompilerParams(dimension_semantics=sem, vmem_limit_bytes=VMEM_LIMIT)


def _softplus(x):
    return jnp.maximum(x, 0.0) + jnp.log1p(jnp.exp(-jnp.abs(x)))


def _silu(x):
    return x * jax.nn.sigmoid(x)


def _bdot(a, b):
    return jnp.dot(a.astype(BF16), b.astype(BF16), preferred_element_type=F32)


def _adaln_kernel(cond_ref, w_ref, b_ref, o_ref):
    s = _silu(cond_ref[...])
    o_ref[...] = jnp.dot(s, w_ref[...], precision=HI, preferred_element_type=F32) + b_ref[...]


def adaln_all(cond8, mod_w, mod_b):
    n_layers, d, n6 = mod_w.shape
    tn = 1024
    return pl.pallas_call(
        _adaln_kernel,
        grid=(n_layers, n6 // tn),
        in_specs=[pl.BlockSpec((8, d), lambda l, j: (0, 0)),
                  pl.BlockSpec((None, d, tn), lambda l, j: (l, 0, j)),
                  pl.BlockSpec((None, 1, tn), lambda l, j: (l, 0, j))],
        out_specs=pl.BlockSpec((None, 8, tn), lambda l, j: (l, 0, j)),
        out_shape=jax.ShapeDtypeStruct((n_layers, 8, n6), F32),
        compiler_params=_params("arbitrary", "arbitrary"),
        name="adaln",
    )(cond8, mod_w, mod_b.reshape(n_layers, 1, n6))


def _norm_mod_kernel(x_ref, g_ref, sh_ref, sc_ref, o_ref):
    x = x_ref[...]
    y = x * lax.rsqrt(jnp.mean(x * x, axis=-1, keepdims=True) + EPS)
    o_ref[...] = ((y * g_ref[...]) * (1.0 + sc_ref[...]) + sh_ref[...]).astype(o_ref.dtype)


def _raster_spec(rows, d):
    return pl.BlockSpec((rows, d), lambda c: (0, c))


def norm_mod(x, g, sh, sc):
    n, d = x.shape
    tm = min(ROW_TILE, n)
    vec = pl.BlockSpec((1, d), lambda i: (0, 0))
    row = pl.BlockSpec((tm, d), lambda i: (i, 0))
    return pl.pallas_call(
        _norm_mod_kernel, grid=(n // tm,), in_specs=[row, vec, vec, vec], out_specs=row,
        out_shape=jax.ShapeDtypeStruct((n, d), BF16),
        compiler_params=_params("parallel"), name="norm_mod",
    )(x, g.reshape(1, d), sh.reshape(1, d), sc.reshape(1, d))


def _dense_rows(r):
    return 2 * ROW_TILE if r % (2 * ROW_TILE) == 0 else min(ROW_TILE, r)


def _norm_proj_kernel(x_ref, g_ref, sh_ref, sc_ref, w_ref, wg_ref, p_ref, pg_ref, h_s, *, d):
    @pl.when(pl.program_id(1) == 0)
    def _():
        rows = x_ref.shape[0]
        for col in range(x_ref.shape[1] // d):
            x = x_ref[:, col * d:(col + 1) * d]
            y = x * lax.rsqrt(jnp.mean(x * x, axis=-1, keepdims=True) + EPS)
            h_s[col * rows:(col + 1) * rows, :] = ((y * g_ref[...]) * (1.0 + sc_ref[...]) + sh_ref[...]).astype(BF16)
        pg_ref[...] = jnp.dot(h_s[...], wg_ref[...], preferred_element_type=F32)

    p_ref[...] = jnp.dot(h_s[...], w_ref[...], preferred_element_type=F32)


def norm_proj(x, g, sh, sc, w_main, w_gate, *, raster=False, tn=1024):
    n, d = x.shape
    n_main = w_main.shape[1]
    tm = _dense_rows(n)
    if raster:
        rows = n // GRID_W
        n_col = max(tm // rows, 1)
        tm = n_col * rows
        x_in, x_spec = x.reshape(rows, GRID_W * d), pl.BlockSpec((rows, n_col * d), lambda i, j: (0, i))
    else:
        x_in, x_spec = x, pl.BlockSpec((tm, d), lambda i, j: (i, 0))
    vec = pl.BlockSpec((1, d), lambda i, j: (0, 0))
    return pl.pallas_call(
        functools.partial(_norm_proj_kernel, d=d), grid=(n // tm, n_main // tn),
        in_specs=[x_spec, vec, vec, vec, pl.BlockSpec((d, tn), lambda i, j: (0, j)),
                  pl.BlockSpec((d, LANES), lambda i, j: (0, 0))],
        out_specs=[pl.BlockSpec((tm, tn), lambda i, j: (i, j)), pl.BlockSpec((tm, LANES), lambda i, j: (i, 0))],
        out_shape=[jax.ShapeDtypeStruct((n, n_main), F32), jax.ShapeDtypeStruct((n, LANES), F32)],
        scratch_shapes=[pltpu.VMEM((tm, d), BF16)],
        compiler_params=_params("parallel", "arbitrary"), name="norm_proj",
    )(x_in, g.reshape(1, d), sh.reshape(1, d), sc.reshape(1, d), w_main, w_gate)


def _evprep_kernel(p_ref, prev_ref, next_ref, pg_ref, cw_ref, cb_ref, alog_ref, dtb_ref,
                   q_ref, k_ref, v_ref, xc_ref, gx_ref, gxt_ref, ext_ref, *, n_tiles):
    i = pl.program_id(0)
    tm, width = p_ref.shape
    ext_ref[8:8 + tm, :] = p_ref[...]
    ext_ref[0:8, :] = jnp.where(i > 0, prev_ref[...], 0.0)
    ext_ref[8 + tm:16 + tm, :] = jnp.where(i < n_tiles - 1, next_ref[...], 0.0)
    acc = ext_ref[pl.ds(6, tm), :] * cw_ref[0:1, :]
    for j in range(1, 4):
        acc = acc + ext_ref[pl.ds(6 + j, tm), :] * cw_ref[j:j + 1, :]
    acc = acc + cb_ref[...]
    qk_w = DN_HEADS * DN_DH
    for hd in range(DN_HEADS):
        for part, ref, scale in ((0, q_ref, DN_DH ** -0.5), (1, k_ref, 1.0)):
            lo = part * qk_w + hd * DN_DH
            t = _silu(acc[:, lo:lo + DN_DH])
            t = t * lax.rsqrt(jnp.sum(t * t, axis=-1, keepdims=True) + EPS)
            ref[:, hd * DN_DH:(hd + 1) * DN_DH] = t * scale
    v_ref[...] = _silu(acc[:, 2 * qk_w:3 * qk_w])
    xc_ref[...] = acc[:, 3 * qk_w:]
    pg = pg_ref[...]
    lane = lax.broadcasted_iota(I32, pg.shape, 1)
    g = -jnp.exp(alog_ref[...]) * _softplus(pg + dtb_ref[...])
    ri = lax.broadcasted_iota(I32, (tm, tm), 0)
    ci = lax.broadcasted_iota(I32, (tm, tm), 1)
    same = (ri // CHUNK) == (ci // CHUNK)
    cum_f = _dot01((same & (ci <= ri)).astype(BF16), g)
    cum_b = _dot01((same & (ci >= ri)).astype(BF16), g)
    tot = _dot01(same.astype(BF16), g)
    gc = jnp.where(lane < DN_HEADS, cum_f, cum_b)
    gx = jnp.where(lane < 2 * DN_HEADS, gc,
                   jnp.where(lane < 4 * DN_HEADS, jax.nn.sigmoid(pg), pltpu.roll(tot, 4 * DN_HEADS, 1)))
    gx_ref[...] = gx
    gxt_ref[...] = gx.T


def _dot01(m01, x):
    x1 = x.astype(BF16)
    r1 = x - x1.astype(F32)
    x2 = r1.astype(BF16)
    x3 = (r1 - x2.astype(F32)).astype(BF16)
    dot = lambda t: jnp.dot(m01, t, preferred_element_type=F32)
    return dot(x1) + dot(x2) + dot(x3)


def even_prep(p, pg, conv_w, conv_b, a_log, dt_bias):
    r = p.shape[0]
    width = conv_w.shape[1]
    tm = min(ROW_TILE, r)
    n_tiles = r // tm
    hb = tm // 8
    out_w = DN_HEADS * DN_DH
    row = lambda w: pl.BlockSpec((tm, w), lambda i: (i, 0))
    vec = lambda w: pl.BlockSpec((1, w), lambda i: (0, 0))
    return pl.pallas_call(
        functools.partial(_evprep_kernel, n_tiles=n_tiles),
        grid=(n_tiles,),
        in_specs=[row(width),
                  pl.BlockSpec((8, width), lambda i: (jnp.maximum(i * hb - 1, 0), 0)),
                  pl.BlockSpec((8, width), lambda i: (jnp.minimum((i + 1) * hb, r // 8 - 1), 0)),
                  row(LANES),
                  pl.BlockSpec((4, width), lambda i: (0, 0)), vec(width), vec(LANES), vec(LANES)],
        out_specs=[row(out_w), row(out_w), row(out_w), row(out_w), row(LANES),
                   pl.BlockSpec((LANES, tm), lambda i: (0, i))],
        out_shape=[jax.ShapeDtypeStruct((r, out_w), F32)] * 4
        + [jax.ShapeDtypeStruct((r, LANES), F32), jax.ShapeDtypeStruct((LANES, r), F32)],
        scratch_shapes=[pltpu.VMEM((tm + 16, width), F32)],
        compiler_params=_params("parallel"), name="even_prep",
    )(p, p, p, pg, conv_w, conv_b, a_log, dt_bias)


DN_BASE = 16
DN_HEADS_PER_STEP = 2


def _dnprep_kernel(q_ref, k_ref, v_ref, gx_ref, gxt_ref, u_ref, w_ref, qd_ref, kd_ref, qk_ref, gl_ref, *, n_chunks):
    c = CHUNK
    tm = q_ref.shape[0]
    ri = lax.broadcasted_iota(I32, (tm, tm), 0)
    ci = lax.broadcasted_iota(I32, (tm, tm), 1)
    same = lambda s: (ri // s) == (ci // s)
    eye = (ri == ci).astype(F32)
    lane = lax.broadcasted_iota(I32, (tm, LANES), 1)
    gx = gx_ref[...]
    pick = lambda idx: jnp.sum(jnp.where(lane == idx, gx, 0.0), axis=1, keepdims=True)
    chains = []
    for hh in range(DN_HEADS_PER_STEP):
        hd = pl.program_id(1) * DN_HEADS_PER_STEP + hh
        cols = slice(hh * DN_DH, (hh + 1) * DN_DH)
        q = q_ref[:, cols]
        k = k_ref[:, cols]
        kb16 = k.astype(BF16)
        gram_k = lax.dot_general(kb16, kb16, NT_DIMS, preferred_element_type=F32)
        gram_q = lax.dot_general(q.astype(BF16), kb16, NT_DIMS, preferred_element_type=F32)
        for d in range(2):
            incl = same(c) & ((ci >= ri) if d else (ci <= ri))
            gcol = pick(d * DN_HEADS + hd)
            bcol = pick((2 + d) * DN_HEADS + hd)
            tcol = pick((4 + d) * DN_HEADS + hd)
            grow = gxt_ref[pl.ds(d * DN_HEADS + hd, 1), :]
            decay = jnp.where(incl, jnp.exp(gcol - grow), 0.0)
            a = jnp.where(ri == ci, 0.0, gram_k * bcol * decay)
            eg = jnp.exp(gcol)
            qk = (gram_q * decay).astype(BF16)
            for n in range(n_chunks):
                qk_ref[d, hh, n * c:(n + 1) * c, :] = qk[n * c:(n + 1) * c, n * c:(n + 1) * c]
                gl_ref[d, hh, n] = jnp.broadcast_to(jnp.exp(tcol[n * c:n * c + 1, :]), (8, LANES))
            qd_ref[d, :, cols] = (q * eg).astype(BF16)
            kd_ref[d, :, cols] = (k * jnp.exp(tcol - gcol)).astype(BF16)
            diag = jnp.where(same(DN_BASE), a, 0.0)
            chains.append(dict(d=d, cols=cols, a=a, t=eye - diag, p=diag, scale=bcol, scale_k=bcol * eg))
    size = 2
    while size < DN_BASE:
        for ch in chains:
            ch["p"] = _bdot(ch["p"], ch["p"])
        for ch in chains:
            ch["t"] = ch["t"] + _bdot(ch["t"], ch["p"])
        size *= 2
    size = DN_BASE
    while size < c:
        couple = same(2 * size) & ~same(size)
        for ch in chains:
            ch["et"] = _bdot(jnp.where(couple, ch["a"], 0.0), ch["t"])
        for ch in chains:
            ch["t"] = ch["t"] - _bdot(ch["t"], ch["et"])
        size *= 2
    for ch in chains:
        d, cols = ch["d"], ch["cols"]
        x = _bdot(ch["t"], jnp.concatenate([v_ref[:, cols] * ch["scale"], k_ref[:, cols] * ch["scale_k"]], axis=1))
        u_ref[d, :, cols] = x[:, :DN_DH]
        w_ref[d, :, cols] = x[:, DN_DH:].astype(BF16)


def deltanet_prep(q, k, v, gx, gxt):
    r = q.shape[0]
    tm = min(ROW_TILE, r)
    n_chunks = tm // CHUNK
    hw = DN_HEADS * DN_DH
    hps = DN_HEADS_PER_STEP
    head = pl.BlockSpec((tm, hps * DN_DH), lambda i, h: (i, h))
    dhead = pl.BlockSpec((2, tm, hps * DN_DH), lambda i, h: (0, i, h))
    return pl.pallas_call(
        functools.partial(_dnprep_kernel, n_chunks=n_chunks),
        grid=(r // tm, DN_HEADS // hps),
        in_specs=[head, head, head, pl.BlockSpec((tm, LANES), lambda i, h: (i, 0)),
                  pl.BlockSpec((LANES, tm), lambda i, h: (0, i))],
        out_specs=[dhead, dhead, dhead, dhead,
                   pl.BlockSpec((2, hps, tm, CHUNK), lambda i, h: (0, h, i, 0)),
                   pl.BlockSpec((2, hps, n_chunks, 8, LANES), lambda i, h: (0, h, i, 0, 0))],
        out_shape=[jax.ShapeDtypeStruct((2, r, hw), F32),
                   jax.ShapeDtypeStruct((2, r, hw), BF16),
                   jax.ShapeDtypeStruct((2, r, hw), BF16),
                   jax.ShapeDtypeStruct((2, r, hw), BF16),
                   jax.ShapeDtypeStruct((2, DN_HEADS, r, CHUNK), BF16),
                   jax.ShapeDtypeStruct((2, DN_HEADS, r // CHUNK, 8, LANES), F32)],
        compiler_params=_params("parallel", "parallel"), name="deltanet_prep",
    )(q, k, v, gx, gxt)


def _dnscan_kernel(u_ref, w_ref, qd_ref, kd_ref, qk_ref, gl_ref, s0_ref, o_ref, sf_ref, s_ref, *, rev, n_steps):
    step = pl.program_id(0)

    @pl.when(step == 0)
    def _():
        s_ref[...] = s0_ref[...]

    c = CHUNK
    n_sub = u_ref.shape[0] // c
    order = range(n_sub - 1, -1, -1) if rev else range(n_sub)
    heads = range(DN_HEADS)
    col = lambda hd: slice(hd * DN_DH, (hd + 1) * DN_DH)
    dot = lambda a, b: jnp.dot(a, b, preferred_element_type=F32)
    for n in order:
        rows = slice(n * c, (n + 1) * c)
        s = [s_ref[hd] for hd in heads]
        sb = [t.astype(BF16) for t in s]
        ws = [dot(w_ref[rows, col(hd)], sb[hd]) for hd in heads]
        qs = [dot(qd_ref[rows, col(hd)], sb[hd]) for hd in heads]
        vb = [(u_ref[rows, col(hd)] - ws[hd]).astype(BF16) for hd in heads]
        for hd in heads:
            o_ref[rows, col(hd)] = qs[hd] + dot(qk_ref[hd, rows, :], vb[hd])
        ds = [lax.dot_general(kd_ref[rows, col(hd)], vb[hd], TN_DIMS, preferred_element_type=F32) for hd in heads]
        for hd in heads:
            s_ref[hd] = s[hd] * gl_ref[hd, n, 0:1, :] + ds[hd]

    @pl.when(step == n_steps - 1)
    def _():
        sf_ref[...] = s_ref[...]


def deltanet_scan(u, w, qd, kd, qk, gl, s0, *, d):
    r = u.shape[1]
    hw = DN_HEADS * DN_DH
    tm = 2 * CHUNK
    n_steps = r // tm
    idx = (lambda i: n_steps - 1 - i) if d else (lambda i: i)
    big = pl.BlockSpec((None, tm, hw), lambda i: (d, idx(i), 0))
    state = pl.BlockSpec((DN_HEADS, DN_DH, DN_DH), lambda i: (0, 0, 0))
    return pl.pallas_call(
        functools.partial(_dnscan_kernel, rev=bool(d), n_steps=n_steps),
        grid=(n_steps,),
        in_specs=[big, big, big, big,
                  pl.BlockSpec((None, DN_HEADS, tm, CHUNK), lambda i: (d, 0, idx(i), 0)),
                  pl.BlockSpec((None, DN_HEADS, tm // CHUNK, 8, LANES), lambda i: (d, 0, idx(i), 0, 0)),
                  state],
        out_specs=[pl.BlockSpec((tm, hw), lambda i: (idx(i), 0)), state],
        out_shape=[jax.ShapeDtypeStruct((r, hw), F32),
                   jax.ShapeDtypeStruct((DN_HEADS, DN_DH, DN_DH), F32)],
        scratch_shapes=[pltpu.VMEM((DN_HEADS, DN_DH, DN_DH), F32)],
        compiler_params=_params("arbitrary"), name="deltanet_scan",
    )(u, w, qd, kd, qk, gl, s0)


def _lru_kernel(xc_ref, wa_ref, wx_ref, ba_ref, bx_ref, lam_ref, h0_ref, h_ref, hf_ref,
                a_s, b_s, carry_s, *, rev, n_steps):
    step = pl.program_id(0)

    @pl.when(step == 0)
    def _():
        carry_s[...] = h0_ref[...]

    tm = xc_ref.shape[0]
    sp = _softplus(-lam_ref[...])
    for n in range(LRU_BLOCKS):
        cols = slice(n * LRU_BW, (n + 1) * LRU_BW)
        xb = xc_ref[:, cols]
        xbb = xb.astype(BF16)
        r = jax.nn.sigmoid(jnp.dot(xbb, wa_ref[n], preferred_element_type=F32) + ba_ref[:, cols])
        gi = jax.nn.sigmoid(jnp.dot(xbb, wx_ref[n], preferred_element_type=F32) + bx_ref[:, cols])
        log_a = -LRU_C * r * sp[:, cols]
        a = jnp.exp(log_a)
        a_s[:, cols] = a
        b_s[:, cols] = jnp.sqrt(-jnp.tanh(log_a) * (a * a + 1.0)) * (gi * xb)

    rid = lax.broadcasted_iota(I32, (8, a_s.shape[1]), 0)
    n_groups = tm // 8

    def group(gidx, carry):
        g = (n_groups - 1 - gidx) if rev else gidx
        base = pl.multiple_of(g * 8, 8)
        a = a_s[pl.ds(base, 8), :]
        b = b_s[pl.ds(base, 8), :]
        for sh in (1, 2, 4):
            if rev:
                keep = rid < 8 - sh
                a_n = jnp.where(keep, pltpu.roll(a, 8 - sh, 0), 1.0)
                b_n = jnp.where(keep, pltpu.roll(b, 8 - sh, 0), 0.0)
            else:
                keep = rid >= sh
                a_n = jnp.where(keep, pltpu.roll(a, sh, 0), 1.0)
                b_n = jnp.where(keep, pltpu.roll(b, sh, 0), 0.0)
            b = a * b_n + b
            a = a * a_n
        h = a * carry + b
        h_ref[pl.ds(base, 8), :] = h
        edge = h[0:1, :] if rev else h[7:8, :]
        return jnp.broadcast_to(edge, h.shape)

    carry = lax.fori_loop(0, n_groups, group, carry_s[...])
    carry_s[...] = carry

    @pl.when(step == n_steps - 1)
    def _():
        hf_ref[...] = carry


def lru_scan(xc, wa, wx, ba, bx, lam, h0, *, d):
    r, width = xc.shape
    tm = min(ROW_TILE, r)
    n_steps = r // tm
    idx = (lambda i: n_steps - 1 - i) if d else (lambda i: i)
    vec = pl.BlockSpec((1, width), lambda i: (0, 0))
    wspec = pl.BlockSpec((LRU_BLOCKS, LRU_BW, LRU_BW), lambda i: (0, 0, 0))
    st = pl.BlockSpec((8, width), lambda i: (0, 0))
    return pl.pallas_call(
        functools.partial(_lru_kernel, rev=bool(d), n_steps=n_steps),
        grid=(n_steps,),
        in_specs=[pl.BlockSpec((tm, width), lambda i: (idx(i), 0)), wspec, wspec, vec, vec, vec, st],
        out_specs=[pl.BlockSpec((tm, width), lambda i: (idx(i), 0)), st],
        out_shape=[jax.ShapeDtypeStruct((r, width), F32), jax.ShapeDtypeStruct((8, width), F32)],
        scratch_shapes=[pltpu.VMEM((tm, width), F32), pltpu.VMEM((tm, width), F32), pltpu.VMEM((8, width), F32)],
        compiler_params=_params("arbitrary"), name="lru_scan",
    )(xc, wa, wx, ba, bx, lam, h0)


def _gelu_tanh(x):
    return 0.5 * x * (1.0 + jnp.tanh(math.sqrt(2.0 / math.pi) * (x + 0.044715 * (x * x * x))))


def _evfin_kernel(of_ref, ob_ref, hf_ref, hb_ref, zg_ref, ng_ref, wout_ref, x_ref, gt_ref, o_ref, mix_s):
    hw = DN_HEADS * DN_DH
    for hd in range(DN_HEADS):
        cols = slice(hd * DN_DH, (hd + 1) * DN_DH)
        o = of_ref[:, cols] + ob_ref[:, cols]
        y = o * lax.rsqrt(jnp.mean(o * o, axis=-1, keepdims=True) + EPS) * ng_ref[...]
        mix_s[:, cols] = (y * _silu(zg_ref[:, cols])).astype(BF16)
    mix_s[:, hw:] = ((hf_ref[...] + hb_ref[...]) * _gelu_tanh(zg_ref[:, hw:])).astype(BF16)
    y = jnp.dot(mix_s[...], wout_ref[...], preferred_element_type=F32)
    o_ref[...] = x_ref[...] + gt_ref[...] * y


def even_finish(o_f, o_b, h_f, h_b, p, norm_g, w_out, x, gate):
    r, d = x.shape
    tm = min(ROW_TILE, r)
    hw = DN_HEADS * DN_DH
    row = lambda w: pl.BlockSpec((tm, w), lambda i: (i, 0))
    return pl.pallas_call(
        _evfin_kernel, grid=(r // tm,),
        in_specs=[row(hw), row(hw), row(hw), row(hw),
                  pl.BlockSpec((tm, 2 * hw), lambda i: (i, 2)),
                  pl.BlockSpec((1, DN_DH), lambda i: (0, 0)),
                  pl.BlockSpec(w_out.shape, lambda i: (0, 0)),
                  row(d), pl.BlockSpec((1, d), lambda i: (0, 0))],
        out_specs=row(d),
        out_shape=jax.ShapeDtypeStruct((r, d), F32),
        scratch_shapes=[pltpu.VMEM((tm, 2 * hw), BF16)],
        compiler_params=_params("parallel"), name="even_finish",
    )(o_f, o_b, h_f, h_b, p, norm_g.reshape(1, DN_DH), w_out, x, gate.reshape(1, d))


def _odfin_kernel(of_ref, ob_ref, go_ref, ng_ref, wout_ref, x_ref, gt_ref, o_ref, mix_s, *, dv):
    for hd in range(GLA_HEADS):
        cols = slice(hd * dv, (hd + 1) * dv)
        o = of_ref[:, cols] + ob_ref[:, cols]
        y = o * lax.rsqrt(jnp.mean(o * o, axis=-1, keepdims=True) + EPS) * ng_ref[...]
        mix_s[:, cols] = (y * _silu(go_ref[:, cols])).astype(BF16)
    y = jnp.dot(mix_s[...], wout_ref[...], preferred_element_type=F32)
    o_ref[...] = x_ref[...] + gt_ref[...] * y


def odd_finish(o_f, o_b, p, norm_g, w_out, x, gate):
    r, d = x.shape
    rows = r // GRID_W
    vw = o_f.shape[1]
    dv = vw // GLA_HEADS
    row = lambda w: pl.BlockSpec((rows, w), lambda c: (c, 0))
    return pl.pallas_call(
        functools.partial(_odfin_kernel, dv=dv), grid=(GRID_W,),
        in_specs=[row(vw), row(vw),
                  pl.BlockSpec((rows, vw), lambda c: (c, 2)),
                  pl.BlockSpec((1, dv), lambda c: (0, 0)),
                  pl.BlockSpec(w_out.shape, lambda c: (0, 0)),
                  _raster_spec(rows, d), pl.BlockSpec((1, d), lambda c: (0, 0))],
        out_specs=_raster_spec(rows, d),
        out_shape=jax.ShapeDtypeStruct((rows, GRID_W * d), F32),
        scratch_shapes=[pltpu.VMEM((rows, vw), BF16)],
        compiler_params=_params("parallel"), name="odd_finish",
    )(o_f, o_b, p, norm_g.reshape(1, dv), w_out, x.reshape(rows, GRID_W * d), gate.reshape(1, d)).reshape(r, d)


def _gla_kernel(q_ref, k_ref, v_ref, gd_ref, wg_ref, bg_ref, s0_ref, o_ref, sf_ref, s_ref, gc_s,
                *, rev, n_steps, dk):
    step = pl.program_id(1)

    @pl.when(step == 0)
    def _():
        s_ref[...] = s0_ref[...]

    c = CHUNK
    tm = q_ref.shape[0]
    ri = lax.broadcasted_iota(I32, (tm, tm), 0)
    ci = lax.broadcasted_iota(I32, (tm, tm), 1)
    tri = (((ri // c) == (ci // c)) & ((ci >= ri) if rev else (ci <= ri))).astype(BF16)
    logit = jnp.dot(gd_ref[...], wg_ref[...], precision=HI, preferred_element_type=F32) + bg_ref[...]
    g = -_softplus(-logit) * (1.0 / GLA_TAU)
    gc_all = _dot01(tri, g)
    gc_s[...] = gc_all
    q_all = q_ref[...] * dk ** -0.5
    r64 = lax.broadcasted_iota(I32, (c, c), 0)
    c64 = lax.broadcasted_iota(I32, (c, c), 1)
    rr = lax.broadcasted_iota(I32, (c, 1), 0) % SUB
    n_sub = c // SUB
    chunks = range(tm // c)
    rows = [slice(n * c, (n + 1) * c) for n in chunks]
    q = [q_all[r] for r in rows]
    k = [k_ref[r, :] for r in rows]
    gc = [gc_all[r] for r in rows]
    att = [jnp.zeros((c, c), F32) for _ in chunks]
    size = c // 2
    while size >= SUB:
        r_hi, c_hi = (r64 & size) != 0, (c64 & size) != 0
        pair = ((r64 // (2 * size)) == (c64 // (2 * size))) & ((~r_hi & c_hi) if rev else (r_hi & ~c_hi))
        for n in chunks:
            pieces = []
            for b in range(c // (2 * size)):
                mid = b * 2 * size + size
                edge = gc[n][mid:mid + 1] if rev else gc[n][mid - 1:mid]
                pieces.append(jnp.broadcast_to(edge, (2 * size, dk)))
            edge = pieces[0] if len(pieces) == 1 else jnp.concatenate(pieces, axis=0)
            qs = (q[n] * jnp.exp(jnp.minimum(gc[n] - edge, 0.0))).astype(BF16)
            ks = (k[n] * jnp.exp(jnp.minimum(edge - gc[n], 0.0))).astype(BF16)
            att[n] = att[n] + jnp.where(pair, lax.dot_general(qs, ks, NT_DIMS, preferred_element_type=F32), 0.0)
        size //= 2
    def diag_direct():
        blocks = [jnp.zeros((c, c), F32) for _ in chunks]
        for jj in range(SUB):
            seen = ((rr <= jj) if rev else (rr >= jj)) & (c64 == (r64 // SUB) * SUB + jj)
            for n in chunks:
                rep = lambda ref: jnp.concatenate(
                    [jnp.broadcast_to(ref[pl.ds(n * c + b * SUB + jj, 1), :], (SUB, dk)) for b in range(n_sub)],
                    axis=0)
                col = jnp.sum(q[n] * rep(k_ref) * jnp.exp(gc[n] - rep(gc_s)), axis=-1, keepdims=True)
                blocks[n] = jnp.where(seen, col, blocks[n])
        return tuple(blocks)

    def diag_factored():
        blocks = []
        inside = ((r64 // SUB) == (c64 // SUB)) & ((c64 >= r64) if rev else (c64 <= r64))
        for n in chunks:
            before = (gc[n] - g[rows[n]]).reshape(n_sub, SUB, dk)
            edge = before[:, SUB - 1:SUB, :] if rev else before[:, 0:1, :]
            edge = jnp.broadcast_to(edge, before.shape).reshape(c, dk)
            qs = (q[n] * jnp.exp(gc[n] - edge)).astype(BF16)
            ks = (k[n] * jnp.exp(edge - gc[n])).astype(BF16)
            blocks.append(jnp.where(inside, lax.dot_general(qs, ks, NT_DIMS, preferred_element_type=F32), 0.0))
        return tuple(blocks)

    diag = lax.cond(jnp.min(g) >= -GLA_SAFE_STEP, diag_factored, diag_direct)
    vb = [v_ref[r, :].astype(BF16) for r in rows]
    o_intra = [jnp.dot((att[n] + diag[n]).astype(BF16), vb[n], preferred_element_type=F32) for n in chunks]
    tot = [gc[n][0:1] if rev else gc[n][c - 1:c] for n in chunks]
    q_dec = [(q[n] * jnp.exp(gc[n])).astype(BF16) for n in chunks]
    ds = [lax.dot_general(vb[n], (k[n] * jnp.exp(tot[n] - gc[n])).astype(BF16), TN_DIMS, preferred_element_type=F32)
          for n in chunks]
    for n in (reversed(chunks) if rev else chunks):
        s = s_ref[...]
        o_ref[rows[n], :] = o_intra[n] + lax.dot_general(q_dec[n], s.astype(BF16), NT_DIMS,
                                                         preferred_element_type=F32)
        s_ref[...] = s * jnp.exp(tot[n]) + ds[n]

    @pl.when(step == n_steps - 1)
    def _():
        sf_ref[...] = s_ref[...]


def gla_scan(p, pg, wg_pad, bg, s0, *, d, dk, dv):
    r = p.shape[0]
    tm = min(ROW_TILE, r)
    n_steps = r // tm
    idx = (lambda i: n_steps - 1 - i) if d else (lambda i: i)
    qk_blocks = GLA_HEADS
    v_block0 = 2 * GLA_HEADS * dk // dv
    state = pl.BlockSpec((None, dv, dk), lambda h, i: (h, 0, 0))
    return pl.pallas_call(
        functools.partial(_gla_kernel, rev=bool(d), n_steps=n_steps, dk=dk),
        grid=(GLA_HEADS, n_steps),
        in_specs=[pl.BlockSpec((tm, dk), lambda h, i: (idx(i), h)),
                  pl.BlockSpec((tm, dk), lambda h, i: (idx(i), qk_blocks + h)),
                  pl.BlockSpec((tm, dv), lambda h, i: (idx(i), v_block0 + h)),
                  pl.BlockSpec((tm, LANES), lambda h, i: (idx(i), 0)),
                  pl.BlockSpec((LANES, dk), lambda h, i: (0, h)),
                  pl.BlockSpec((1, dk), lambda h, i: (0, h)),
                  state],
        out_specs=[pl.BlockSpec((tm, dv), lambda h, i: (idx(i), h)), state],
        out_shape=[jax.ShapeDtypeStruct((r, GLA_HEADS * dv), F32),
                   jax.ShapeDtypeStruct((GLA_HEADS, dv, dk), F32)],
        scratch_shapes=[pltpu.VMEM((dv, dk), F32), pltpu.VMEM((tm, dk), F32)],
        compiler_params=_params("parallel", "arbitrary"), name="gla_scan",
    )(p, p, p, pg, wg_pad, bg, s0)


def _when_block_used(nu_ref, o_ref, body):
    used = pl.program_id(1) < nu_ref[0]

    @pl.when(used)
    def _():
        body()

    @pl.when(jnp.logical_not(used))
    def _():
        o_ref[...] = jnp.zeros_like(o_ref)


def _ffn1_kernel(be_ref, nu_ref, x_ref, wg_ref, wu_ref, o_ref):
    def body():
        x = x_ref[...]
        a = jnp.dot(x, wg_ref[...], preferred_element_type=F32)
        b = jnp.dot(x, wu_ref[...], preferred_element_type=F32)
        o_ref[...] = (_silu(a) * b).astype(o_ref.dtype)

    _when_block_used(nu_ref, o_ref, body)


def ffn_up(x, w_gate, w_up, block_expert, n_used, *, tm, tn=1408):
    r, d = x.shape
    hidden = w_gate.shape[2]
    wspec = pl.BlockSpec((None, d, tn), lambda j, i, be, nu: (be[i], 0, j))
    return pl.pallas_call(
        _ffn1_kernel,
        grid_spec=pltpu.PrefetchScalarGridSpec(
            num_scalar_prefetch=2, grid=(hidden // tn, r // tm),
            in_specs=[pl.BlockSpec((tm, d), lambda j, i, be, nu: (i, 0)), wspec, wspec],
            out_specs=pl.BlockSpec((tm, tn), lambda j, i, be, nu: (i, j))),
        out_shape=jax.ShapeDtypeStruct((r, hidden), BF16),
        compiler_params=_params("parallel", "arbitrary"), name="ffn_up",
    )(block_expert, n_used, x, w_gate, w_up)


def _ffn2_res_kernel(be_ref, nu_ref, h_ref, w_ref, x_ref, gt_ref, o_ref):
    def body():
        y = jnp.dot(h_ref[...], w_ref[...], preferred_element_type=F32)
        o_ref[...] = x_ref[...] + gt_ref[...] * y

    _when_block_used(nu_ref, o_ref, body)


def _ffn2_scale_kernel(be_ref, nu_ref, h_ref, w_ref, sw_ref, o_ref):
    def body():
        y = jnp.dot(h_ref[...], w_ref[...], preferred_element_type=F32)
        o_ref[...] = (y * sw_ref[...]).astype(o_ref.dtype)

    _when_block_used(nu_ref, o_ref, body)


def ffn_down_residual(h, w_down, block_expert, n_used, x, gate, *, tm, tn=1024):
    r, hidden = h.shape
    d = w_down.shape[2]
    return pl.pallas_call(
        _ffn2_res_kernel,
        grid_spec=pltpu.PrefetchScalarGridSpec(
            num_scalar_prefetch=2, grid=(d // tn, r // tm),
            in_specs=[pl.BlockSpec((tm, hidden), lambda j, i, be, nu: (i, 0)),
                      pl.BlockSpec((None, hidden, tn), lambda j, i, be, nu: (be[i], 0, j)),
                      pl.BlockSpec((tm, tn), lambda j, i, be, nu: (i, j)),
                      pl.BlockSpec((1, tn), lambda j, i, be, nu: (0, j))],
            out_specs=pl.BlockSpec((tm, tn), lambda j, i, be, nu: (i, j))),
        out_shape=jax.ShapeDtypeStruct((r, d), F32),
        compiler_params=_params("parallel", "arbitrary"), name="ffn_down_residual",
    )(block_expert, n_used, h, w_down, x, gate.reshape(1, d))


def ffn_down_scaled(h, w_down, block_expert, n_used, slot_w, *, tn=1024):
    r, hidden = h.shape
    d = w_down.shape[2]
    tm = min(MOE_ROWS, r)
    return pl.pallas_call(
        _ffn2_scale_kernel,
        grid_spec=pltpu.PrefetchScalarGridSpec(
            num_scalar_prefetch=2, grid=(d // tn, r // tm),
            in_specs=[pl.BlockSpec((tm, hidden), lambda j, i, be, nu: (i, 0)),
                      pl.BlockSpec((None, hidden, tn), lambda j, i, be, nu: (be[i], 0, j)),
                      pl.BlockSpec((tm, 1), lambda j, i, be, nu: (i, 0))],
            out_specs=pl.BlockSpec((tm, tn), lambda j, i, be, nu: (i, j))),
        out_shape=jax.ShapeDtypeStruct((r, d), BF16),
        compiler_params=_params("parallel", "arbitrary"), name="ffn_down_scaled",
    )(block_expert, n_used, h, w_down, slot_w.reshape(r, 1))


def _router_kernel(x_ref, g_ref, sh_ref, sc_ref, rw_ref, rb_ref, h_ref, idx_ref, wt_ref, rank_ref, cnt_ref):
    x = x_ref[...]
    y = x * lax.rsqrt(jnp.mean(x * x, axis=-1, keepdims=True) + EPS)
    h = (y * g_ref[...]) * (1.0 + sc_ref[...]) + sh_ref[...]
    hb = h.astype(BF16)
    h_ref[...] = hb
    logits = jnp.dot(hb, rw_ref[...], preferred_element_type=F32) + rb_ref[...]
    lane = lax.broadcasted_iota(I32, logits.shape, 1)
    neg = jnp.float32(-jnp.inf)
    logits = jnp.where(lane < N_EXPERTS, logits, neg)
    m0 = jnp.max(logits, axis=-1, keepdims=True)
    i0 = jnp.min(jnp.where(logits == m0, lane, LANES), axis=-1, keepdims=True)
    rest = jnp.where(lane == i0, neg, logits)
    m1 = jnp.max(rest, axis=-1, keepdims=True)
    i1 = jnp.min(jnp.where(rest == m1, lane, LANES), axis=-1, keepdims=True)
    e1 = jnp.exp(m1 - m0)
    w0 = 1.0 / (1.0 + e1)
    idx_ref[...] = jnp.where(lane == 0, i0, jnp.where(lane == 1, i1, 0))
    wt_ref[...] = jnp.where(lane == 0, w0, jnp.where(lane == 1, e1 * w0, 0.0))
    tm = x.shape[0]
    hot = (lane == i0) | (lane == i1)
    earlier = (lax.broadcasted_iota(I32, (tm, tm), 1) < lax.broadcasted_iota(I32, (tm, tm), 0)).astype(BF16)
    before = jnp.dot(earlier, hot.astype(BF16), preferred_element_type=F32)
    r0 = jnp.sum(jnp.where(lane == i0, before, 0.0), axis=-1, keepdims=True)
    r1 = jnp.sum(jnp.where(lane == i1, before, 0.0), axis=-1, keepdims=True)
    rank_ref[...] = jnp.where(lane == 0, r0, jnp.where(lane == 1, r1, 0.0)).astype(I32)
    cnt_ref[...] = jnp.broadcast_to(jnp.sum(hot.astype(F32), axis=0, keepdims=True), cnt_ref.shape).astype(I32)


def route(x, g, sh, sc, router_w, router_b):
    t, d = x.shape
    tm = min(ROW_TILE, t)
    vec = pl.BlockSpec((1, d), lambda i: (0, 0))
    row = lambda w: pl.BlockSpec((tm, w), lambda i: (i, 0))
    rw = jnp.zeros((d, LANES), BF16).at[:, :N_EXPERTS].set(router_w.astype(BF16))
    rb = jnp.zeros((1, LANES), F32).at[0, :N_EXPERTS].set(router_b)
    h, idx, wt, rank, cnt = pl.pallas_call(
        _router_kernel, grid=(t // tm,),
        in_specs=[row(d), vec, vec, vec, pl.BlockSpec((d, LANES), lambda i: (0, 0)),
                  pl.BlockSpec((1, LANES), lambda i: (0, 0))],
        out_specs=[row(d), row(LANES), row(LANES), row(LANES), pl.BlockSpec((8, LANES), lambda i: (i, 0))],
        out_shape=[jax.ShapeDtypeStruct((t, d), BF16), jax.ShapeDtypeStruct((t, LANES), I32),
                   jax.ShapeDtypeStruct((t, LANES), F32), jax.ShapeDtypeStruct((t, LANES), I32),
                   jax.ShapeDtypeStruct((t // tm * 8, LANES), I32)],
        compiler_params=_params("parallel"), name="route",
    )(x, g.reshape(1, d), sh.reshape(1, d), sc.reshape(1, d), rw, rb)
    return h, idx[:, :2], wt[:, :2], rank[:, :2], cnt.reshape(t // tm, 8, LANES)[:, 0, :N_EXPERTS]


def _gather_kernel(pb_ref, pt_ref, pf_ref, h_ref, dest_ref, wt_ref, o_ref, sw_ref, acc_ref, swacc_ref):
    i = pl.program_id(0)
    flags = pf_ref[i]

    @pl.when((flags & 1) != 0)
    def _():
        acc_ref[...] = jnp.zeros_like(acc_ref)
        swacc_ref[...] = jnp.zeros_like(swacc_ref)

    @pl.when((flags & 4) != 0)
    def _():
        slot = pb_ref[i] * MOE_ROWS + lax.broadcasted_iota(I32, (MOE_ROWS, 1), 0)
        hit0 = dest_ref[0:1, :] == slot
        hit1 = dest_ref[1:2, :] == slot
        acc_ref[...] += jnp.dot((hit0 | hit1).astype(BF16), h_ref[...], preferred_element_type=F32)
        swacc_ref[...] += jnp.sum(jnp.where(hit0, wt_ref[0:1, :], 0.0) + jnp.where(hit1, wt_ref[1:2, :], 0.0),
                                  axis=1, keepdims=True)

    @pl.when((flags & 2) != 0)
    def _():
        o_ref[...] = acc_ref[...].astype(o_ref.dtype)
        sw_ref[...] = swacc_ref[...]


def moe_gather(h, dest_t, wt_t, n_slots, pair_block, pair_tile, pair_flags):
    t, d = h.shape
    tm = min(ROW_TILE, t)
    n_pairs = pair_block.shape[0]
    tok = pl.BlockSpec((2, tm), lambda i, pb, pt, pf: (0, pt[i]))
    return pl.pallas_call(
        _gather_kernel,
        grid_spec=pltpu.PrefetchScalarGridSpec(
            num_scalar_prefetch=3, grid=(n_pairs,),
            in_specs=[pl.BlockSpec((tm, d), lambda i, pb, pt, pf: (pt[i], 0)), tok, tok],
            out_specs=[pl.BlockSpec((MOE_ROWS, d), lambda i, pb, pt, pf: (pb[i], 0)),
                       pl.BlockSpec((MOE_ROWS, 1), lambda i, pb, pt, pf: (pb[i], 0))],
            scratch_shapes=[pltpu.VMEM((MOE_ROWS, d), F32), pltpu.VMEM((MOE_ROWS, 1), F32)]),
        out_shape=[jax.ShapeDtypeStruct((n_slots, d), BF16), jax.ShapeDtypeStruct((n_slots, 1), F32)],
        compiler_params=_params("arbitrary"), name="moe_gather",
    )(pair_block, pair_tile, pair_flags, h, dest_t, wt_t)


def _combine_kernel(pb_ref, pt_ref, pf_ref, yb_ref, dest_ref, x_ref, gt_ref, ng_ref, o_ref, acc_ref):
    i = pl.program_id(0)
    flags = pf_ref[i]

    @pl.when((flags & 1) != 0)
    def _():
        acc_ref[...] = jnp.zeros_like(acc_ref)

    @pl.when((flags & 4) != 0)
    def _():
        slot = pb_ref[i] * MOE_ROWS + lax.broadcasted_iota(I32, (dest_ref.shape[0], MOE_ROWS), 1)
        hit = (dest_ref[:, 0:1] == slot) | (dest_ref[:, 1:2] == slot)
        acc_ref[...] += jnp.dot(hit.astype(BF16), yb_ref[...], preferred_element_type=F32)

    @pl.when((flags & 2) != 0)
    def _():
        y = x_ref[...] + gt_ref[...] * acc_ref[...]
        o_ref[...] = y * lax.rsqrt(jnp.mean(y * y, axis=-1, keepdims=True) + EPS) * ng_ref[...]


def moe_combine_norm(yb, dest, pair_block, pair_tile, pair_flags, x, gate, norm_g):
    t, d = x.shape
    tm = min(ROW_TILE, t)
    n_pairs = pair_block.shape[0]
    return pl.pallas_call(
        _combine_kernel,
        grid_spec=pltpu.PrefetchScalarGridSpec(
            num_scalar_prefetch=3, grid=(n_pairs,),
            in_specs=[pl.BlockSpec((MOE_ROWS, d), lambda i, pb, pt, pf: (pb[i], 0)),
                      pl.BlockSpec((tm, 2), lambda i, pb, pt, pf: (pt[i], 0)),
                      pl.BlockSpec((tm, d), lambda i, pb, pt, pf: (pt[i], 0)),
                      pl.BlockSpec((1, d), lambda i, pb, pt, pf: (0, 0)),
                      pl.BlockSpec((1, d), lambda i, pb, pt, pf: (0, 0))],
            out_specs=pl.BlockSpec((tm, d), lambda i, pb, pt, pf: (pt[i], 0)),
            scratch_shapes=[pltpu.VMEM((tm, d), F32)]),
        out_shape=jax.ShapeDtypeStruct((t, d), F32),
        compiler_params=_params("arbitrary"), name="moe_combine",
    )(pair_block, pair_tile, pair_flags, yb, dest, x, gate.reshape(1, d), norm_g.reshape(1, d))


def _pair_lists(lo, hi, nonempty, n_pairs):
    cnt = jnp.where(nonempty, hi - lo + 1, 1)
    end = jnp.cumsum(cnt)
    start = end - cnt
    i = jnp.arange(n_pairs, dtype=I32)
    ic = jnp.minimum(i, end[-1] - 1)
    grp = jnp.sum((end[None, :] <= ic[:, None]).astype(I32), axis=1)
    off = ic - start[grp]
    member = jnp.where(nonempty[grp], lo[grp] + off, 0).astype(I32)
    return grp, member, off == 0, off == cnt[grp] - 1, nonempty[grp], i < end[-1]


def _pair_flags(first, last, data, valid):
    flags = jnp.where(first, 1, 0) | jnp.where(last, 2, 0) | jnp.where(data, 4, 0)
    return jnp.where(valid, flags, 0).astype(I32)


def moe_plan(top_idx, rank, tile_cnt, n_tok):
    tm = min(ROW_TILE, n_tok)
    n_tiles = n_tok // tm
    n_blocks = 2 * n_tok // MOE_ROWS + N_EXPERTS
    experts = jnp.arange(N_EXPERTS, dtype=I32)
    tile_off = jnp.cumsum(tile_cnt, axis=0) - tile_cnt
    counts = jnp.sum(tile_cnt, axis=0)
    padded = (counts + MOE_ROWS - 1) // MOE_ROWS * MOE_ROWS
    pad_end = jnp.cumsum(padded)
    base = (pad_end - padded)[None, :] + tile_off
    hot = top_idx[:, :, None] == experts[None, None, :]
    dest = (jnp.sum(jnp.where(hot, jnp.repeat(base, tm, axis=0)[:, None, :], 0), axis=-1) + rank).astype(I32)
    blk = jnp.arange(n_blocks, dtype=I32)
    block_expert = jnp.minimum(jnp.sum((pad_end[None, :] <= (blk * MOE_ROWS)[:, None]).astype(I32), axis=1),
                               N_EXPERTS - 1)
    n_pairs = N_EXPERTS * n_tiles + n_blocks
    c_grp, c_block, first, last, data, valid = _pair_lists(
        (base // MOE_ROWS).reshape(-1), ((base + tile_cnt - 1) // MOE_ROWS).reshape(-1), (tile_cnt > 0).reshape(-1),
        n_pairs)
    c_flags = _pair_flags(first & (c_grp % N_EXPERTS == 0), last & (c_grp % N_EXPERTS == N_EXPERTS - 1), data, valid)
    mine = block_expert[:, None, None] == experts[None, None, :]
    base_b = jnp.sum(jnp.where(mine, base[None], 0), axis=-1)
    cnt_b = jnp.sum(jnp.where(mine, tile_cnt[None], 0), axis=-1)
    sends = (cnt_b > 0) & (base_b + cnt_b > (blk * MOE_ROWS)[:, None]) & (base_b < ((blk + 1) * MOE_ROWS)[:, None])
    tiles = jnp.arange(n_tiles, dtype=I32)[None, :]
    g_block, g_tile, *g_bits = _pair_lists(jnp.min(jnp.where(sends, tiles, n_tiles), axis=1),
                                           jnp.max(jnp.where(sends, tiles, -1), axis=1), jnp.any(sends, axis=1), n_pairs)
    return dict(dest=dest, block_expert=block_expert.astype(I32), n_slots=n_blocks * MOE_ROWS,
                n_used=(pad_end[-1:] // MOE_ROWS).astype(I32),
                gather=(g_block, g_tile, _pair_flags(*g_bits)), combine=(c_block, c_grp // N_EXPERTS, c_flags))


def _mods(mod_all, layer, row, d):
    m = mod_all[layer, row]
    return tuple(m[j * d:(j + 1) * d] for j in range(N_MOD))


def _even_layer(x_lat, x_ctx, mods_lat, mods_ctx, norm1_g, norm2_g, w):
    d = x_lat.shape[1]
    hw = DN_HEADS * DN_DH
    states = dict(dn=[jnp.zeros((DN_HEADS, DN_DH, DN_DH), F32)] * 2, lru=[jnp.zeros((8, hw), F32)] * 2)
    outs = []
    for x, mods in ((x_ctx, mods_ctx), (x_lat, mods_lat)):
        sh1, sc1, gt1, sh2, sc2, gt2 = mods
        p, pg = norm_proj(x, norm1_g, sh1, sc1, w["w_main"], w["w_gate"])
        q, k, v, xc, gx, gxt = even_prep(p, pg, w["conv_w"], w["conv_b"], w["a_log"], w["dt_bias"])
        u, wm, qd, kd, qk, gl = deltanet_prep(q, k, v, gx, gxt)
        o, hl = [], []
        for dirn in range(2):
            o_d, states["dn"][dirn] = deltanet_scan(u, wm, qd, kd, qk, gl, states["dn"][dirn], d=dirn)
            h_d, states["lru"][dirn] = lru_scan(xc, w["lru_wa"][dirn], w["lru_wx"][dirn], w["lru_ba"][dirn],
                                                w["lru_bx"][dirn], w["lru_lam"][dirn], states["lru"][dirn], d=dirn)
            o.append(o_d)
            hl.append(h_d)
        x = even_finish(o[0], o[1], hl[0], hl[1], p, w["dn_norm_g"], w["w_out"], x, gt1)
        h2 = norm_mod(x, norm2_g, sh2, sc2)
        tm = _dense_rows(h2.shape[0])
        be = jnp.zeros((h2.shape[0] // tm,), I32)
        every = jnp.full((1,), h2.shape[0] // tm, I32)
        hh = ffn_up(h2, w["ffn_gate"], w["ffn_up"], be, every, tm=tm)
        x = ffn_down_residual(hh, w["ffn_down"], be, every, x, gt2, tm=tm)
        outs.append(x)
    return outs[1], outs[0]


def _odd_layer_last(x_lat, x_ctx, mods_lat, mods_ctx, norm1_g, norm2_g, final_g, w):
    dk, dv = w["dk"], w["dv"]
    states = [jnp.zeros((GLA_HEADS, dv, dk), F32)] * 2
    sh1, sc1 = mods_ctx[0], mods_ctx[1]
    p, pg = norm_proj(x_ctx, norm1_g, sh1, sc1, w["w_main"], w["w_gate"])
    for dirn in range(2):
        _, states[dirn] = gla_scan(p, pg, w["wg_pad"][dirn], w["bg"][dirn], states[dirn], d=dirn, dk=dk, dv=dv)
    sh1, sc1, gt1, sh2, sc2, gt2 = mods_lat
    p, pg = norm_proj(x_lat, norm1_g, sh1, sc1, w["w_main"], w["w_gate"], raster=True)
    o = [gla_scan(p, pg, w["wg_pad"][dirn], w["bg"][dirn], states[dirn], d=dirn, dk=dk, dv=dv)[0]
         for dirn in range(2)]
    x = odd_finish(o[0], o[1], p, w["gla_norm_g"], w["w_out"], x_lat, gt1)
    n_tok = x.shape[0]
    h2, top_idx, top_w, rank, tile_cnt = route(x, norm2_g, sh2, sc2, w["router_w"], w["router_b"])
    plan = moe_plan(top_idx, rank, tile_cnt, n_tok)
    xb, slot_w = moe_gather(h2, plan["dest"].T, top_w.T, plan["n_slots"], *plan["gather"])
    hh = ffn_up(xb, w["exp_gate"], w["exp_up"], plan["block_expert"], plan["n_used"], tm=MOE_ROWS)
    yb = ffn_down_scaled(hh, w["exp_down"], plan["block_expert"], plan["n_used"], slot_w)
    return moe_combine_norm(yb, plan["dest"], *plan["combine"], x, gt2, final_g)


def kernel(x, c, ctx, c_ctx, mod_w, mod_b, norm1_g, norm2_g, ev_w_in, ev_conv_qkv, ev_dn_a_log, ev_dn_dt_bias,
           ev_dn_norm_g, ev_lru_conv_w, ev_lru_conv_b, ev_lru_wa, ev_lru_ba, ev_lru_wx, ev_lru_bx, ev_lru_lambda,
           ev_w_out, ev_ffn_w_gate, ev_ffn_w_up, ev_ffn_w_down, od_w_in, od_gla_wg2, od_gla_bg, od_gla_norm_g,
           od_w_out, od_router_w, od_router_b, od_exp_w_gate, od_exp_w_up, od_exp_w_down, final_norm_g):
    b_, length, d = x.shape
    assert b_ == 1 and mod_w.shape[0] == 2, "this kernel implements the batch-1, depth-2 configuration"
    hw = DN_HEADS * DN_DH
    x_lat, x_ctx = x[0], ctx[0]

    cond8 = jnp.zeros((8, d), F32).at[0].set(c[0]).at[1].set(c_ctx)
    mod_all = adaln_all(cond8, mod_w, mod_b)

    w_in = ev_w_in[0]
    qkv_w, z0, ab0, xr0, gr0 = 3 * hw, 3 * hw, 4 * hw, 4 * hw + 4 * DN_HEADS, 5 * hw + 4 * DN_HEADS
    w_main = jnp.concatenate([w_in[:, :qkv_w], w_in[:, xr0:xr0 + hw], w_in[:, z0:z0 + hw], w_in[:, gr0:gr0 + hw]],
                             axis=1).astype(BF16)
    w_gate = jnp.zeros((d, LANES), BF16).at[:, :4 * DN_HEADS].set(w_in[:, ab0:ab0 + 4 * DN_HEADS].astype(BF16))
    pad16 = lambda t: jnp.zeros((1, LANES), F32).at[0, :2 * DN_HEADS].set(t.reshape(-1))
    ev = dict(
        w_main=w_main, w_gate=w_gate,
        conv_w=jnp.concatenate([ev_conv_qkv[0], ev_lru_conv_w[0]], axis=1),
        conv_b=jnp.concatenate([jnp.zeros((qkv_w,), F32), ev_lru_conv_b[0]]).reshape(1, -1),
        a_log=pad16(ev_dn_a_log[0]), dt_bias=pad16(ev_dn_dt_bias[0]),
        dn_norm_g=ev_dn_norm_g[0],
        lru_wa=ev_lru_wa[0].astype(BF16), lru_wx=ev_lru_wx[0].astype(BF16),
        lru_ba=ev_lru_ba[0].reshape(2, 1, hw), lru_bx=ev_lru_bx[0].reshape(2, 1, hw),
        lru_lam=ev_lru_lambda[0].reshape(2, 1, hw),
        w_out=ev_w_out[0].astype(BF16),
        ffn_gate=ev_ffn_w_gate.astype(BF16), ffn_up=ev_ffn_w_up.astype(BF16), ffn_down=ev_ffn_w_down.astype(BF16),
    )
    x_lat, x_ctx = _even_layer(x_lat, x_ctx, _mods(mod_all, 0, 0, d), _mods(mod_all, 0, 1, d),
                               norm1_g[0], norm2_g[0], ev)

    w_in = od_w_in[0]
    qk_w = od_gla_wg2.shape[-1]
    v_w = od_w_out.shape[1]
    main_w = 2 * qk_w + 2 * v_w
    wg_pad = jnp.zeros((2, LANES, qk_w), F32)
    for dirn in range(2):
        wg_pad = wg_pad.at[dirn, dirn * GLA_RANK:(dirn + 1) * GLA_RANK].set(od_gla_wg2[0, dirn])
    od = dict(
        w_main=w_in[:, :main_w].astype(BF16),
        w_gate=jnp.zeros((d, LANES), BF16).at[:, :2 * GLA_RANK].set(w_in[:, main_w:].astype(BF16)),
        wg_pad=wg_pad, bg=od_gla_bg[0].reshape(2, 1, qk_w), gla_norm_g=od_gla_norm_g[0],
        w_out=od_w_out[0].astype(BF16), router_w=od_router_w[0], router_b=od_router_b[0],
        exp_gate=od_exp_w_gate[0].astype(BF16), exp_up=od_exp_w_up[0].astype(BF16),
        exp_down=od_exp_w_down[0].astype(BF16),
        dk=qk_w // GLA_HEADS, dv=v_w // GLA_HEADS,
    )
    return _odd_layer_last(x_lat, x_ctx, _mods(mod_all, 1, 0, d), _mods(mod_all, 1, 1, d),
                           norm1_g[1], norm2_g[1], final_norm_g, od)[None]
```

```python
import functools
import math

import jax
import jax.numpy as jnp
from jax import lax
from jax.experimental import pallas as pl
from jax.experimental.pallas import tpu as pltpu

F32 = jnp.float32
BF16 = jnp.bfloat16
I32 = jnp.int32
HI = lax.Precision.HIGHEST

EPS = 1e-6
N_MOD = 6
GRID_W = 64
CHUNK = 64
SUB = 8
LANES = 128
ROW_TILE = 256
DN_HEADS = 8
DN_DH = 128
LRU_BLOCKS = 8
LRU_BW = 128
LRU_C = 8.0
GLA_HEADS = 4
GLA_RANK = 16
GLA_TAU = 16.0
GLA_SAFE_STEP = 7.5
N_EXPERTS = 8
MOE_ROWS = 256
VMEM_LIMIT = 56 * 1024 * 1024

NT_DIMS = (((1,), (1,)), ((), ()))
TN_DIMS = (((0,), (0,)), ((), ()))


def _params(*sem):
    return pltpu.CompilerParams(dimension_semantics=sem, vmem_limit_bytes=VMEM_LIMIT)


def _softplus(x):
    return jnp.maximum(x, 0.0) + jnp.log1p(jnp.exp(-jnp.abs(x)))


def _silu(x):
    return x * jax.nn.sigmoid(x)


def _bdot(a, b):
    return jnp.dot(a.astype(BF16), b.astype(BF16), preferred_element_type=F32)


def _adaln_kernel(cond_ref, w_ref, b_ref, o_ref):
    s = _silu(cond_ref[...])
    o_ref[...] = jnp.dot(s, w_ref[...], precision=HI, preferred_element_type=F32) + b_ref[...]


def adaln_all(cond8, mod_w, mod_b):
    n_layers, d, n6 = mod_w.shape
    tn = 1024
    return pl.pallas_call(
        _adaln_kernel,
        grid=(n_layers, n6 // tn),
        in_specs=[pl.BlockSpec((8, d), lambda l, j: (0, 0)),
                  pl.BlockSpec((None, d, tn), lambda l, j: (l, 0, j)),
                  pl.BlockSpec((None, 1, tn), lambda l, j: (l, 0, j))],
        out_specs=pl.BlockSpec((None, 8, tn), lambda l, j: (l, 0, j)),
        out_shape=jax.ShapeDtypeStruct((n_layers, 8, n6), F32),
        compiler_params=_params("arbitrary", "arbitrary"),
        name="adaln",
    )(cond8, mod_w, mod_b.reshape(n_layers, 1, n6))


def _norm_mod_kernel(x_ref, g_ref, sh_ref, sc_ref, o_ref):
    x = x_ref[...]
    y = x * lax.rsqrt(jnp.mean(x * x, axis=-1, keepdims=True) + EPS)
    o_ref[...] = ((y * g_ref[...]) * (1.0 + sc_ref[...]) + sh_ref[...]).astype(o_ref.dtype)


def _raster_spec(rows, d):
    return pl.BlockSpec((rows, d), lambda c: (0, c))


def norm_mod(x, g, sh, sc):
    n, d = x.shape
    tm = min(ROW_TILE, n)
    vec = pl.BlockSpec((1, d), lambda i: (0, 0))
    row = pl.BlockSpec((tm, d), lambda i: (i, 0))
    return pl.pallas_call(
        _norm_mod_kernel, grid=(n // tm,), in_specs=[row, vec, vec, vec], out_specs=row,
        out_shape=jax.ShapeDtypeStruct((n, d), BF16),
        compiler_params=_params("parallel"), name="norm_mod",
    )(x, g.reshape(1, d), sh.reshape(1, d), sc.reshape(1, d))


def _dense_rows(r):
    return 2 * ROW_TILE if r % (2 * ROW_TILE) == 0 else min(ROW_TILE, r)


def _norm_proj_kernel(x_ref, g_ref, sh_ref, sc_ref, w_ref, wg_ref, p_ref, pg_ref, h_s, *, d):
    @pl.when(pl.program_id(1) == 0)
    def _():
        rows = x_ref.shape[0]
        for col in range(x_ref.shape[1] // d):
            x = x_ref[:, col * d:(col + 1) * d]
            y = x * lax.rsqrt(jnp.mean(x * x, axis=-1, keepdims=True) + EPS)
            h_s[col * rows:(col + 1) * rows, :] = ((y * g_ref[...]) * (1.0 + sc_ref[...]) + sh_ref[...]).astype(BF16)
        pg_ref[...] = jnp.dot(h_s[...], wg_ref[...], preferred_element_type=F32)

    p_ref[...] = jnp.dot(h_s[...], w_ref[...], preferred_element_type=F32)


def norm_proj(x, g, sh, sc, w_main, w_gate, *, raster=False, tn=1024):
    n, d = x.shape
    n_main = w_main.shape[1]
    tm = next(t for t in (4 * ROW_TILE, 2 * ROW_TILE, min(ROW_TILE, n)) if n % t == 0)
    if raster:
        rows = n // GRID_W
        n_col = max(tm // rows, 1)
        tm = n_col * rows
        x_in, x_spec = x.reshape(rows, GRID_W * d), pl.BlockSpec((rows, n_col * d), lambda i, j: (0, i))
    else:
        x_in, x_spec = x, pl.BlockSpec((tm, d), lambda i, j: (i, 0))
    vec = pl.BlockSpec((1, d), lambda i, j: (0, 0))
    return pl.pallas_call(
        functools.partial(_norm_proj_kernel, d=d), grid=(n // tm, n_main // tn),
        in_specs=[x_spec, vec, vec, vec, pl.BlockSpec((d, tn), lambda i, j: (0, j)),
                  pl.BlockSpec((d, LANES), lambda i, j: (0, 0))],
        out_specs=[pl.BlockSpec((tm, tn), lambda i, j: (i, j)), pl.BlockSpec((tm, LANES), lambda i, j: (i, 0))],
        out_shape=[jax.ShapeDtypeStruct((n, n_main), F32), jax.ShapeDtypeStruct((n, LANES), F32)],
        scratch_shapes=[pltpu.VMEM((tm, d), BF16)],
        compiler_params=_params("parallel", "arbitrary"), name="norm_proj",
    )(x_in, g.reshape(1, d), sh.reshape(1, d), sc.reshape(1, d), w_main, w_gate)


def _evprep_kernel(p_ref, prev_ref, next_ref, pg_ref, cw_ref, cb_ref, alog_ref, dtb_ref,
                   q_ref, k_ref, v_ref, xc_ref, gx_ref, gxt_ref, ext_ref, *, n_tiles):
    i = pl.program_id(0)
    tm, width = p_ref.shape
    ext_ref[8:8 + tm, :] = p_ref[...]
    ext_ref[0:8, :] = jnp.where(i > 0, prev_ref[...], 0.0)
    ext_ref[8 + tm:16 + tm, :] = jnp.where(i < n_tiles - 1, next_ref[...], 0.0)
    acc = ext_ref[pl.ds(6, tm), :] * cw_ref[0:1, :]
    for j in range(1, 4):
        acc = acc + ext_ref[pl.ds(6 + j, tm), :] * cw_ref[j:j + 1, :]
    acc = acc + cb_ref[...]
    qk_w = DN_HEADS * DN_DH
    for hd in range(DN_HEADS):
        for part, ref, scale in ((0, q_ref, DN_DH ** -0.5), (1, k_ref, 1.0)):
            lo = part * qk_w + hd * DN_DH
            t = _silu(acc[:, lo:lo + DN_DH])
            t = t * lax.rsqrt(jnp.sum(t * t, axis=-1, keepdims=True) + EPS)
            ref[:, hd * DN_DH:(hd + 1) * DN_DH] = t * scale
    v_ref[...] = _silu(acc[:, 2 * qk_w:3 * qk_w])
    xc_ref[...] = acc[:, 3 * qk_w:]
    pg = pg_ref[...]
    lane = lax.broadcasted_iota(I32, pg.shape, 1)
    g = -jnp.exp(alog_ref[...]) * _softplus(pg + dtb_ref[...])
    ri = lax.broadcasted_iota(I32, (tm, tm), 0)
    ci = lax.broadcasted_iota(I32, (tm, tm), 1)
    same = (ri // CHUNK) == (ci // CHUNK)
    cum_f = _dot01((same & (ci <= ri)).astype(BF16), g)
    cum_b = _dot01((same & (ci >= ri)).astype(BF16), g)
    tot = _dot01(same.astype(BF16), g)
    gc = jnp.where(lane < DN_HEADS, cum_f, cum_b)
    gx = jnp.where(lane < 2 * DN_HEADS, gc,
                   jnp.where(lane < 4 * DN_HEADS, jax.nn.sigmoid(pg), pltpu.roll(tot, 4 * DN_HEADS, 1)))
    gx_ref[...] = gx
    gxt_ref[...] = gx.T


def _dot01(m01, x):
    x1 = x.astype(BF16)
    r1 = x - x1.astype(F32)
    x2 = r1.astype(BF16)
    x3 = (r1 - x2.astype(F32)).astype(BF16)
    dot = lambda t: jnp.dot(m01, t, preferred_element_type=F32)
    return dot(x1) + dot(x2) + dot(x3)


def even_prep(p, pg, conv_w, conv_b, a_log, dt_bias):
    r = p.shape[0]
    width = conv_w.shape[1]
    tm = min(ROW_TILE, r)
    n_tiles = r // tm
    hb = tm // 8
    out_w = DN_HEADS * DN_DH
    row = lambda w: pl.BlockSpec((tm, w), lambda i: (i, 0))
    vec = lambda w: pl.BlockSpec((1, w), lambda i: (0, 0))
    return pl.pallas_call(
        functools.partial(_evprep_kernel, n_tiles=n_tiles),
        grid=(n_tiles,),
        in_specs=[row(width),
                  pl.BlockSpec((8, width), lambda i: (jnp.maximum(i * hb - 1, 0), 0)),
                  pl.BlockSpec((8, width), lambda i: (jnp.minimum((i + 1) * hb, r // 8 - 1), 0)),
                  row(LANES),
                  pl.BlockSpec((4, width), lambda i: (0, 0)), vec(width), vec(LANES), vec(LANES)],
        out_specs=[row(out_w), row(out_w), row(out_w), row(out_w), row(LANES),
                   pl.BlockSpec((LANES, tm), lambda i: (0, i))],
        out_shape=[jax.ShapeDtypeStruct((r, out_w), F32)] * 4
        + [jax.ShapeDtypeStruct((r, LANES), F32), jax.ShapeDtypeStruct((LANES, r), F32)],
        scratch_shapes=[pltpu.VMEM((tm + 16, width), F32)],
        compiler_params=_params("parallel"), name="even_prep",
    )(p, p, p, pg, conv_w, conv_b, a_log, dt_bias)


DN_BASE = 16
DN_HEADS_PER_STEP = 2


def _dnprep_kernel(q_ref, k_ref, v_ref, gx_ref, gxt_ref, u_ref, w_ref, qd_ref, kd_ref, qk_ref, gl_ref, *, n_chunks):
    c = CHUNK
    tm = q_ref.shape[0]
    ri = lax.broadcasted_iota(I32, (tm, tm), 0)
    ci = lax.broadcasted_iota(I32, (tm, tm), 1)
    same = lambda s: (ri // s) == (ci // s)
    eye = (ri == ci).astype(F32)
    lane = lax.broadcasted_iota(I32, (tm, LANES), 1)
    gx = gx_ref[...]
    pick = lambda idx: jnp.sum(jnp.where(lane == idx, gx, 0.0), axis=1, keepdims=True)
    chains = []
    for hh in range(DN_HEADS_PER_STEP):
        hd = pl.program_id(1) * DN_HEADS_PER_STEP + hh
        cols = slice(hh * DN_DH, (hh + 1) * DN_DH)
        q = q_ref[:, cols]
        k = k_ref[:, cols]
        kb16 = k.astype(BF16)
        gram_k = lax.dot_general(kb16, kb16, NT_DIMS, preferred_element_type=F32)
        gram_q = lax.dot_general(q.astype(BF16), kb16, NT_DIMS, preferred_element_type=F32)
        for d in range(2):
            incl = same(c) & ((ci >= ri) if d else (ci <= ri))
            gcol = pick(d * DN_HEADS + hd)
            bcol = pick((2 + d) * DN_HEADS + hd)
            tcol = pick((4 + d) * DN_HEADS + hd)
            grow = gxt_ref[pl.ds(d * DN_HEADS + hd, 1), :]
            decay = jnp.where(incl, jnp.exp(gcol - grow), 0.0)
            a = jnp.where(ri == ci, 0.0, gram_k * bcol * decay)
            eg = jnp.exp(gcol)
            qk = (gram_q * decay).astype(BF16)
            for n in range(n_chunks):
                qk_ref[d, hh, n * c:(n + 1) * c, :] = qk[n * c:(n + 1) * c, n * c:(n + 1) * c]
                gl_ref[d, hh, n] = jnp.broadcast_to(jnp.exp(tcol[n * c:n * c + 1, :]), (8, LANES))
            qd_ref[d, :, cols] = (q * eg).astype(BF16)
            kd_ref[d, :, cols] = (k * jnp.exp(tcol - gcol)).astype(BF16)
            diag = jnp.where(same(DN_BASE), a, 0.0)
            chains.append(dict(d=d, cols=cols, a=a, t=eye - diag, p=diag, scale=bcol, scale_k=bcol * eg))
    size = 2
    while size < DN_BASE:
        for ch in chains:
            ch["p"] = _bdot(ch["p"], ch["p"])
        for ch in chains:
            ch["t"] = ch["t"] + _bdot(ch["t"], ch["p"])
        size *= 2
    size = DN_BASE
    while size < c:
        couple = same(2 * size) & ~same(size)
        for ch in chains:
            ch["et"] = _bdot(jnp.where(couple, ch["a"], 0.0), ch["t"])
        for ch in chains:
            ch["t"] = ch["t"] - _bdot(ch["t"], ch["et"])
        size *= 2
    for ch in chains:
        d, cols = ch["d"], ch["cols"]
        x = _bdot(ch["t"], jnp.concatenate([v_ref[:, cols] * ch["scale"], k_ref[:, cols] * ch["scale_k"]], axis=1))
        u_ref[d, :, cols] = x[:, :DN_DH]
        w_ref[d, :, cols] = x[:, DN_DH:].astype(BF16)


def deltanet_prep(q, k, v, gx, gxt):
    r = q.shape[0]
    tm = min(ROW_TILE, r)
    n_chunks = tm // CHUNK
    hw = DN_HEADS * DN_DH
    hps = DN_HEADS_PER_STEP
    head = pl.BlockSpec((tm, hps * DN_DH), lambda i, h: (i, h))
    dhead = pl.BlockSpec((2, tm, hps * DN_DH), lambda i, h: (0, i, h))
    return pl.pallas_call(
        functools.partial(_dnprep_kernel, n_chunks=n_chunks),
        grid=(r // tm, DN_HEADS // hps),
        in_specs=[head, head, head, pl.BlockSpec((tm, LANES), lambda i, h: (i, 0)),
                  pl.BlockSpec((LANES, tm), lambda i, h: (0, i))],
        out_specs=[dhead, dhead, dhead, dhead,
                   pl.BlockSpec((2, hps, tm, CHUNK), lambda i, h: (0, h, i, 0)),
                   pl.BlockSpec((2, hps, n_chunks, 8, LANES), lambda i, h: (0, h, i, 0, 0))],
        out_shape=[jax.ShapeDtypeStruct((2, r, hw), F32),
                   jax.ShapeDtypeStruct((2, r, hw), BF16),
                   jax.ShapeDtypeStruct((2, r, hw), BF16),
                   jax.ShapeDtypeStruct((2, r, hw), BF16),
                   jax.ShapeDtypeStruct((2, DN_HEADS, r, CHUNK), BF16),
                   jax.ShapeDtypeStruct((2, DN_HEADS, r // CHUNK, 8, LANES), F32)],
        compiler_params=_params("parallel", "parallel"), name="deltanet_prep",
    )(q, k, v, gx, gxt)


def _dnscan_kernel(*refs, n_steps):
    ins = (refs[0:6], refs[6:12])
    s0_ref, o_refs, sf_ref, s_ref = refs[12], refs[13:15], refs[15], refs[16]
    step = pl.program_id(0)

    @pl.when(step == 0)
    def _():
        s_ref[...] = s0_ref[...]

    c = CHUNK
    n_sub = ins[0][0].shape[0] // c
    col = lambda hd: slice(hd * DN_DH, (hd + 1) * DN_DH)
    dot = lambda a, b: jnp.dot(a, b, preferred_element_type=F32)
    for t in range(n_sub):
        chains = []
        for d in range(2):
            u_ref, w_ref, qd_ref, kd_ref, qk_ref, gl_ref = ins[d]
            n = n_sub - 1 - t if d else t
            rows = slice(n * c, (n + 1) * c)
            for hd in range(DN_HEADS):
                chains.append(dict(d=d, hd=hd, rows=rows, u=u_ref.at[rows, col(hd)], w=w_ref.at[rows, col(hd)],
                                   qd=qd_ref.at[rows, col(hd)], kd=kd_ref.at[rows, col(hd)],
                                   qk=qk_ref.at[hd, rows, :], gl=gl_ref.at[hd, n, 0:1, :]))
        for ch in chains:
            ch["s"] = s_ref[ch["d"], ch["hd"]]
            ch["sb"] = ch["s"].astype(BF16)
        for ch in chains:
            ch["ws"] = dot(ch["w"][...], ch["sb"])
        for ch in chains:
            ch["qs"] = dot(ch["qd"][...], ch["sb"])
        for ch in chains:
            ch["vb"] = (ch["u"][...] - ch["ws"]).astype(BF16)
        for ch in chains:
            o_refs[ch["d"]][ch["rows"], col(ch["hd"])] = ch["qs"] + dot(ch["qk"][...], ch["vb"])
        for ch in chains:
            ch["ds"] = lax.dot_general(ch["kd"][...], ch["vb"], TN_DIMS, preferred_element_type=F32)
        for ch in chains:
            s_ref[ch["d"], ch["hd"]] = ch["s"] * ch["gl"][...] + ch["ds"]

    @pl.when(step == n_steps - 1)
    def _():
        sf_ref[...] = s_ref[...]


def deltanet_scan(u, w, qd, kd, qk, gl, s0):
    r = u.shape[1]
    hw = DN_HEADS * DN_DH
    tm = 2 * CHUNK
    n_steps = r // tm
    state = pl.BlockSpec((2, DN_HEADS, DN_DH, DN_DH), lambda i: (0, 0, 0, 0))
    in_specs, out_specs = [], []
    for d in range(2):
        idx = (lambda i: n_steps - 1 - i) if d else (lambda i: i)
        big = pl.BlockSpec((None, tm, hw), lambda i, d=d, idx=idx: (d, idx(i), 0))
        in_specs += [big, big, big, big,
                     pl.BlockSpec((None, DN_HEADS, tm, CHUNK), lambda i, d=d, idx=idx: (d, 0, idx(i), 0)),
                     pl.BlockSpec((None, DN_HEADS, tm // CHUNK, 8, LANES), lambda i, d=d, idx=idx: (d, 0, idx(i), 0, 0))]
        out_specs.append(pl.BlockSpec((tm, hw), lambda i, idx=idx: (idx(i), 0)))
    return pl.pallas_call(
        functools.partial(_dnscan_kernel, n_steps=n_steps),
        grid=(n_steps,),
        in_specs=in_specs + [state],
        out_specs=out_specs + [state],
        out_shape=[jax.ShapeDtypeStruct((r, hw), F32), jax.ShapeDtypeStruct((r, hw), F32),
                   jax.ShapeDtypeStruct((2, DN_HEADS, DN_DH, DN_DH), F32)],
        scratch_shapes=[pltpu.VMEM((2, DN_HEADS, DN_DH, DN_DH), F32)],
        compiler_params=_params("arbitrary"), name="deltanet_scan",
    )(u, w, qd, kd, qk, gl, u, w, qd, kd, qk, gl, s0)


def _lru_kernel(xf_ref, xb_ref, wa_ref, wx_ref, ba_ref, bx_ref, lam_ref, h0_ref, hf_ref, hb_ref, hl_ref,
                a_s, b_s, carry_s, *, n_steps):
    step = pl.program_id(0)

    @pl.when(step == 0)
    def _():
        carry_s[...] = h0_ref[...]

    tm = xf_ref.shape[0]
    for d, xc_ref in enumerate((xf_ref, xb_ref)):
        sp = _softplus(-lam_ref[d])
        for n in range(LRU_BLOCKS):
            cols = slice(n * LRU_BW, (n + 1) * LRU_BW)
            xb = xc_ref[:, cols]
            xbb = xb.astype(BF16)
            r = jax.nn.sigmoid(jnp.dot(xbb, wa_ref[d, n], preferred_element_type=F32) + ba_ref[d, :, cols])
            gi = jax.nn.sigmoid(jnp.dot(xbb, wx_ref[d, n], preferred_element_type=F32) + bx_ref[d, :, cols])
            log_a = -LRU_C * r * sp[:, cols]
            a = jnp.exp(log_a)
            a_s[d, :, cols] = a
            b_s[d, :, cols] = jnp.sqrt(-jnp.tanh(log_a) * (a * a + 1.0)) * (gi * xb)

    rid = lax.broadcasted_iota(I32, (8, a_s.shape[2]), 0)
    n_groups = tm // 8
    h_refs = (hf_ref, hb_ref)

    def group(gidx, carries):
        out = []
        for d in range(2):
            g = (n_groups - 1 - gidx) if d else gidx
            base = pl.multiple_of(g * 8, 8)
            a = a_s[d, pl.ds(base, 8), :]
            b = b_s[d, pl.ds(base, 8), :]
            for sh in (1, 2, 4):
                if d:
                    keep = rid < 8 - sh
                    a_n = jnp.where(keep, pltpu.roll(a, 8 - sh, 0), 1.0)
                    b_n = jnp.where(keep, pltpu.roll(b, 8 - sh, 0), 0.0)
                else:
                    keep = rid >= sh
                    a_n = jnp.where(keep, pltpu.roll(a, sh, 0), 1.0)
                    b_n = jnp.where(keep, pltpu.roll(b, sh, 0), 0.0)
                b = a * b_n + b
                a = a * a_n
            h = a * carries[d] + b
            h_refs[d][pl.ds(base, 8), :] = h
            edge = h[0:1, :] if d else h[7:8, :]
            out.append(jnp.broadcast_to(edge, h.shape))
        return tuple(out)

    carries = lax.fori_loop(0, n_groups, group, (carry_s[0], carry_s[1]))
    carry_s[0] = carries[0]
    carry_s[1] = carries[1]

    @pl.when(step == n_steps - 1)
    def _():
        hl_ref[...] = carry_s[...]


def lru_scan(xc, wa, wx, ba, bx, lam, h0):
    r, width = xc.shape
    tm = min(ROW_TILE, r)
    n_steps = r // tm
    vec = pl.BlockSpec((2, 1, width), lambda i: (0, 0, 0))
    wspec = pl.BlockSpec((2, LRU_BLOCKS, LRU_BW, LRU_BW), lambda i: (0, 0, 0, 0))
    st = pl.BlockSpec((2, 8, width), lambda i: (0, 0, 0))
    fwd = pl.BlockSpec((tm, width), lambda i: (i, 0))
    bwd = pl.BlockSpec((tm, width), lambda i: (n_steps - 1 - i, 0))
    return pl.pallas_call(
        functools.partial(_lru_kernel, n_steps=n_steps),
        grid=(n_steps,),
        in_specs=[fwd, bwd, wspec, wspec, vec, vec, vec, st],
        out_specs=[fwd, bwd, st],
        out_shape=[jax.ShapeDtypeStruct((r, width), F32), jax.ShapeDtypeStruct((r, width), F32),
                   jax.ShapeDtypeStruct((2, 8, width), F32)],
        scratch_shapes=[pltpu.VMEM((2, tm, width), F32), pltpu.VMEM((2, tm, width), F32),
                        pltpu.VMEM((2, 8, width), F32)],
        compiler_params=_params("arbitrary"), name="lru_scan",
    )(xc, xc, wa, wx, ba, bx, lam, h0)


def _gelu_tanh(x):
    return 0.5 * x * (1.0 + jnp.tanh(math.sqrt(2.0 / math.pi) * (x + 0.044715 * (x * x * x))))


def _evfin_kernel(of_ref, ob_ref, hf_ref, hb_ref, zg_ref, ng_ref, wout_ref, x_ref, gt_ref, o_ref, mix_s):
    hw = DN_HEADS * DN_DH
    for hd in range(DN_HEADS):
        cols = slice(hd * DN_DH, (hd + 1) * DN_DH)
        o = of_ref[:, cols] + ob_ref[:, cols]
        y = o * lax.rsqrt(jnp.mean(o * o, axis=-1, keepdims=True) + EPS) * ng_ref[...]
        mix_s[:, cols] = (y * _silu(zg_ref[:, cols])).astype(BF16)
    mix_s[:, hw:] = ((hf_ref[...] + hb_ref[...]) * _gelu_tanh(zg_ref[:, hw:])).astype(BF16)
    y = jnp.dot(mix_s[...], wout_ref[...], preferred_element_type=F32)
    o_ref[...] = x_ref[...] + gt_ref[...] * y


def even_finish(o_f, o_b, h_f, h_b, p, norm_g, w_out, x, gate):
    r, d = x.shape
    tm = min(ROW_TILE, r)
    hw = DN_HEADS * DN_DH
    row = lambda w: pl.BlockSpec((tm, w), lambda i: (i, 0))
    return pl.pallas_call(
        _evfin_kernel, grid=(r // tm,),
        in_specs=[row(hw), row(hw), row(hw), row(hw),
                  pl.BlockSpec((tm, 2 * hw), lambda i: (i, 2)),
                  pl.BlockSpec((1, DN_DH), lambda i: (0, 0)),
                  pl.BlockSpec(w_out.shape, lambda i: (0, 0)),
                  row(d), pl.BlockSpec((1, d), lambda i: (0, 0))],
        out_specs=row(d),
        out_shape=jax.ShapeDtypeStruct((r, d), F32),
        scratch_shapes=[pltpu.VMEM((tm, 2 * hw), BF16)],
        compiler_params=_params("parallel"), name="even_finish",
    )(o_f, o_b, h_f, h_b, p, norm_g.reshape(1, DN_DH), w_out, x, gate.reshape(1, d))


def _odfin_kernel(of_ref, ob_ref, go_ref, ng_ref, wout_ref, x_ref, gt_ref, o_ref, mix_s, *, dv):
    for hd in range(GLA_HEADS):
        cols = slice(hd * dv, (hd + 1) * dv)
        o = of_ref[:, cols] + ob_ref[:, cols]
        y = o * lax.rsqrt(jnp.mean(o * o, axis=-1, keepdims=True) + EPS) * ng_ref[...]
        mix_s[:, cols] = (y * _silu(go_ref[:, cols])).astype(BF16)
    y = jnp.dot(mix_s[...], wout_ref[...], preferred_element_type=F32)
    o_ref[...] = x_ref[...] + gt_ref[...] * y


def odd_finish(o_f, o_b, p, norm_g, w_out, x, gate):
    r, d = x.shape
    rows = r // GRID_W
    vw = o_f.shape[1]
    dv = vw // GLA_HEADS
    row = lambda w: pl.BlockSpec((rows, w), lambda c: (c, 0))
    return pl.pallas_call(
        functools.partial(_odfin_kernel, dv=dv), grid=(GRID_W,),
        in_specs=[row(vw), row(vw),
                  pl.BlockSpec((rows, vw), lambda c: (c, 2)),
                  pl.BlockSpec((1, dv), lambda c: (0, 0)),
                  pl.BlockSpec(w_out.shape, lambda c: (0, 0)),
                  _raster_spec(rows, d), pl.BlockSpec((1, d), lambda c: (0, 0))],
        out_specs=_raster_spec(rows, d),
        out_shape=jax.ShapeDtypeStruct((rows, GRID_W * d), F32),
        scratch_shapes=[pltpu.VMEM((rows, vw), BF16)],
        compiler_params=_params("parallel"), name="odd_finish",
    )(o_f, o_b, p, norm_g.reshape(1, dv), w_out, x.reshape(rows, GRID_W * d), gate.reshape(1, d)).reshape(r, d)


def _gla_kernel(*refs, n_steps, dk):
    ins = (refs[0:6], refs[6:12])
    s0_ref, o_refs, sf_ref, s_ref, gc_s = refs[12], refs[13:15], refs[15], refs[16], refs[17]
    step = pl.program_id(1)

    @pl.when(step == 0)
    def _():
        s_ref[...] = s0_ref[...]

    c = CHUNK
    tm = ins[0][0].shape[0]
    n_chunks = tm // c
    n_sub = c // SUB
    ri = lax.broadcasted_iota(I32, (tm, tm), 0)
    ci = lax.broadcasted_iota(I32, (tm, tm), 1)
    r64 = lax.broadcasted_iota(I32, (c, c), 0)
    c64 = lax.broadcasted_iota(I32, (c, c), 1)
    rr = lax.broadcasted_iota(I32, (c, 1), 0) % SUB
    units = []
    g_min = None
    for d in range(2):
        q_ref, k_ref, v_ref, gd_ref, wg_ref, bg_ref = ins[d]
        tri = (((ri // c) == (ci // c)) & ((ci >= ri) if d else (ci <= ri))).astype(BF16)
        logit = _bdot(gd_ref[...], wg_ref[...]) + bg_ref[...]
        g = -_softplus(-logit) * (1.0 / GLA_TAU)
        gc = _dot01(tri, g)
        gc_s[d] = gc
        g_min = jnp.min(g) if g_min is None else jnp.minimum(g_min, jnp.min(g))
        q = q_ref[...] * dk ** -0.5
        for n in range(n_chunks):
            rows = slice(n * c, (n + 1) * c)
            units.append(dict(d=d, n=n, rows=rows, q=q[rows], k=k_ref[rows, :], g=g[rows], gc=gc[rows],
                              att=jnp.zeros((c, c), F32)))
    size = c // 2
    while size >= SUB:
        r_hi, c_hi = (r64 & size) != 0, (c64 & size) != 0
        same = (r64 // (2 * size)) == (c64 // (2 * size))
        pair = (same & r_hi & ~c_hi, same & ~r_hi & c_hi)
        for u in units:
            gc = u["gc"]
            pieces = []
            for b in range(c // (2 * size)):
                mid = b * 2 * size + size
                edge = gc[mid:mid + 1] if u["d"] else gc[mid - 1:mid]
                pieces.append(jnp.broadcast_to(edge, (2 * size, dk)))
            edge = pieces[0] if len(pieces) == 1 else jnp.concatenate(pieces, axis=0)
            qs = (u["q"] * jnp.exp(gc - edge)).astype(BF16)
            ks = (u["k"] * jnp.exp(edge - gc)).astype(BF16)
            u["att"] = u["att"] + jnp.where(pair[u["d"]],
                                            lax.dot_general(qs, ks, NT_DIMS, preferred_element_type=F32), 0.0)
        size //= 2

    def diag_direct():
        blocks = [jnp.zeros((c, c), F32) for _ in units]
        for jj in range(SUB):
            here = c64 == (r64 // SUB) * SUB + jj
            seen = (here & (rr >= jj), here & (rr <= jj))
            for i, u in enumerate(units):
                rep = lambda ref: jnp.concatenate(
                    [jnp.broadcast_to(ref[pl.ds(u["n"] * c + b * SUB + jj, 1), :], (SUB, dk)) for b in range(n_sub)],
                    axis=0)
                col = jnp.sum(u["q"] * rep(ins[u["d"]][1]) * jnp.exp(u["gc"] - rep(gc_s.at[u["d"]])),
                              axis=-1, keepdims=True)
                blocks[i] = jnp.where(seen[u["d"]], col, blocks[i])
        return tuple(blocks)

    def diag_factored():
        blocks = []
        same = (r64 // SUB) == (c64 // SUB)
        inside = (same & (c64 <= r64), same & (c64 >= r64))
        for u in units:
            before = (u["gc"] - u["g"]).reshape(n_sub, SUB, dk)
            edge = before[:, SUB - 1:SUB, :] if u["d"] else before[:, 0:1, :]
            edge = jnp.broadcast_to(edge, before.shape).reshape(c, dk)
            qs = (u["q"] * jnp.exp(u["gc"] - edge)).astype(BF16)
            ks = (u["k"] * jnp.exp(edge - u["gc"])).astype(BF16)
            blocks.append(jnp.where(inside[u["d"]],
                                    lax.dot_general(qs, ks, NT_DIMS, preferred_element_type=F32), 0.0))
        return tuple(blocks)

    diag = lax.cond(g_min >= -GLA_SAFE_STEP, diag_factored, diag_direct)
    for u in units:
        u["vb"] = ins[u["d"]][2][u["rows"], :].astype(BF16)
        u["tot"] = u["gc"][0:1] if u["d"] else u["gc"][c - 1:c]
    for i, u in enumerate(units):
        u["o"] = jnp.dot((u["att"] + diag[i]).astype(BF16), u["vb"], preferred_element_type=F32)
    for u in units:
        u["qd"] = (u["q"] * jnp.exp(u["gc"])).astype(BF16)
        u["ds"] = lax.dot_general(u["vb"], (u["k"] * jnp.exp(u["tot"] - u["gc"])).astype(BF16), TN_DIMS,
                                  preferred_element_type=F32)
    for t in range(n_chunks):
        for d in range(2):
            u = units[d * n_chunks + (n_chunks - 1 - t if d else t)]
            s = s_ref[d]
            o_refs[d][u["rows"], :] = u["o"] + lax.dot_general(u["qd"], s.astype(BF16), NT_DIMS,
                                                               preferred_element_type=F32)
            s_ref[d] = s * jnp.exp(u["tot"]) + u["ds"]

    @pl.when(step == n_steps - 1)
    def _():
        sf_ref[...] = s_ref[...]


def gla_scan(p, pg, wg_pad, bg, s0, *, dk, dv):
    r = p.shape[0]
    tm = min(ROW_TILE, r)
    n_steps = r // tm
    qk_blocks = GLA_HEADS
    v_block0 = 2 * GLA_HEADS * dk // dv
    state = pl.BlockSpec((2, None, dv, dk), lambda h, i: (0, h, 0, 0))
    in_specs, out_specs = [], []
    for d in range(2):
        idx = (lambda i: n_steps - 1 - i) if d else (lambda i: i)
        in_specs += [pl.BlockSpec((tm, dk), lambda h, i, idx=idx: (idx(i), h)),
                     pl.BlockSpec((tm, dk), lambda h, i, idx=idx: (idx(i), qk_blocks + h)),
                     pl.BlockSpec((tm, dv), lambda h, i, idx=idx: (idx(i), v_block0 + h)),
                     pl.BlockSpec((tm, LANES), lambda h, i, idx=idx: (idx(i), 0)),
                     pl.BlockSpec((None, LANES, dk), lambda h, i, d=d: (d, 0, h)),
                     pl.BlockSpec((None, 1, dk), lambda h, i, d=d: (d, 0, h))]
        out_specs.append(pl.BlockSpec((tm, dv), lambda h, i, idx=idx: (idx(i), h)))
    args = (p, p, p, pg, wg_pad, bg)
    return pl.pallas_call(
        functools.partial(_gla_kernel, n_steps=n_steps, dk=dk),
        grid=(GLA_HEADS, n_steps),
        in_specs=in_specs + [state],
        out_specs=out_specs + [state],
        out_shape=[jax.ShapeDtypeStruct((r, GLA_HEADS * dv), F32), jax.ShapeDtypeStruct((r, GLA_HEADS * dv), F32),
                   jax.ShapeDtypeStruct((2, GLA_HEADS, dv, dk), F32)],
        scratch_shapes=[pltpu.VMEM((2, dv, dk), F32), pltpu.VMEM((2, tm, dk), F32)],
        compiler_params=_params("parallel", "arbitrary"), name="gla_scan",
    )(*args, *args, s0)


def _when_block_used(nu_ref, o_ref, body):
    used = pl.program_id(1) < nu_ref[0]

    @pl.when(used)
    def _():
        body()

    @pl.when(jnp.logical_not(used))
    def _():
        o_ref[...] = jnp.zeros_like(o_ref)


def _ffn1_kernel(be_ref, nu_ref, x_ref, wg_ref, wu_ref, o_ref):
    def body():
        x = x_ref[...]
        a = jnp.dot(x, wg_ref[...], preferred_element_type=F32)
        b = jnp.dot(x, wu_ref[...], preferred_element_type=F32)
        o_ref[...] = (_silu(a) * b).astype(o_ref.dtype)

    _when_block_used(nu_ref, o_ref, body)


def ffn_up(x, w_gate, w_up, block_expert, n_used, *, tm, tn=1408):
    r, d = x.shape
    hidden = w_gate.shape[2]
    wspec = pl.BlockSpec((None, d, tn), lambda j, i, be, nu: (be[i], 0, j))
    return pl.pallas_call(
        _ffn1_kernel,
        grid_spec=pltpu.PrefetchScalarGridSpec(
            num_scalar_prefetch=2, grid=(hidden // tn, r // tm),
            in_specs=[pl.BlockSpec((tm, d), lambda j, i, be, nu: (i, 0)), wspec, wspec],
            out_specs=pl.BlockSpec((tm, tn), lambda j, i, be, nu: (i, j))),
        out_shape=jax.ShapeDtypeStruct((r, hidden), BF16),
        compiler_params=_params("parallel", "arbitrary"), name="ffn_up",
    )(block_expert, n_used, x, w_gate, w_up)


def _ffn2_res_kernel(be_ref, nu_ref, h_ref, w_ref, x_ref, gt_ref, o_ref):
    def body():
        y = jnp.dot(h_ref[...], w_ref[...], preferred_element_type=F32)
        o_ref[...] = x_ref[...] + gt_ref[...] * y

    _when_block_used(nu_ref, o_ref, body)


def _ffn2_scale_kernel(be_ref, nu_ref, h_ref, w_ref, sw_ref, o_ref):
    def body():
        y = jnp.dot(h_ref[...], w_ref[...], preferred_element_type=F32)
        o_ref[...] = (y * sw_ref[...]).astype(o_ref.dtype)

    _when_block_used(nu_ref, o_ref, body)


def ffn_down_residual(h, w_down, block_expert, n_used, x, gate, *, tm, tn=1024):
    r, hidden = h.shape
    d = w_down.shape[2]
    return pl.pallas_call(
        _ffn2_res_kernel,
        grid_spec=pltpu.PrefetchScalarGridSpec(
            num_scalar_prefetch=2, grid=(d // tn, r // tm),
            in_specs=[pl.BlockSpec((tm, hidden), lambda j, i, be, nu: (i, 0)),
                      pl.BlockSpec((None, hidden, tn), lambda j, i, be, nu: (be[i], 0, j)),
                      pl.BlockSpec((tm, tn), lambda j, i, be, nu: (i, j)),
                      pl.BlockSpec((1, tn), lambda j, i, be, nu: (0, j))],
            out_specs=pl.BlockSpec((tm, tn), lambda j, i, be, nu: (i, j))),
        out_shape=jax.ShapeDtypeStruct((r, d), F32),
        compiler_params=_params("parallel", "arbitrary"), name="ffn_down_residual",
    )(block_expert, n_used, h, w_down, x, gate.reshape(1, d))


def ffn_down_scaled(h, w_down, block_expert, n_used, slot_w, *, tn=1024):
    r, hidden = h.shape
    d = w_down.shape[2]
    tm = min(MOE_ROWS, r)
    return pl.pallas_call(
        _ffn2_scale_kernel,
        grid_spec=pltpu.PrefetchScalarGridSpec(
            num_scalar_prefetch=2, grid=(d // tn, r // tm),
            in_specs=[pl.BlockSpec((tm, hidden), lambda j, i, be, nu: (i, 0)),
                      pl.BlockSpec((None, hidden, tn), lambda j, i, be, nu: (be[i], 0, j)),
                      pl.BlockSpec((tm, 1), lambda j, i, be, nu: (i, 0))],
            out_specs=pl.BlockSpec((tm, tn), lambda j, i, be, nu: (i, j))),
        out_shape=jax.ShapeDtypeStruct((r, d), BF16),
        compiler_params=_params("parallel", "arbitrary"), name="ffn_down_scaled",
    )(block_expert, n_used, h, w_down, slot_w.reshape(r, 1))


def _router_kernel(x_ref, g_ref, sh_ref, sc_ref, rw_ref, rb_ref, h_ref, idx_ref, wt_ref, rank_ref, cnt_ref):
    x = x_ref[...]
    y = x * lax.rsqrt(jnp.mean(x * x, axis=-1, keepdims=True) + EPS)
    h = (y * g_ref[...]) * (1.0 + sc_ref[...]) + sh_ref[...]
    hb = h.astype(BF16)
    h_ref[...] = hb
    logits = jnp.dot(hb, rw_ref[...], preferred_element_type=F32) + rb_ref[...]
    lane = lax.broadcasted_iota(I32, logits.shape, 1)
    neg = jnp.float32(-jnp.inf)
    logits = jnp.where(lane < N_EXPERTS, logits, neg)
    m0 = jnp.max(logits, axis=-1, keepdims=True)
    i0 = jnp.min(jnp.where(logits == m0, lane, LANES), axis=-1, keepdims=True)
    rest = jnp.where(lane == i0, neg, logits)
    m1 = jnp.max(rest, axis=-1, keepdims=True)
    i1 = jnp.min(jnp.where(rest == m1, lane, LANES), axis=-1, keepdims=True)
    e1 = jnp.exp(m1 - m0)
    w0 = 1.0 / (1.0 + e1)
    idx_ref[...] = jnp.where(lane == 0, i0, jnp.where(lane == 1, i1, 0))
    wt_ref[...] = jnp.where(lane == 0, w0, jnp.where(lane == 1, e1 * w0, 0.0))
    tm = x.shape[0]
    hot = (lane == i0) | (lane == i1)
    earlier = (lax.broadcasted_iota(I32, (tm, tm), 1) < lax.broadcasted_iota(I32, (tm, tm), 0)).astype(BF16)
    before = jnp.dot(earlier, hot.astype(BF16), preferred_element_type=F32)
    r0 = jnp.sum(jnp.where(lane == i0, before, 0.0), axis=-1, keepdims=True)
    r1 = jnp.sum(jnp.where(lane == i1, before, 0.0), axis=-1, keepdims=True)
    rank_ref[...] = jnp.where(lane == 0, r0, jnp.where(lane == 1, r1, 0.0)).astype(I32)
    cnt_ref[...] = jnp.broadcast_to(jnp.sum(hot.astype(F32), axis=0, keepdims=True), cnt_ref.shape).astype(I32)


def route(x, g, sh, sc, router_w, router_b):
    t, d = x.shape
    tm = min(ROW_TILE, t)
    vec = pl.BlockSpec((1, d), lambda i: (0, 0))
    row = lambda w: pl.BlockSpec((tm, w), lambda i: (i, 0))
    rw = jnp.zeros((d, LANES), BF16).at[:, :N_EXPERTS].set(router_w.astype(BF16))
    rb = jnp.zeros((1, LANES), F32).at[0, :N_EXPERTS].set(router_b)
    h, idx, wt, rank, cnt = pl.pallas_call(
        _router_kernel, grid=(t // tm,),
        in_specs=[row(d), vec, vec, vec, pl.BlockSpec((d, LANES), lambda i: (0, 0)),
                  pl.BlockSpec((1, LANES), lambda i: (0, 0))],
        out_specs=[row(d), row(LANES), row(LANES), row(LANES), pl.BlockSpec((8, LANES), lambda i: (i, 0))],
        out_shape=[jax.ShapeDtypeStruct((t, d), BF16), jax.ShapeDtypeStruct((t, LANES), I32),
                   jax.ShapeDtypeStruct((t, LANES), F32), jax.ShapeDtypeStruct((t, LANES), I32),
                   jax.ShapeDtypeStruct((t // tm * 8, LANES), I32)],
        compiler_params=_params("parallel"), name="route",
    )(x, g.reshape(1, d), sh.reshape(1, d), sc.reshape(1, d), rw, rb)
    return h, idx[:, :2], wt[:, :2], rank[:, :2], cnt.reshape(t // tm, 8, LANES)[:, 0, :N_EXPERTS]


def _gather_kernel(pb_ref, pt_ref, pf_ref, h_ref, dest_ref, wt_ref, o_ref, sw_ref, acc_ref, swacc_ref):
    i = pl.program_id(0)
    flags = pf_ref[i]

    @pl.when((flags & 1) != 0)
    def _():
        acc_ref[...] = jnp.zeros_like(acc_ref)
        swacc_ref[...] = jnp.zeros_like(swacc_ref)

    @pl.when((flags & 4) != 0)
    def _():
        slot = pb_ref[i] * MOE_ROWS + lax.broadcasted_iota(I32, (MOE_ROWS, 1), 0)
        hit0 = dest_ref[0:1, :] == slot
        hit1 = dest_ref[1:2, :] == slot
        acc_ref[...] += jnp.dot((hit0 | hit1).astype(BF16), h_ref[...], preferred_element_type=F32)
        swacc_ref[...] += jnp.sum(jnp.where(hit0, wt_ref[0:1, :], 0.0) + jnp.where(hit1, wt_ref[1:2, :], 0.0),
                                  axis=1, keepdims=True)

    @pl.when((flags & 2) != 0)
    def _():
        o_ref[...] = acc_ref[...].astype(o_ref.dtype)
        sw_ref[...] = swacc_ref[...]


def moe_gather(h, dest_t, wt_t, n_slots, pair_block, pair_tile, pair_flags):
    t, d = h.shape
    tm = min(ROW_TILE, t)
    n_pairs = pair_block.shape[0]
    tok = pl.BlockSpec((2, tm), lambda i, pb, pt, pf: (0, pt[i]))
    return pl.pallas_call(
        _gather_kernel,
        grid_spec=pltpu.PrefetchScalarGridSpec(
            num_scalar_prefetch=3, grid=(n_pairs,),
            in_specs=[pl.BlockSpec((tm, d), lambda i, pb, pt, pf: (pt[i], 0)), tok, tok],
            out_specs=[pl.BlockSpec((MOE_ROWS, d), lambda i, pb, pt, pf: (pb[i], 0)),
                       pl.BlockSpec((MOE_ROWS, 1), lambda i, pb, pt, pf: (pb[i], 0))],
            scratch_shapes=[pltpu.VMEM((MOE_ROWS, d), F32), pltpu.VMEM((MOE_ROWS, 1), F32)]),
        out_shape=[jax.ShapeDtypeStruct((n_slots, d), BF16), jax.ShapeDtypeStruct((n_slots, 1), F32)],
        compiler_params=_params("arbitrary"), name="moe_gather",
    )(pair_block, pair_tile, pair_flags, h, dest_t, wt_t)


def _combine_kernel(pb_ref, pt_ref, pf_ref, yb_ref, dest_ref, x_ref, gt_ref, ng_ref, o_ref, acc_ref):
    i = pl.program_id(0)
    flags = pf_ref[i]

    @pl.when((flags & 1) != 0)
    def _():
        acc_ref[...] = jnp.zeros_like(acc_ref)

    @pl.when((flags & 4) != 0)
    def _():
        slot = pb_ref[i] * MOE_ROWS + lax.broadcasted_iota(I32, (dest_ref.shape[0], MOE_ROWS), 1)
        hit = (dest_ref[:, 0:1] == slot) | (dest_ref[:, 1:2] == slot)
        acc_ref[...] += jnp.dot(hit.astype(BF16), yb_ref[...], preferred_element_type=F32)

    @pl.when((flags & 2) != 0)
    def _():
        y = x_ref[...] + gt_ref[...] * acc_ref[...]
        o_ref[...] = y * lax.rsqrt(jnp.mean(y * y, axis=-1, keepdims=True) + EPS) * ng_ref[...]


def moe_combine_norm(yb, dest, pair_block, pair_tile, pair_flags, x, gate, norm_g):
    t, d = x.shape
    tm = min(ROW_TILE, t)
    n_pairs = pair_block.shape[0]
    return pl.pallas_call(
        _combine_kernel,
        grid_spec=pltpu.PrefetchScalarGridSpec(
            num_scalar_prefetch=3, grid=(n_pairs,),
            in_specs=[pl.BlockSpec((MOE_ROWS, d), lambda i, pb, pt, pf: (pb[i], 0)),
                      pl.BlockSpec((tm, 2), lambda i, pb, pt, pf: (pt[i], 0)),
                      pl.BlockSpec((tm, d), lambda i, pb, pt, pf: (pt[i], 0)),
                      pl.BlockSpec((1, d), lambda i, pb, pt, pf: (0, 0)),
                      pl.BlockSpec((1, d), lambda i, pb, pt, pf: (0, 0))],
            out_specs=pl.BlockSpec((tm, d), lambda i, pb, pt, pf: (pt[i], 0)),
            scratch_shapes=[pltpu.VMEM((tm, d), F32)]),
        out_shape=jax.ShapeDtypeStruct((t, d), F32),
        compiler_params=_params("arbitrary"), name="moe_combine",
    )(pair_block, pair_tile, pair_flags, yb, dest, x, gate.reshape(1, d), norm_g.reshape(1, d))


def _pair_lists(lo, hi, nonempty, n_pairs):
    cnt = jnp.where(nonempty, hi - lo + 1, 1)
    end = jnp.cumsum(cnt)
    start = end - cnt
    i = jnp.arange(n_pairs, dtype=I32)
    ic = jnp.minimum(i, end[-1] - 1)
    grp = jnp.sum((end[None, :] <= ic[:, None]).astype(I32), axis=1)
    off = ic - start[grp]
    member = jnp.where(nonempty[grp], lo[grp] + off, 0).astype(I32)
    return grp, member, off == 0, off == cnt[grp] - 1, nonempty[grp], i < end[-1]


def _pair_flags(first, last, data, valid):
    flags = jnp.where(first, 1, 0) | jnp.where(last, 2, 0) | jnp.where(data, 4, 0)
    return jnp.where(valid, flags, 0).astype(I32)


def moe_plan(top_idx, rank, tile_cnt, n_tok):
    tm = min(ROW_TILE, n_tok)
    n_tiles = n_tok // tm
    n_blocks = 2 * n_tok // MOE_ROWS + N_EXPERTS
    experts = jnp.arange(N_EXPERTS, dtype=I32)
    tile_off = jnp.cumsum(tile_cnt, axis=0) - tile_cnt
    counts = jnp.sum(tile_cnt, axis=0)
    padded = (counts + MOE_ROWS - 1) // MOE_ROWS * MOE_ROWS
    pad_end = jnp.cumsum(padded)
    base = (pad_end - padded)[None, :] + tile_off
    hot = top_idx[:, :, None] == experts[None, None, :]
    dest = (jnp.sum(jnp.where(hot, jnp.repeat(base, tm, axis=0)[:, None, :], 0), axis=-1) + rank).astype(I32)
    blk = jnp.arange(n_blocks, dtype=I32)
    block_expert = jnp.minimum(jnp.sum((pad_end[None, :] <= (blk * MOE_ROWS)[:, None]).astype(I32), axis=1),
                               N_EXPERTS - 1)
    n_pairs = N_EXPERTS * n_tiles + n_blocks
    c_grp, c_block, first, last, data, valid = _pair_lists(
        (base // MOE_ROWS).reshape(-1), ((base + tile_cnt - 1) // MOE_ROWS).reshape(-1), (tile_cnt > 0).reshape(-1),
        n_pairs)
    c_flags = _pair_flags(first & (c_grp % N_EXPERTS == 0), last & (c_grp % N_EXPERTS == N_EXPERTS - 1), data, valid)
    mine = block_expert[:, None, None] == experts[None, None, :]
    base_b = jnp.sum(jnp.where(mine, base[None], 0), axis=-1)
    cnt_b = jnp.sum(jnp.where(mine, tile_cnt[None], 0), axis=-1)
    sends = (cnt_b > 0) & (base_b + cnt_b > (blk * MOE_ROWS)[:, None]) & (base_b < ((blk + 1) * MOE_ROWS)[:, None])
    tiles = jnp.arange(n_tiles, dtype=I32)[None, :]
    g_block, g_tile, *g_bits = _pair_lists(jnp.min(jnp.where(sends, tiles, n_tiles), axis=1),
                                           jnp.max(jnp.where(sends, tiles, -1), axis=1), jnp.any(sends, axis=1), n_pairs)
    return dict(dest=dest, block_expert=block_expert.astype(I32), n_slots=n_blocks * MOE_ROWS,
                n_used=(pad_end[-1:] // MOE_ROWS).astype(I32),
                gather=(g_block, g_tile, _pair_flags(*g_bits)), combine=(c_block, c_grp // N_EXPERTS, c_flags))


def _mods(mod_all, layer, row, d):
    m = mod_all[layer, row]
    return tuple(m[j * d:(j + 1) * d] for j in range(N_MOD))


def _even_layer(x_lat, x_ctx, mods_lat, mods_ctx, norm1_g, norm2_g, w):
    d = x_lat.shape[1]
    hw = DN_HEADS * DN_DH
    s_dn = jnp.zeros((2, DN_HEADS, DN_DH, DN_DH), F32)
    s_lru = jnp.zeros((2, 8, hw), F32)
    outs = []
    for x, mods in ((x_ctx, mods_ctx), (x_lat, mods_lat)):
        sh1, sc1, gt1, sh2, sc2, gt2 = mods
        p, pg = norm_proj(x, norm1_g, sh1, sc1, w["w_main"], w["w_gate"])
        q, k, v, xc, gx, gxt = even_prep(p, pg, w["conv_w"], w["conv_b"], w["a_log"], w["dt_bias"])
        u, wm, qd, kd, qk, gl = deltanet_prep(q, k, v, gx, gxt)
        o_f, o_b, s_dn = deltanet_scan(u, wm, qd, kd, qk, gl, s_dn)
        h_f, h_b, s_lru = lru_scan(xc, w["lru_wa"], w["lru_wx"], w["lru_ba"], w["lru_bx"], w["lru_lam"], s_lru)
        x = even_finish(o_f, o_b, h_f, h_b, p, w["dn_norm_g"], w["w_out"], x, gt1)
        h2 = norm_mod(x, norm2_g, sh2, sc2)
        tm = _dense_rows(h2.shape[0])
        be = jnp.zeros((h2.shape[0] // tm,), I32)
        every = jnp.full((1,), h2.shape[0] // tm, I32)
        hh = ffn_up(h2, w["ffn_gate"], w["ffn_up"], be, every, tm=tm)
        x = ffn_down_residual(hh, w["ffn_down"], be, every, x, gt2, tm=tm)
        outs.append(x)
    return outs[1], outs[0]


def _odd_layer_last(x_lat, x_ctx, mods_lat, mods_ctx, norm1_g, norm2_g, final_g, w):
    dk, dv = w["dk"], w["dv"]
    states = jnp.zeros((2, GLA_HEADS, dv, dk), F32)
    sh1, sc1 = mods_ctx[0], mods_ctx[1]
    p, pg = norm_proj(x_ctx, norm1_g, sh1, sc1, w["w_main"], w["w_gate"])
    _, _, states = gla_scan(p, pg, w["wg_pad"], w["bg"], states, dk=dk, dv=dv)
    sh1, sc1, gt1, sh2, sc2, gt2 = mods_lat
    p, pg = norm_proj(x_lat, norm1_g, sh1, sc1, w["w_main"], w["w_gate"], raster=True)
    o_f, o_b, _ = gla_scan(p, pg, w["wg_pad"], w["bg"], states, dk=dk, dv=dv)
    x = odd_finish(o_f, o_b, p, w["gla_norm_g"], w["w_out"], x_lat, gt1)
    n_tok = x.shape[0]
    h2, top_idx, top_w, rank, tile_cnt = route(x, norm2_g, sh2, sc2, w["router_w"], w["router_b"])
    plan = moe_plan(top_idx, rank, tile_cnt, n_tok)
    xb, slot_w = moe_gather(h2, plan["dest"].T, top_w.T, plan["n_slots"], *plan["gather"])
    hh = ffn_up(xb, w["exp_gate"], w["exp_up"], plan["block_expert"], plan["n_used"], tm=MOE_ROWS)
    yb = ffn_down_scaled(hh, w["exp_down"], plan["block_expert"], plan["n_used"], slot_w)
    return moe_combine_norm(yb, plan["dest"], *plan["combine"], x, gt2, final_g)


def kernel(x, c, ctx, c_ctx, mod_w, mod_b, norm1_g, norm2_g, ev_w_in, ev_conv_qkv, ev_dn_a_log, ev_dn_dt_bias,
           ev_dn_norm_g, ev_lru_conv_w, ev_lru_conv_b, ev_lru_wa, ev_lru_ba, ev_lru_wx, ev_lru_bx, ev_lru_lambda,
           ev_w_out, ev_ffn_w_gate, ev_ffn_w_up, ev_ffn_w_down, od_w_in, od_gla_wg2, od_gla_bg, od_gla_norm_g,
           od_w_out, od_router_w, od_router_b, od_exp_w_gate, od_exp_w_up, od_exp_w_down, final_norm_g):
    b_, length, d = x.shape
    assert b_ == 1 and mod_w.shape[0] == 2, "this kernel implements the batch-1, depth-2 configuration"
    hw = DN_HEADS * DN_DH
    x_lat, x_ctx = x[0], ctx[0]

    cond8 = jnp.zeros((8, d), F32).at[0].set(c[0]).at[1].set(c_ctx)
    mod_all = adaln_all(cond8, mod_w, mod_b)

    w_in = ev_w_in[0]
    qkv_w, z0, ab0, xr0, gr0 = 3 * hw, 3 * hw, 4 * hw, 4 * hw + 4 * DN_HEADS, 5 * hw + 4 * DN_HEADS
    w_main = jnp.concatenate([w_in[:, :qkv_w], w_in[:, xr0:xr0 + hw], w_in[:, z0:z0 + hw], w_in[:, gr0:gr0 + hw]],
                             axis=1).astype(BF16)
    w_gate = jnp.zeros((d, LANES), BF16).at[:, :4 * DN_HEADS].set(w_in[:, ab0:ab0 + 4 * DN_HEADS].astype(BF16))
    pad16 = lambda t: jnp.zeros((1, LANES), F32).at[0, :2 * DN_HEADS].set(t.reshape(-1))
    ev = dict(
        w_main=w_main, w_gate=w_gate,
        conv_w=jnp.concatenate([ev_conv_qkv[0], ev_lru_conv_w[0]], axis=1),
        conv_b=jnp.concatenate([jnp.zeros((qkv_w,), F32), ev_lru_conv_b[0]]).reshape(1, -1),
        a_log=pad16(ev_dn_a_log[0]), dt_bias=pad16(ev_dn_dt_bias[0]),
        dn_norm_g=ev_dn_norm_g[0],
        lru_wa=ev_lru_wa[0].astype(BF16), lru_wx=ev_lru_wx[0].astype(BF16),
        lru_ba=ev_lru_ba[0].reshape(2, 1, hw), lru_bx=ev_lru_bx[0].reshape(2, 1, hw),
        lru_lam=ev_lru_lambda[0].reshape(2, 1, hw),
        w_out=ev_w_out[0].astype(BF16),
        ffn_gate=ev_ffn_w_gate.astype(BF16), ffn_up=ev_ffn_w_up.astype(BF16), ffn_down=ev_ffn_w_down.astype(BF16),
    )
    x_lat, x_ctx = _even_layer(x_lat, x_ctx, _mods(mod_all, 0, 0, d), _mods(mod_all, 0, 1, d),
                               norm1_g[0], norm2_g[0], ev)

    w_in = od_w_in[0]
    qk_w = od_gla_wg2.shape[-1]
    v_w = od_w_out.shape[1]
    main_w = 2 * qk_w + 2 * v_w
    wg_pad = jnp.zeros((2, LANES, qk_w), F32)
    for dirn in range(2):
        wg_pad = wg_pad.at[dirn, dirn * GLA_RANK:(dirn + 1) * GLA_RANK].set(od_gla_wg2[0, dirn])
    od = dict(
        w_main=w_in[:, :main_w].astype(BF16),
        w_gate=jnp.zeros((d, LANES), BF16).at[:, :2 * GLA_RANK].set(w_in[:, main_w:].astype(BF16)),
        wg_pad=wg_pad, bg=od_gla_bg[0].reshape(2, 1, qk_w), gla_norm_g=od_gla_norm_g[0],
        w_out=od_w_out[0].astype(BF16), router_w=od_router_w[0], router_b=od_router_b[0],
        exp_gate=od_exp_w_gate[0].astype(BF16), exp_up=od_exp_w_up[0].astype(BF16),
        exp_down=od_exp_w_down[0].astype(BF16),
        dk=qk_w // GLA_HEADS, dv=v_w // GLA_HEADS,
    )
    return _odd_layer_last(x_lat, x_ctx, _mods(mod_all, 1, 0, d), _mods(mod_all, 1, 1, d),
                           norm1_g[1], norm2_g[1], final_norm_g, od)[None]
```

```python
import functools
import math

import jax
import jax.numpy as jnp
from jax import lax
from jax.experimental import pallas as pl
from jax.experimental.pallas import tpu as pltpu

F32 = jnp.float32
BF16 = jnp.bfloat16
I32 = jnp.int32
HI = lax.Precision.HIGHEST

EPS = 1e-6
N_MOD = 6
GRID_W = 64
CHUNK = 64
SUB = 8
LANES = 128
ROW_TILE = 256
DN_HEADS = 8
DN_DH = 128
LRU_BLOCKS = 8
LRU_BW = 128
LRU_C = 8.0
LRU_UNROLL = 4
GLA_HEADS = 4
GLA_RANK = 16
GLA_TAU = 16.0
GLA_SAFE_STEP = 7.5
N_EXPERTS = 8
MOE_ROWS = 256
VMEM_LIMIT = 56 * 1024 * 1024

NT_DIMS = (((1,), (1,)), ((), ()))
TN_DIMS = (((0,), (0,)), ((), ()))


def _params(*sem):
    return pltpu.CompilerParams(dimension_semantics=sem, vmem_limit_bytes=VMEM_LIMIT)


def _softplus(x):
    return jnp.maximum(x, 0.0) + jnp.log1p(jnp.exp(-jnp.abs(x)))


def _silu(x):
    return x * jax.nn.sigmoid(x)


def _bdot(a, b):
    return jnp.dot(a.astype(BF16), b.astype(BF16), preferred_element_type=F32)


def _adaln_kernel(cond_ref, w_ref, b_ref, o_ref):
    s = _silu(cond_ref[...])
    o_ref[...] = jnp.dot(s, w_ref[...], precision=HI, preferred_element_type=F32) + b_ref[...]


def adaln_all(cond8, mod_w, mod_b):
    n_layers, d, n6 = mod_w.shape
    tn = 1024
    return pl.pallas_call(
        _adaln_kernel,
        grid=(n_layers, n6 // tn),
        in_specs=[pl.BlockSpec((8, d), lambda l, j: (0, 0)),
                  pl.BlockSpec((None, d, tn), lambda l, j: (l, 0, j)),
                  pl.BlockSpec((None, 1, tn), lambda l, j: (l, 0, j))],
        out_specs=pl.BlockSpec((None, 8, tn), lambda l, j: (l, 0, j)),
        out_shape=jax.ShapeDtypeStruct((n_layers, 8, n6), F32),
        compiler_params=_params("arbitrary", "arbitrary"),
        name="adaln",
    )(cond8, mod_w, mod_b.reshape(n_layers, 1, n6))


def _norm_mod_kernel(x_ref, g_ref, sh_ref, sc_ref, o_ref):
    x = x_ref[...]
    y = x * lax.rsqrt(jnp.mean(x * x, axis=-1, keepdims=True) + EPS)
    o_ref[...] = ((y * g_ref[...]) * (1.0 + sc_ref[...]) + sh_ref[...]).astype(o_ref.dtype)


def _raster_spec(rows, d):
    return pl.BlockSpec((rows, d), lambda c: (0, c))


def norm_mod(x, g, sh, sc):
    n, d = x.shape
    tm = min(ROW_TILE, n)
    vec = pl.BlockSpec((1, d), lambda i: (0, 0))
    row = pl.BlockSpec((tm, d), lambda i: (i, 0))
    return pl.pallas_call(
        _norm_mod_kernel, grid=(n // tm,), in_specs=[row, vec, vec, vec], out_specs=row,
        out_shape=jax.ShapeDtypeStruct((n, d), BF16),
        compiler_params=_params("parallel"), name="norm_mod",
    )(x, g.reshape(1, d), sh.reshape(1, d), sc.reshape(1, d))


def _dense_rows(r):
    return 2 * ROW_TILE if r % (2 * ROW_TILE) == 0 else min(ROW_TILE, r)


def _norm_proj_kernel(x_ref, g_ref, sh_ref, sc_ref, w_ref, wg_ref, p_ref, pg_ref, h_s, *, d):
    @pl.when(pl.program_id(1) == 0)
    def _():
        rows = x_ref.shape[0]
        for col in range(x_ref.shape[1] // d):
            x = x_ref[:, col * d:(col + 1) * d]
            y = x * lax.rsqrt(jnp.mean(x * x, axis=-1, keepdims=True) + EPS)
            h_s[col * rows:(col + 1) * rows, :] = ((y * g_ref[...]) * (1.0 + sc_ref[...]) + sh_ref[...]).astype(BF16)
        pg_ref[...] = jnp.dot(h_s[...], wg_ref[...], preferred_element_type=F32)

    p_ref[...] = jnp.dot(h_s[...], w_ref[...], preferred_element_type=F32)


def norm_proj(x, g, sh, sc, w_main, w_gate, *, raster=False, tn=1024):
    n, d = x.shape
    n_main = w_main.shape[1]
    tm = next(t for t in (4 * ROW_TILE, 2 * ROW_TILE, min(ROW_TILE, n)) if n % t == 0)
    if raster:
        rows = n // GRID_W
        n_col = max(tm // rows, 1)
        tm = n_col * rows
        x_in, x_spec = x.reshape(rows, GRID_W * d), pl.BlockSpec((rows, n_col * d), lambda i, j: (0, i))
    else:
        x_in, x_spec = x, pl.BlockSpec((tm, d), lambda i, j: (i, 0))
    vec = pl.BlockSpec((1, d), lambda i, j: (0, 0))
    return pl.pallas_call(
        functools.partial(_norm_proj_kernel, d=d), grid=(n // tm, n_main // tn),
        in_specs=[x_spec, vec, vec, vec, pl.BlockSpec((d, tn), lambda i, j: (0, j)),
                  pl.BlockSpec((d, LANES), lambda i, j: (0, 0))],
        out_specs=[pl.BlockSpec((tm, tn), lambda i, j: (i, j)), pl.BlockSpec((tm, LANES), lambda i, j: (i, 0))],
        out_shape=[jax.ShapeDtypeStruct((n, n_main), F32), jax.ShapeDtypeStruct((n, LANES), F32)],
        scratch_shapes=[pltpu.VMEM((tm, d), BF16)],
        compiler_params=_params("parallel", "arbitrary"), name="norm_proj",
    )(x_in, g.reshape(1, d), sh.reshape(1, d), sc.reshape(1, d), w_main, w_gate)


def _evprep_kernel(p_ref, prev_ref, next_ref, pg_ref, cw_ref, cb_ref, alog_ref, dtb_ref,
                   q_ref, k_ref, v_ref, xc_ref, gx_ref, gxt_ref, ext_ref, *, n_tiles):
    i = pl.program_id(0)
    tm, width = p_ref.shape
    ext_ref[8:8 + tm, :] = p_ref[...]
    ext_ref[0:8, :] = jnp.where(i > 0, prev_ref[...], 0.0)
    ext_ref[8 + tm:16 + tm, :] = jnp.where(i < n_tiles - 1, next_ref[...], 0.0)
    acc = ext_ref[pl.ds(6, tm), :] * cw_ref[0:1, :]
    for j in range(1, 4):
        acc = acc + ext_ref[pl.ds(6 + j, tm), :] * cw_ref[j:j + 1, :]
    acc = acc + cb_ref[...]
    qk_w = DN_HEADS * DN_DH
    for hd in range(DN_HEADS):
        for part, ref, scale in ((0, q_ref, DN_DH ** -0.5), (1, k_ref, 1.0)):
            lo = part * qk_w + hd * DN_DH
            t = _silu(acc[:, lo:lo + DN_DH])
            t = t * lax.rsqrt(jnp.sum(t * t, axis=-1, keepdims=True) + EPS)
            ref[:, hd * DN_DH:(hd + 1) * DN_DH] = t * scale
    v_ref[...] = _silu(acc[:, 2 * qk_w:3 * qk_w])
    xc_ref[...] = acc[:, 3 * qk_w:]
    pg = pg_ref[...]
    lane = lax.broadcasted_iota(I32, pg.shape, 1)
    g = -jnp.exp(alog_ref[...]) * _softplus(pg + dtb_ref[...])
    ri = lax.broadcasted_iota(I32, (tm, tm), 0)
    ci = lax.broadcasted_iota(I32, (tm, tm), 1)
    same = (ri // CHUNK) == (ci // CHUNK)
    cum_f = _dot01((same & (ci <= ri)).astype(BF16), g)
    cum_b = _dot01((same & (ci >= ri)).astype(BF16), g)
    tot = _dot01(same.astype(BF16), g)
    gc = jnp.where(lane < DN_HEADS, cum_f, cum_b)
    gx = jnp.where(lane < 2 * DN_HEADS, gc,
                   jnp.where(lane < 4 * DN_HEADS, jax.nn.sigmoid(pg), pltpu.roll(tot, 4 * DN_HEADS, 1)))
    gx_ref[...] = gx
    gxt_ref[...] = gx.T


def _dot01(m01, x):
    x1 = x.astype(BF16)
    r1 = x - x1.astype(F32)
    x2 = r1.astype(BF16)
    x3 = (r1 - x2.astype(F32)).astype(BF16)
    dot = lambda t: jnp.dot(m01, t, preferred_element_type=F32)
    return dot(x1) + dot(x2) + dot(x3)


def even_prep(p, pg, conv_w, conv_b, a_log, dt_bias):
    r = p.shape[0]
    width = conv_w.shape[1]
    tm = min(ROW_TILE, r)
    n_tiles = r // tm
    hb = tm // 8
    out_w = DN_HEADS * DN_DH
    row = lambda w: pl.BlockSpec((tm, w), lambda i: (i, 0))
    vec = lambda w: pl.BlockSpec((1, w), lambda i: (0, 0))
    return pl.pallas_call(
        functools.partial(_evprep_kernel, n_tiles=n_tiles),
        grid=(n_tiles,),
        in_specs=[row(width),
                  pl.BlockSpec((8, width), lambda i: (jnp.maximum(i * hb - 1, 0), 0)),
                  pl.BlockSpec((8, width), lambda i: (jnp.minimum((i + 1) * hb, r // 8 - 1), 0)),
                  row(LANES),
                  pl.BlockSpec((4, width), lambda i: (0, 0)), vec(width), vec(LANES), vec(LANES)],
        out_specs=[row(out_w), row(out_w), row(out_w), row(out_w), row(LANES),
                   pl.BlockSpec((LANES, tm), lambda i: (0, i))],
        out_shape=[jax.ShapeDtypeStruct((r, out_w), F32)] * 4
        + [jax.ShapeDtypeStruct((r, LANES), F32), jax.ShapeDtypeStruct((LANES, r), F32)],
        scratch_shapes=[pltpu.VMEM((tm + 16, width), F32)],
        compiler_params=_params("parallel"), name="even_prep",
    )(p, p, p, pg, conv_w, conv_b, a_log, dt_bias)


DN_BASE = 16
DN_HEADS_PER_STEP = 4


def _dnprep_kernel(q_ref, k_ref, v_ref, gx_ref, gxt_ref, u_ref, w_ref, qd_ref, kd_ref, qk_ref, gl_ref, *, n_chunks):
    c = CHUNK
    tm = q_ref.shape[0]
    ri = lax.broadcasted_iota(I32, (tm, tm), 0)
    ci = lax.broadcasted_iota(I32, (tm, tm), 1)
    same = lambda s: (ri // s) == (ci // s)
    eye = (ri == ci).astype(F32)
    lane = lax.broadcasted_iota(I32, (tm, LANES), 1)
    gx = gx_ref[...]
    pick = lambda idx: jnp.sum(jnp.where(lane == idx, gx, 0.0), axis=1, keepdims=True)
    chains = []
    for hh in range(DN_HEADS_PER_STEP):
        hd = pl.program_id(1) * DN_HEADS_PER_STEP + hh
        cols = slice(hh * DN_DH, (hh + 1) * DN_DH)
        q = q_ref[:, cols]
        k = k_ref[:, cols]
        kb16 = k.astype(BF16)
        gram_k = lax.dot_general(kb16, kb16, NT_DIMS, preferred_element_type=F32)
        gram_q = lax.dot_general(q.astype(BF16), kb16, NT_DIMS, preferred_element_type=F32)
        for d in range(2):
            incl = same(c) & ((ci >= ri) if d else (ci <= ri))
            gcol = pick(d * DN_HEADS + hd)
            bcol = pick((2 + d) * DN_HEADS + hd)
            tcol = pick((4 + d) * DN_HEADS + hd)
            grow = gxt_ref[pl.ds(d * DN_HEADS + hd, 1), :]
            decay = jnp.where(incl, jnp.exp(gcol - grow), 0.0)
            a = jnp.where(ri == ci, 0.0, gram_k * bcol * decay)
            eg = jnp.exp(gcol)
            qk = (gram_q * decay).astype(BF16)
            for n in range(n_chunks):
                qk_ref[d, hh, n * c:(n + 1) * c, :] = qk[n * c:(n + 1) * c, n * c:(n + 1) * c]
                gl_ref[d, hh, n] = jnp.broadcast_to(jnp.exp(tcol[n * c:n * c + 1, :]), (8, LANES))
            qd_ref[d, :, cols] = (q * eg).astype(BF16)
            kd_ref[d, :, cols] = (k * jnp.exp(tcol - gcol)).astype(BF16)
            diag = jnp.where(same(DN_BASE), a, 0.0)
            chains.append(dict(d=d, cols=cols, a=a, t=eye - diag, p=diag, scale=bcol, scale_k=bcol * eg))
    size = 2
    while size < DN_BASE:
        for ch in chains:
            ch["p"] = _bdot(ch["p"], ch["p"])
        for ch in chains:
            ch["t"] = ch["t"] + _bdot(ch["t"], ch["p"])
        size *= 2
    size = DN_BASE
    while size < c:
        couple = same(2 * size) & ~same(size)
        for ch in chains:
            ch["et"] = _bdot(jnp.where(couple, ch["a"], 0.0), ch["t"])
        for ch in chains:
            ch["t"] = ch["t"] - _bdot(ch["t"], ch["et"])
        size *= 2
    for ch in chains:
        d, cols = ch["d"], ch["cols"]
        x = _bdot(ch["t"], jnp.concatenate([v_ref[:, cols] * ch["scale"], k_ref[:, cols] * ch["scale_k"]], axis=1))
        u_ref[d, :, cols] = x[:, :DN_DH]
        w_ref[d, :, cols] = x[:, DN_DH:].astype(BF16)


def deltanet_prep(q, k, v, gx, gxt):
    r = q.shape[0]
    tm = min(ROW_TILE, r)
    n_chunks = tm // CHUNK
    hw = DN_HEADS * DN_DH
    hps = DN_HEADS_PER_STEP
    head = pl.BlockSpec((tm, hps * DN_DH), lambda i, h: (i, h))
    dhead = pl.BlockSpec((2, tm, hps * DN_DH), lambda i, h: (0, i, h))
    return pl.pallas_call(
        functools.partial(_dnprep_kernel, n_chunks=n_chunks),
        grid=(r // tm, DN_HEADS // hps),
        in_specs=[head, head, head, pl.BlockSpec((tm, LANES), lambda i, h: (i, 0)),
                  pl.BlockSpec((LANES, tm), lambda i, h: (0, i))],
        out_specs=[dhead, dhead, dhead, dhead,
                   pl.BlockSpec((2, hps, tm, CHUNK), lambda i, h: (0, h, i, 0)),
                   pl.BlockSpec((2, hps, n_chunks, 8, LANES), lambda i, h: (0, h, i, 0, 0))],
        out_shape=[jax.ShapeDtypeStruct((2, r, hw), F32),
                   jax.ShapeDtypeStruct((2, r, hw), BF16),
                   jax.ShapeDtypeStruct((2, r, hw), BF16),
                   jax.ShapeDtypeStruct((2, r, hw), BF16),
                   jax.ShapeDtypeStruct((2, DN_HEADS, r, CHUNK), BF16),
                   jax.ShapeDtypeStruct((2, DN_HEADS, r // CHUNK, 8, LANES), F32)],
        compiler_params=_params("parallel", "parallel"), name="deltanet_prep",
    )(q, k, v, gx, gxt)


def _dnscan_kernel(*refs, n_steps):
    ins = (refs[0:6], refs[6:12])
    s0_ref, o_refs, sf_ref, s_ref = refs[12], refs[13:15], refs[15], refs[16]
    step = pl.program_id(0)

    @pl.when(step == 0)
    def _():
        s_ref[...] = s0_ref[...]

    c = CHUNK
    n_sub = ins[0][0].shape[0] // c
    col = lambda hd: slice(hd * DN_DH, (hd + 1) * DN_DH)
    dot = lambda a, b: jnp.dot(a, b, preferred_element_type=F32)
    for t in range(n_sub):
        chains = []
        for d in range(2):
            u_ref, w_ref, qd_ref, kd_ref, qk_ref, gl_ref = ins[d]
            n = n_sub - 1 - t if d else t
            rows = slice(n * c, (n + 1) * c)
            for hd in range(DN_HEADS):
                chains.append(dict(d=d, hd=hd, rows=rows, u=u_ref.at[rows, col(hd)], w=w_ref.at[rows, col(hd)],
                                   qd=qd_ref.at[rows, col(hd)], kd=kd_ref.at[rows, col(hd)],
                                   qk=qk_ref.at[hd, rows, :], gl=gl_ref.at[hd, n, 0:1, :]))
        for ch in chains:
            ch["s"] = s_ref[ch["d"], ch["hd"]]
            ch["sb"] = ch["s"].astype(BF16)
        for ch in chains:
            ch["ws"] = dot(ch["w"][...], ch["sb"])
        for ch in chains:
            ch["qs"] = dot(ch["qd"][...], ch["sb"])
        for ch in chains:
            ch["vb"] = (ch["u"][...] - ch["ws"]).astype(BF16)
        for ch in chains:
            o_refs[ch["d"]][ch["rows"], col(ch["hd"])] = ch["qs"] + dot(ch["qk"][...], ch["vb"])
        for ch in chains:
            ch["ds"] = lax.dot_general(ch["kd"][...], ch["vb"], TN_DIMS, preferred_element_type=F32)
        for ch in chains:
            s_ref[ch["d"], ch["hd"]] = ch["s"] * ch["gl"][...] + ch["ds"]

    @pl.when(step == n_steps - 1)
    def _():
        sf_ref[...] = s_ref[...]


def deltanet_scan(u, w, qd, kd, qk, gl, s0):
    r = u.shape[1]
    hw = DN_HEADS * DN_DH
    tm = min(ROW_TILE, r)
    n_steps = r // tm
    state = pl.BlockSpec((2, DN_HEADS, DN_DH, DN_DH), lambda i: (0, 0, 0, 0))
    in_specs, out_specs = [], []
    for d in range(2):
        idx = (lambda i: n_steps - 1 - i) if d else (lambda i: i)
        big = pl.BlockSpec((None, tm, hw), lambda i, d=d, idx=idx: (d, idx(i), 0))
        in_specs += [big, big, big, big,
                     pl.BlockSpec((None, DN_HEADS, tm, CHUNK), lambda i, d=d, idx=idx: (d, 0, idx(i), 0)),
                     pl.BlockSpec((None, DN_HEADS, tm // CHUNK, 8, LANES), lambda i, d=d, idx=idx: (d, 0, idx(i), 0, 0))]
        out_specs.append(pl.BlockSpec((tm, hw), lambda i, idx=idx: (idx(i), 0)))
    return pl.pallas_call(
        functools.partial(_dnscan_kernel, n_steps=n_steps),
        grid=(n_steps,),
        in_specs=in_specs + [state],
        out_specs=out_specs + [state],
        out_shape=[jax.ShapeDtypeStruct((r, hw), F32), jax.ShapeDtypeStruct((r, hw), F32),
                   jax.ShapeDtypeStruct((2, DN_HEADS, DN_DH, DN_DH), F32)],
        scratch_shapes=[pltpu.VMEM((2, DN_HEADS, DN_DH, DN_DH), F32)],
        compiler_params=_params("arbitrary"), name="deltanet_scan",
    )(u, w, qd, kd, qk, gl, u, w, qd, kd, qk, gl, s0)


def _lru_kernel(xf_ref, xb_ref, wa_ref, wx_ref, ba_ref, bx_ref, lam_ref, h0_ref, hf_ref, hb_ref, hl_ref,
                a_s, b_s, carry_s, *, n_steps):
    step = pl.program_id(0)

    @pl.when(step == 0)
    def _():
        carry_s[...] = h0_ref[...]

    tm = xf_ref.shape[0]
    for d, xc_ref in enumerate((xf_ref, xb_ref)):
        sp = _softplus(-lam_ref[d])
        for n in range(LRU_BLOCKS):
            cols = slice(n * LRU_BW, (n + 1) * LRU_BW)
            xb = xc_ref[:, cols]
            xbb = xb.astype(BF16)
            r = jax.nn.sigmoid(jnp.dot(xbb, wa_ref[d, n], preferred_element_type=F32) + ba_ref[d, :, cols])
            gi = jax.nn.sigmoid(jnp.dot(xbb, wx_ref[d, n], preferred_element_type=F32) + bx_ref[d, :, cols])
            log_a = -LRU_C * r * sp[:, cols]
            a = jnp.exp(log_a)
            a_s[d, :, cols] = a
            b_s[d, :, cols] = jnp.sqrt(-jnp.tanh(log_a) * (a * a + 1.0)) * (gi * xb)

    rid = lax.broadcasted_iota(I32, (8, a_s.shape[2]), 0)
    n_groups = tm // 8
    h_refs = (hf_ref, hb_ref)

    def group(gidx, carries):
        out = []
        for d in range(2):
            g = (n_groups - 1 - gidx) if d else gidx
            base = pl.multiple_of(g * 8, 8)
            a = a_s[d, pl.ds(base, 8), :]
            b = b_s[d, pl.ds(base, 8), :]
            for sh in (1, 2, 4):
                if d:
                    keep = rid < 8 - sh
                    a_n = jnp.where(keep, pltpu.roll(a, 8 - sh, 0), 1.0)
                    b_n = jnp.where(keep, pltpu.roll(b, 8 - sh, 0), 0.0)
                else:
                    keep = rid >= sh
                    a_n = jnp.where(keep, pltpu.roll(a, sh, 0), 1.0)
                    b_n = jnp.where(keep, pltpu.roll(b, sh, 0), 0.0)
                b = a * b_n + b
                a = a * a_n
            h = a * carries[d] + b
            h_refs[d][pl.ds(base, 8), :] = h
            edge = h[0:1, :] if d else h[7:8, :]
            out.append(jnp.broadcast_to(edge, h.shape))
        return tuple(out)

    carries = lax.fori_loop(0, n_groups, group, (carry_s[0], carry_s[1]), unroll=LRU_UNROLL)
    carry_s[0] = carries[0]
    carry_s[1] = carries[1]

    @pl.when(step == n_steps - 1)
    def _():
        hl_ref[...] = carry_s[...]


def lru_scan(xc, wa, wx, ba, bx, lam, h0):
    r, width = xc.shape
    tm = min(ROW_TILE, r)
    n_steps = r // tm
    vec = pl.BlockSpec((2, 1, width), lambda i: (0, 0, 0))
    wspec = pl.BlockSpec((2, LRU_BLOCKS, LRU_BW, LRU_BW), lambda i: (0, 0, 0, 0))
    st = pl.BlockSpec((2, 8, width), lambda i: (0, 0, 0))
    fwd = pl.BlockSpec((tm, width), lambda i: (i, 0))
    bwd = pl.BlockSpec((tm, width), lambda i: (n_steps - 1 - i, 0))
    return pl.pallas_call(
        functools.partial(_lru_kernel, n_steps=n_steps),
        grid=(n_steps,),
        in_specs=[fwd, bwd, wspec, wspec, vec, vec, vec, st],
        out_specs=[fwd, bwd, st],
        out_shape=[jax.ShapeDtypeStruct((r, width), F32), jax.ShapeDtypeStruct((r, width), F32),
                   jax.ShapeDtypeStruct((2, 8, width), F32)],
        scratch_shapes=[pltpu.VMEM((2, tm, width), F32), pltpu.VMEM((2, tm, width), F32),
                        pltpu.VMEM((2, 8, width), F32)],
        compiler_params=_params("arbitrary"), name="lru_scan",
    )(xc, xc, wa, wx, ba, bx, lam, h0)


def _gelu_tanh(x):
    return 0.5 * x * (1.0 + jnp.tanh(math.sqrt(2.0 / math.pi) * (x + 0.044715 * (x * x * x))))


def _evfin_kernel(of_ref, ob_ref, hf_ref, hb_ref, zg_ref, ng_ref, wout_ref, x_ref, gt_ref, o_ref, mix_s):
    hw = DN_HEADS * DN_DH
    for hd in range(DN_HEADS):
        cols = slice(hd * DN_DH, (hd + 1) * DN_DH)
        o = of_ref[:, cols] + ob_ref[:, cols]
        y = o * lax.rsqrt(jnp.mean(o * o, axis=-1, keepdims=True) + EPS) * ng_ref[...]
        mix_s[:, cols] = (y * _silu(zg_ref[:, cols])).astype(BF16)
    mix_s[:, hw:] = ((hf_ref[...] + hb_ref[...]) * _gelu_tanh(zg_ref[:, hw:])).astype(BF16)
    y = jnp.dot(mix_s[...], wout_ref[...], preferred_element_type=F32)
    o_ref[...] = x_ref[...] + gt_ref[...] * y


def even_finish(o_f, o_b, h_f, h_b, p, norm_g, w_out, x, gate):
    r, d = x.shape
    tm = min(ROW_TILE, r)
    hw = DN_HEADS * DN_DH
    row = lambda w: pl.BlockSpec((tm, w), lambda i: (i, 0))
    return pl.pallas_call(
        _evfin_kernel, grid=(r // tm,),
        in_specs=[row(hw), row(hw), row(hw), row(hw),
                  pl.BlockSpec((tm, 2 * hw), lambda i: (i, 2)),
                  pl.BlockSpec((1, DN_DH), lambda i: (0, 0)),
                  pl.BlockSpec(w_out.shape, lambda i: (0, 0)),
                  row(d), pl.BlockSpec((1, d), lambda i: (0, 0))],
        out_specs=row(d),
        out_shape=jax.ShapeDtypeStruct((r, d), F32),
        scratch_shapes=[pltpu.VMEM((tm, 2 * hw), BF16)],
        compiler_params=_params("parallel"), name="even_finish",
    )(o_f, o_b, h_f, h_b, p, norm_g.reshape(1, DN_DH), w_out, x, gate.reshape(1, d))


def _odfin_kernel(of_ref, ob_ref, go_ref, ng_ref, wout_ref, x_ref, gt_ref, o_ref, mix_s, *, dv):
    for hd in range(GLA_HEADS):
        cols = slice(hd * dv, (hd + 1) * dv)
        o = of_ref[:, cols] + ob_ref[:, cols]
        y = o * lax.rsqrt(jnp.mean(o * o, axis=-1, keepdims=True) + EPS) * ng_ref[...]
        mix_s[:, cols] = (y * _silu(go_ref[:, cols])).astype(BF16)
    y = jnp.dot(mix_s[...], wout_ref[...], preferred_element_type=F32)
    o_ref[...] = x_ref[...] + gt_ref[...] * y


def odd_finish(o_f, o_b, p, norm_g, w_out, x, gate):
    r, d = x.shape
    rows = r // GRID_W
    vw = o_f.shape[1]
    dv = vw // GLA_HEADS
    row = lambda w: pl.BlockSpec((rows, w), lambda c: (c, 0))
    return pl.pallas_call(
        functools.partial(_odfin_kernel, dv=dv), grid=(GRID_W,),
        in_specs=[row(vw), row(vw),
                  pl.BlockSpec((rows, vw), lambda c: (c, 2)),
                  pl.BlockSpec((1, dv), lambda c: (0, 0)),
                  pl.BlockSpec(w_out.shape, lambda c: (0, 0)),
                  _raster_spec(rows, d), pl.BlockSpec((1, d), lambda c: (0, 0))],
        out_specs=_raster_spec(rows, d),
        out_shape=jax.ShapeDtypeStruct((rows, GRID_W * d), F32),
        scratch_shapes=[pltpu.VMEM((rows, vw), BF16)],
        compiler_params=_params("parallel"), name="odd_finish",
    )(o_f, o_b, p, norm_g.reshape(1, dv), w_out, x.reshape(rows, GRID_W * d), gate.reshape(1, d)).reshape(r, d)


def _gla_kernel(*refs, n_steps, dk):
    ins = (refs[0:6], refs[6:12])
    s0_ref, o_refs, sf_ref, s_ref, gc_s = refs[12], refs[13:15], refs[15], refs[16], refs[17]
    step = pl.program_id(1)

    @pl.when(step == 0)
    def _():
        s_ref[...] = s0_ref[...]

    c = CHUNK
    tm = ins[0][0].shape[0]
    n_chunks = tm // c
    n_sub = c // SUB
    ri = lax.broadcasted_iota(I32, (tm, tm), 0)
    ci = lax.broadcasted_iota(I32, (tm, tm), 1)
    r64 = lax.broadcasted_iota(I32, (c, c), 0)
    c64 = lax.broadcasted_iota(I32, (c, c), 1)
    rr = lax.broadcasted_iota(I32, (c, 1), 0) % SUB
    units = []
    g_min = None
    for d in range(2):
        q_ref, k_ref, v_ref, gd_ref, wg_ref, bg_ref = ins[d]
        tri = (((ri // c) == (ci // c)) & ((ci >= ri) if d else (ci <= ri))).astype(BF16)
        logit = _bdot(gd_ref[...], wg_ref[...]) + bg_ref[...]
        g = -_softplus(-logit) * (1.0 / GLA_TAU)
        gc = _dot01(tri, g)
        gc_s[d] = gc
        g_min = jnp.min(g) if g_min is None else jnp.minimum(g_min, jnp.min(g))
        q = q_ref[...] * dk ** -0.5
        for n in range(n_chunks):
            rows = slice(n * c, (n + 1) * c)
            units.append(dict(d=d, n=n, rows=rows, q=q[rows], k=k_ref[rows, :], g=g[rows], gc=gc[rows],
                              att=jnp.zeros((c, c), F32)))
    size = c // 2
    while size >= SUB:
        r_hi, c_hi = (r64 & size) != 0, (c64 & size) != 0
        same = (r64 // (2 * size)) == (c64 // (2 * size))
        pair = (same & r_hi & ~c_hi, same & ~r_hi & c_hi)
        for u in units:
            gc = u["gc"]
            pieces = []
            for b in range(c // (2 * size)):
                mid = b * 2 * size + size
                edge = gc[mid:mid + 1] if u["d"] else gc[mid - 1:mid]
                pieces.append(jnp.broadcast_to(edge, (2 * size, dk)))
            edge = pieces[0] if len(pieces) == 1 else jnp.concatenate(pieces, axis=0)
            qs = (u["q"] * jnp.exp(gc - edge)).astype(BF16)
            ks = (u["k"] * jnp.exp(edge - gc)).astype(BF16)
            u["att"] = u["att"] + jnp.where(pair[u["d"]],
                                            lax.dot_general(qs, ks, NT_DIMS, preferred_element_type=F32), 0.0)
        size //= 2

    def diag_direct():
        blocks = [jnp.zeros((c, c), F32) for _ in units]
        for jj in range(SUB):
            here = c64 == (r64 // SUB) * SUB + jj
            seen = (here & (rr >= jj), here & (rr <= jj))
            for i, u in enumerate(units):
                rep = lambda ref: jnp.concatenate(
                    [jnp.broadcast_to(ref[pl.ds(u["n"] * c + b * SUB + jj, 1), :], (SUB, dk)) for b in range(n_sub)],
                    axis=0)
                col = jnp.sum(u["q"] * rep(ins[u["d"]][1]) * jnp.exp(u["gc"] - rep(gc_s.at[u["d"]])),
                              axis=-1, keepdims=True)
                blocks[i] = jnp.where(seen[u["d"]], col, blocks[i])
        return tuple(blocks)

    def diag_factored():
        blocks = []
        same = (r64 // SUB) == (c64 // SUB)
        inside = (same & (c64 <= r64), same & (c64 >= r64))
        for u in units:
            before = (u["gc"] - u["g"]).reshape(n_sub, SUB, dk)
            edge = before[:, SUB - 1:SUB, :] if u["d"] else before[:, 0:1, :]
            edge = jnp.broadcast_to(edge, before.shape).reshape(c, dk)
            qs = (u["q"] * jnp.exp(u["gc"] - edge)).astype(BF16)
            ks = (u["k"] * jnp.exp(edge - u["gc"])).astype(BF16)
            blocks.append(jnp.where(inside[u["d"]],
                                    lax.dot_general(qs, ks, NT_DIMS, preferred_element_type=F32), 0.0))
        return tuple(blocks)

    diag = lax.cond(g_min >= -GLA_SAFE_STEP, diag_factored, diag_direct)
    for u in units:
        u["vb"] = ins[u["d"]][2][u["rows"], :].astype(BF16)
        u["tot"] = u["gc"][0:1] if u["d"] else u["gc"][c - 1:c]
    for i, u in enumerate(units):
        u["o"] = jnp.dot((u["att"] + diag[i]).astype(BF16), u["vb"], preferred_element_type=F32)
    for u in units:
        u["qd"] = (u["q"] * jnp.exp(u["gc"])).astype(BF16)
        u["ds"] = lax.dot_general(u["vb"], (u["k"] * jnp.exp(u["tot"] - u["gc"])).astype(BF16), TN_DIMS,
                                  preferred_element_type=F32)
    for t in range(n_chunks):
        for d in range(2):
            u = units[d * n_chunks + (n_chunks - 1 - t if d else t)]
            s = s_ref[d]
            o_refs[d][u["rows"], :] = u["o"] + lax.dot_general(u["qd"], s.astype(BF16), NT_DIMS,
                                                               preferred_element_type=F32)
            s_ref[d] = s * jnp.exp(u["tot"]) + u["ds"]

    @pl.when(step == n_steps - 1)
    def _():
        sf_ref[...] = s_ref[...]


def gla_scan(p, pg, wg_pad, bg, s0, *, dk, dv):
    r = p.shape[0]
    tm = min(ROW_TILE, r)
    n_steps = r // tm
    qk_blocks = GLA_HEADS
    v_block0 = 2 * GLA_HEADS * dk // dv
    state = pl.BlockSpec((2, None, dv, dk), lambda h, i: (0, h, 0, 0))
    in_specs, out_specs = [], []
    for d in range(2):
        idx = (lambda i: n_steps - 1 - i) if d else (lambda i: i)
        in_specs += [pl.BlockSpec((tm, dk), lambda h, i, idx=idx: (idx(i), h)),
                     pl.BlockSpec((tm, dk), lambda h, i, idx=idx: (idx(i), qk_blocks + h)),
                     pl.BlockSpec((tm, dv), lambda h, i, idx=idx: (idx(i), v_block0 + h)),
                     pl.BlockSpec((tm, LANES), lambda h, i, idx=idx: (idx(i), 0)),
                     pl.BlockSpec((None, LANES, dk), lambda h, i, d=d: (d, 0, h)),
                     pl.BlockSpec((None, 1, dk), lambda h, i, d=d: (d, 0, h))]
        out_specs.append(pl.BlockSpec((tm, dv), lambda h, i, idx=idx: (idx(i), h)))
    args = (p, p, p, pg, wg_pad, bg)
    return pl.pallas_call(
        functools.partial(_gla_kernel, n_steps=n_steps, dk=dk),
        grid=(GLA_HEADS, n_steps),
        in_specs=in_specs + [state],
        out_specs=out_specs + [state],
        out_shape=[jax.ShapeDtypeStruct((r, GLA_HEADS * dv), F32), jax.ShapeDtypeStruct((r, GLA_HEADS * dv), F32),
                   jax.ShapeDtypeStruct((2, GLA_HEADS, dv, dk), F32)],
        scratch_shapes=[pltpu.VMEM((2, dv, dk), F32), pltpu.VMEM((2, tm, dk), F32)],
        compiler_params=_params("parallel", "arbitrary"), name="gla_scan",
    )(*args, *args, s0)


def _when_block_used(nu_ref, o_ref, body):
    used = pl.program_id(1) < nu_ref[0]

    @pl.when(used)
    def _():
        body()

    @pl.when(jnp.logical_not(used))
    def _():
        o_ref[...] = jnp.zeros_like(o_ref)


FFN_TN = 1408


def pack_gate_up(w_gate, w_up):
    e, d, hidden = w_gate.shape
    tiles = lambda w: w.astype(BF16).reshape(e, d, hidden // FFN_TN, FFN_TN)
    return jnp.concatenate([tiles(w_gate), tiles(w_up)], axis=-1).reshape(e, d, 2 * hidden)


def _ffn1_kernel(be_ref, nu_ref, x_ref, w_ref, o_ref):
    def body():
        ab = jnp.dot(x_ref[...], w_ref[...], preferred_element_type=F32)
        tn = o_ref.shape[1]
        o_ref[...] = (_silu(ab[:, :tn]) * ab[:, tn:]).astype(o_ref.dtype)

    _when_block_used(nu_ref, o_ref, body)


def ffn_up(x, w_gate_up, block_expert, n_used, *, tm):
    r, d = x.shape
    hidden = w_gate_up.shape[2] // 2
    tn = FFN_TN
    return pl.pallas_call(
        _ffn1_kernel,
        grid_spec=pltpu.PrefetchScalarGridSpec(
            num_scalar_prefetch=2, grid=(hidden // tn, r // tm),
            in_specs=[pl.BlockSpec((tm, d), lambda j, i, be, nu: (i, 0)),
                      pl.BlockSpec((None, d, 2 * tn), lambda j, i, be, nu: (be[i], 0, j))],
            out_specs=pl.BlockSpec((tm, tn), lambda j, i, be, nu: (i, j))),
        out_shape=jax.ShapeDtypeStruct((r, hidden), BF16),
        compiler_params=_params("parallel", "arbitrary"), name="ffn_up",
    )(block_expert, n_used, x, w_gate_up)


def _ffn2_res_kernel(be_ref, nu_ref, h_ref, w_ref, x_ref, gt_ref, o_ref):
    def body():
        y = jnp.dot(h_ref[...], w_ref[...], preferred_element_type=F32)
        o_ref[...] = x_ref[...] + gt_ref[...] * y

    _when_block_used(nu_ref, o_ref, body)


def _ffn2_scale_kernel(be_ref, nu_ref, h_ref, w_ref, sw_ref, o_ref):
    def body():
        y = jnp.dot(h_ref[...], w_ref[...], preferred_element_type=F32)
        o_ref[...] = (y * sw_ref[...]).astype(o_ref.dtype)

    _when_block_used(nu_ref, o_ref, body)


def ffn_down_residual(h, w_down, block_expert, n_used, x, gate, *, tm, tn=1024):
    r, hidden = h.shape
    d = w_down.shape[2]
    return pl.pallas_call(
        _ffn2_res_kernel,
        grid_spec=pltpu.PrefetchScalarGridSpec(
            num_scalar_prefetch=2, grid=(d // tn, r // tm),
            in_specs=[pl.BlockSpec((tm, hidden), lambda j, i, be, nu: (i, 0)),
                      pl.BlockSpec((None, hidden, tn), lambda j, i, be, nu: (be[i], 0, j)),
                      pl.BlockSpec((tm, tn), lambda j, i, be, nu: (i, j)),
                      pl.BlockSpec((1, tn), lambda j, i, be, nu: (0, j))],
            out_specs=pl.BlockSpec((tm, tn), lambda j, i, be, nu: (i, j))),
        out_shape=jax.ShapeDtypeStruct((r, d), F32),
        compiler_params=_params("parallel", "arbitrary"), name="ffn_down_residual",
    )(block_expert, n_used, h, w_down, x, gate.reshape(1, d))


def ffn_down_scaled(h, w_down, block_expert, n_used, slot_w, *, tn=1024):
    r, hidden = h.shape
    d = w_down.shape[2]
    tm = min(MOE_ROWS, r)
    return pl.pallas_call(
        _ffn2_scale_kernel,
        grid_spec=pltpu.PrefetchScalarGridSpec(
            num_scalar_prefetch=2, grid=(d // tn, r // tm),
            in_specs=[pl.BlockSpec((tm, hidden), lambda j, i, be, nu: (i, 0)),
                      pl.BlockSpec((None, hidden, tn), lambda j, i, be, nu: (be[i], 0, j)),
                      pl.BlockSpec((tm, 1), lambda j, i, be, nu: (i, 0))],
            out_specs=pl.BlockSpec((tm, tn), lambda j, i, be, nu: (i, j))),
        out_shape=jax.ShapeDtypeStruct((r, d), BF16),
        compiler_params=_params("parallel", "arbitrary"), name="ffn_down_scaled",
    )(block_expert, n_used, h, w_down, slot_w.reshape(r, 1))


def _router_kernel(x_ref, g_ref, sh_ref, sc_ref, rw_ref, rb_ref, h_ref, idx_ref, wt_ref, rank_ref, cnt_ref):
    x = x_ref[...]
    y = x * lax.rsqrt(jnp.mean(x * x, axis=-1, keepdims=True) + EPS)
    h = (y * g_ref[...]) * (1.0 + sc_ref[...]) + sh_ref[...]
    hb = h.astype(BF16)
    h_ref[...] = hb
    logits = jnp.dot(hb, rw_ref[...], preferred_element_type=F32) + rb_ref[...]
    lane = lax.broadcasted_iota(I32, logits.shape, 1)
    neg = jnp.float32(-jnp.inf)
    logits = jnp.where(lane < N_EXPERTS, logits, neg)
    m0 = jnp.max(logits, axis=-1, keepdims=True)
    i0 = jnp.min(jnp.where(logits == m0, lane, LANES), axis=-1, keepdims=True)
    rest = jnp.where(lane == i0, neg, logits)
    m1 = jnp.max(rest, axis=-1, keepdims=True)
    i1 = jnp.min(jnp.where(rest == m1, lane, LANES), axis=-1, keepdims=True)
    e1 = jnp.exp(m1 - m0)
    w0 = 1.0 / (1.0 + e1)
    idx_ref[...] = jnp.where(lane == 0, i0, jnp.where(lane == 1, i1, 0))
    wt_ref[...] = jnp.where(lane == 0, w0, jnp.where(lane == 1, e1 * w0, 0.0))
    tm = x.shape[0]
    hot = (lane == i0) | (lane == i1)
    earlier = (lax.broadcasted_iota(I32, (tm, tm), 1) < lax.broadcasted_iota(I32, (tm, tm), 0)).astype(BF16)
    before = jnp.dot(earlier, hot.astype(BF16), preferred_element_type=F32)
    r0 = jnp.sum(jnp.where(lane == i0, before, 0.0), axis=-1, keepdims=True)
    r1 = jnp.sum(jnp.where(lane == i1, before, 0.0), axis=-1, keepdims=True)
    rank_ref[...] = jnp.where(lane == 0, r0, jnp.where(lane == 1, r1, 0.0)).astype(I32)
    cnt_ref[...] = jnp.broadcast_to(jnp.sum(hot.astype(F32), axis=0, keepdims=True), cnt_ref.shape).astype(I32)


def route(x, g, sh, sc, router_w, router_b):
    t, d = x.shape
    tm = min(ROW_TILE, t)
    vec = pl.BlockSpec((1, d), lambda i: (0, 0))
    row = lambda w: pl.BlockSpec((tm, w), lambda i: (i, 0))
    rw = jnp.zeros((d, LANES), BF16).at[:, :N_EXPERTS].set(router_w.astype(BF16))
    rb = jnp.zeros((1, LANES), F32).at[0, :N_EXPERTS].set(router_b)
    h, idx, wt, rank, cnt = pl.pallas_call(
        _router_kernel, grid=(t // tm,),
        in_specs=[row(d), vec, vec, vec, pl.BlockSpec((d, LANES), lambda i: (0, 0)),
                  pl.BlockSpec((1, LANES), lambda i: (0, 0))],
        out_specs=[row(d), row(LANES), row(LANES), row(LANES), pl.BlockSpec((8, LANES), lambda i: (i, 0))],
        out_shape=[jax.ShapeDtypeStruct((t, d), BF16), jax.ShapeDtypeStruct((t, LANES), I32),
                   jax.ShapeDtypeStruct((t, LANES), F32), jax.ShapeDtypeStruct((t, LANES), I32),
                   jax.ShapeDtypeStruct((t // tm * 8, LANES), I32)],
        compiler_params=_params("parallel"), name="route",
    )(x, g.reshape(1, d), sh.reshape(1, d), sc.reshape(1, d), rw, rb)
    return h, idx[:, :2], wt[:, :2], rank[:, :2], cnt.reshape(t // tm, 8, LANES)[:, 0, :N_EXPERTS]


def _gather_kernel(pb_ref, pt_ref, pf_ref, h_ref, dest_ref, wt_ref, o_ref, sw_ref, acc_ref, swacc_ref):
    i = pl.program_id(0)
    flags = pf_ref[i]

    @pl.when((flags & 1) != 0)
    def _():
        acc_ref[...] = jnp.zeros_like(acc_ref)
        swacc_ref[...] = jnp.zeros_like(swacc_ref)

    @pl.when((flags & 4) != 0)
    def _():
        slot = pb_ref[i] * MOE_ROWS + lax.broadcasted_iota(I32, (MOE_ROWS, 1), 0)
        hit0 = dest_ref[0:1, :] == slot
        hit1 = dest_ref[1:2, :] == slot
        acc_ref[...] += jnp.dot((hit0 | hit1).astype(BF16), h_ref[...], preferred_element_type=F32)
        swacc_ref[...] += jnp.sum(jnp.where(hit0, wt_ref[0:1, :], 0.0) + jnp.where(hit1, wt_ref[1:2, :], 0.0),
                                  axis=1, keepdims=True)

    @pl.when((flags & 2) != 0)
    def _():
        o_ref[...] = acc_ref[...].astype(o_ref.dtype)
        sw_ref[...] = swacc_ref[...]


def moe_gather(h, dest_t, wt_t, n_slots, pair_block, pair_tile, pair_flags):
    t, d = h.shape
    tm = min(ROW_TILE, t)
    n_pairs = pair_block.shape[0]
    tok = pl.BlockSpec((2, tm), lambda i, pb, pt, pf: (0, pt[i]))
    return pl.pallas_call(
        _gather_kernel,
        grid_spec=pltpu.PrefetchScalarGridSpec(
            num_scalar_prefetch=3, grid=(n_pairs,),
            in_specs=[pl.BlockSpec((tm, d), lambda i, pb, pt, pf: (pt[i], 0)), tok, tok],
            out_specs=[pl.BlockSpec((MOE_ROWS, d), lambda i, pb, pt, pf: (pb[i], 0)),
                       pl.BlockSpec((MOE_ROWS, 1), lambda i, pb, pt, pf: (pb[i], 0))],
            scratch_shapes=[pltpu.VMEM((MOE_ROWS, d), F32), pltpu.VMEM((MOE_ROWS, 1), F32)]),
        out_shape=[jax.ShapeDtypeStruct((n_slots, d), BF16), jax.ShapeDtypeStruct((n_slots, 1), F32)],
        compiler_params=_params("arbitrary"), name="moe_gather",
    )(pair_block, pair_tile, pair_flags, h, dest_t, wt_t)


def _combine_kernel(pb_ref, pt_ref, pf_ref, yb_ref, dest_ref, x_ref, gt_ref, ng_ref, o_ref, acc_ref):
    i = pl.program_id(0)
    flags = pf_ref[i]

    @pl.when((flags & 1) != 0)
    def _():
        acc_ref[...] = jnp.zeros_like(acc_ref)

    @pl.when((flags & 4) != 0)
    def _():
        slot = pb_ref[i] * MOE_ROWS + lax.broadcasted_iota(I32, (dest_ref.shape[0], MOE_ROWS), 1)
        hit = (dest_ref[:, 0:1] == slot) | (dest_ref[:, 1:2] == slot)
        acc_ref[...] += jnp.dot(hit.astype(BF16), yb_ref[...], preferred_element_type=F32)

    @pl.when((flags & 2) != 0)
    def _():
        y = x_ref[...] + gt_ref[...] * acc_ref[...]
        o_ref[...] = y * lax.rsqrt(jnp.mean(y * y, axis=-1, keepdims=True) + EPS) * ng_ref[...]


def moe_combine_norm(yb, dest, pair_block, pair_tile, pair_flags, x, gate, norm_g):
    t, d = x.shape
    tm = min(ROW_TILE, t)
    n_pairs = pair_block.shape[0]
    return pl.pallas_call(
        _combine_kernel,
        grid_spec=pltpu.PrefetchScalarGridSpec(
            num_scalar_prefetch=3, grid=(n_pairs,),
            in_specs=[pl.BlockSpec((MOE_ROWS, d), lambda i, pb, pt, pf: (pb[i], 0)),
                      pl.BlockSpec((tm, 2), lambda i, pb, pt, pf: (pt[i], 0)),
                      pl.BlockSpec((tm, d), lambda i, pb, pt, pf: (pt[i], 0)),
                      pl.BlockSpec((1, d), lambda i, pb, pt, pf: (0, 0)),
                      pl.BlockSpec((1, d), lambda i, pb, pt, pf: (0, 0))],
            out_specs=pl.BlockSpec((tm, d), lambda i, pb, pt, pf: (pt[i], 0)),
            scratch_shapes=[pltpu.VMEM((tm, d), F32)]),
        out_shape=jax.ShapeDtypeStruct((t, d), F32),
        compiler_params=_params("arbitrary"), name="moe_combine",
    )(pair_block, pair_tile, pair_flags, yb, dest, x, gate.reshape(1, d), norm_g.reshape(1, d))


def _pair_lists(lo, hi, nonempty, n_pairs):
    cnt = jnp.where(nonempty, hi - lo + 1, 1)
    end = jnp.cumsum(cnt)
    start = end - cnt
    i = jnp.arange(n_pairs, dtype=I32)
    ic = jnp.minimum(i, end[-1] - 1)
    grp = jnp.sum((end[None, :] <= ic[:, None]).astype(I32), axis=1)
    off = ic - start[grp]
    member = jnp.where(nonempty[grp], lo[grp] + off, 0).astype(I32)
    return grp, member, off == 0, off == cnt[grp] - 1, nonempty[grp], i < end[-1]


def _pair_flags(first, last, data, valid):
    flags = jnp.where(first, 1, 0) | jnp.where(last, 2, 0) | jnp.where(data, 4, 0)
    return jnp.where(valid, flags, 0).astype(I32)


def moe_plan(top_idx, rank, tile_cnt, n_tok):
    tm = min(ROW_TILE, n_tok)
    n_tiles = n_tok // tm
    n_blocks = 2 * n_tok // MOE_ROWS + N_EXPERTS
    experts = jnp.arange(N_EXPERTS, dtype=I32)
    tile_off = jnp.cumsum(tile_cnt, axis=0) - tile_cnt
    counts = jnp.sum(tile_cnt, axis=0)
    padded = (counts + MOE_ROWS - 1) // MOE_ROWS * MOE_ROWS
    pad_end = jnp.cumsum(padded)
    base = (pad_end - padded)[None, :] + tile_off
    hot = top_idx[:, :, None] == experts[None, None, :]
    dest = (jnp.sum(jnp.where(hot, jnp.repeat(base, tm, axis=0)[:, None, :], 0), axis=-1) + rank).astype(I32)
    blk = jnp.arange(n_blocks, dtype=I32)
    block_expert = jnp.minimum(jnp.sum((pad_end[None, :] <= (blk * MOE_ROWS)[:, None]).astype(I32), axis=1),
                               N_EXPERTS - 1)
    n_pairs = N_EXPERTS * n_tiles + n_blocks
    c_grp, c_block, first, last, data, valid = _pair_lists(
        (base // MOE_ROWS).reshape(-1), ((base + tile_cnt - 1) // MOE_ROWS).reshape(-1), (tile_cnt > 0).reshape(-1),
        n_pairs)
    c_flags = _pair_flags(first & (c_grp % N_EXPERTS == 0), last & (c_grp % N_EXPERTS == N_EXPERTS - 1), data, valid)
    mine = block_expert[:, None, None] == experts[None, None, :]
    base_b = jnp.sum(jnp.where(mine, base[None], 0), axis=-1)
    cnt_b = jnp.sum(jnp.where(mine, tile_cnt[None], 0), axis=-1)
    sends = (cnt_b > 0) & (base_b + cnt_b > (blk * MOE_ROWS)[:, None]) & (base_b < ((blk + 1) * MOE_ROWS)[:, None])
    tiles = jnp.arange(n_tiles, dtype=I32)[None, :]
    g_block, g_tile, *g_bits = _pair_lists(jnp.min(jnp.where(sends, tiles, n_tiles), axis=1),
                                           jnp.max(jnp.where(sends, tiles, -1), axis=1), jnp.any(sends, axis=1), n_pairs)
    return dict(dest=dest, block_expert=block_expert.astype(I32), n_slots=n_blocks * MOE_ROWS,
                n_used=(pad_end[-1:] // MOE_ROWS).astype(I32),
                gather=(g_block, g_tile, _pair_flags(*g_bits)), combine=(c_block, c_grp // N_EXPERTS, c_flags))


def _mods(mod_all, layer, row, d):
    m = mod_all[layer, row]
    return tuple(m[j * d:(j + 1) * d] for j in range(N_MOD))


def _even_layer(x_lat, x_ctx, mods_lat, mods_ctx, norm1_g, norm2_g, w):
    d = x_lat.shape[1]
    hw = DN_HEADS * DN_DH
    s_dn = jnp.zeros((2, DN_HEADS, DN_DH, DN_DH), F32)
    s_lru = jnp.zeros((2, 8, hw), F32)
    outs = []
    for x, mods in ((x_ctx, mods_ctx), (x_lat, mods_lat)):
        sh1, sc1, gt1, sh2, sc2, gt2 = mods
        p, pg = norm_proj(x, norm1_g, sh1, sc1, w["w_main"], w["w_gate"])
        q, k, v, xc, gx, gxt = even_prep(p, pg, w["conv_w"], w["conv_b"], w["a_log"], w["dt_bias"])
        u, wm, qd, kd, qk, gl = deltanet_prep(q, k, v, gx, gxt)
        o_f, o_b, s_dn = deltanet_scan(u, wm, qd, kd, qk, gl, s_dn)
        h_f, h_b, s_lru = lru_scan(xc, w["lru_wa"], w["lru_wx"], w["lru_ba"], w["lru_bx"], w["lru_lam"], s_lru)
        x = even_finish(o_f, o_b, h_f, h_b, p, w["dn_norm_g"], w["w_out"], x, gt1)
        h2 = norm_mod(x, norm2_g, sh2, sc2)
        tm = _dense_rows(h2.shape[0])
        be = jnp.zeros((h2.shape[0] // tm,), I32)
        every = jnp.full((1,), h2.shape[0] // tm, I32)
        hh = ffn_up(h2, w["ffn_gate_up"], be, every, tm=tm)
        x = ffn_down_residual(hh, w["ffn_down"], be, every, x, gt2, tm=tm)
        outs.append(x)
    return outs[1], outs[0]


def _odd_layer_last(x_lat, x_ctx, mods_lat, mods_ctx, norm1_g, norm2_g, final_g, w):
    dk, dv = w["dk"], w["dv"]
    states = jnp.zeros((2, GLA_HEADS, dv, dk), F32)
    sh1, sc1 = mods_ctx[0], mods_ctx[1]
    p, pg = norm_proj(x_ctx, norm1_g, sh1, sc1, w["w_main"], w["w_gate"])
    _, _, states = gla_scan(p, pg, w["wg_pad"], w["bg"], states, dk=dk, dv=dv)
    sh1, sc1, gt1, sh2, sc2, gt2 = mods_lat
    p, pg = norm_proj(x_lat, norm1_g, sh1, sc1, w["w_main"], w["w_gate"], raster=True)
    o_f, o_b, _ = gla_scan(p, pg, w["wg_pad"], w["bg"], states, dk=dk, dv=dv)
    x = odd_finish(o_f, o_b, p, w["gla_norm_g"], w["w_out"], x_lat, gt1)
    n_tok = x.shape[0]
    h2, top_idx, top_w, rank, tile_cnt = route(x, norm2_g, sh2, sc2, w["router_w"], w["router_b"])
    plan = moe_plan(top_idx, rank, tile_cnt, n_tok)
    xb, slot_w = moe_gather(h2, plan["dest"].T, top_w.T, plan["n_slots"], *plan["gather"])
    hh = ffn_up(xb, w["exp_gate_up"], plan["block_expert"], plan["n_used"], tm=MOE_ROWS)
    yb = ffn_down_scaled(hh, w["exp_down"], plan["block_expert"], plan["n_used"], slot_w)
    return moe_combine_norm(yb, plan["dest"], *plan["combine"], x, gt2, final_g)


def kernel(x, c, ctx, c_ctx, mod_w, mod_b, norm1_g, norm2_g, ev_w_in, ev_conv_qkv, ev_dn_a_log, ev_dn_dt_bias,
           ev_dn_norm_g, ev_lru_conv_w, ev_lru_conv_b, ev_lru_wa, ev_lru_ba, ev_lru_wx, ev_lru_bx, ev_lru_lambda,
           ev_w_out, ev_ffn_w_gate, ev_ffn_w_up, ev_ffn_w_down, od_w_in, od_gla_wg2, od_gla_bg, od_gla_norm_g,
           od_w_out, od_router_w, od_router_b, od_exp_w_gate, od_exp_w_up, od_exp_w_down, final_norm_g):
    b_, length, d = x.shape
    assert b_ == 1 and mod_w.shape[0] == 2, "this kernel implements the batch-1, depth-2 configuration"
    hw = DN_HEADS * DN_DH
    x_lat, x_ctx = x[0], ctx[0]

    cond8 = jnp.zeros((8, d), F32).at[0].set(c[0]).at[1].set(c_ctx)
    mod_all = adaln_all(cond8, mod_w, mod_b)

    w_in = ev_w_in[0]
    qkv_w, z0, ab0, xr0, gr0 = 3 * hw, 3 * hw, 4 * hw, 4 * hw + 4 * DN_HEADS, 5 * hw + 4 * DN_HEADS
    w_main = jnp.concatenate([w_in[:, :qkv_w], w_in[:, xr0:xr0 + hw], w_in[:, z0:z0 + hw], w_in[:, gr0:gr0 + hw]],
                             axis=1).astype(BF16)
    w_gate = jnp.zeros((d, LANES), BF16).at[:, :4 * DN_HEADS].set(w_in[:, ab0:ab0 + 4 * DN_HEADS].astype(BF16))
    pad16 = lambda t: jnp.zeros((1, LANES), F32).at[0, :2 * DN_HEADS].set(t.reshape(-1))
    ev = dict(
        w_main=w_main, w_gate=w_gate,
        conv_w=jnp.concatenate([ev_conv_qkv[0], ev_lru_conv_w[0]], axis=1),
        conv_b=jnp.concatenate([jnp.zeros((qkv_w,), F32), ev_lru_conv_b[0]]).reshape(1, -1),
        a_log=pad16(ev_dn_a_log[0]), dt_bias=pad16(ev_dn_dt_bias[0]),
        dn_norm_g=ev_dn_norm_g[0],
        lru_wa=ev_lru_wa[0].astype(BF16), lru_wx=ev_lru_wx[0].astype(BF16),
        lru_ba=ev_lru_ba[0].reshape(2, 1, hw), lru_bx=ev_lru_bx[0].reshape(2, 1, hw),
        lru_lam=ev_lru_lambda[0].reshape(2, 1, hw),
        w_out=ev_w_out[0].astype(BF16),
        ffn_gate_up=pack_gate_up(ev_ffn_w_gate, ev_ffn_w_up), ffn_down=ev_ffn_w_down.astype(BF16),
    )
    x_lat, x_ctx = _even_layer(x_lat, x_ctx, _mods(mod_all, 0, 0, d), _mods(mod_all, 0, 1, d),
                               norm1_g[0], norm2_g[0], ev)

    w_in = od_w_in[0]
    qk_w = od_gla_wg2.shape[-1]
    v_w = od_w_out.shape[1]
    main_w = 2 * qk_w + 2 * v_w
    wg_pad = jnp.zeros((2, LANES, qk_w), F32)
    for dirn in range(2):
        wg_pad = wg_pad.at[dirn, dirn * GLA_RANK:(dirn + 1) * GLA_RANK].set(od_gla_wg2[0, dirn])
    od = dict(
        w_main=w_in[:, :main_w].astype(BF16),
        w_gate=jnp.zeros((d, LANES), BF16).at[:, :2 * GLA_RANK].set(w_in[:, main_w:].astype(BF16)),
        wg_pad=wg_pad, bg=od_gla_bg[0].reshape(2, 1, qk_w), gla_norm_g=od_gla_norm_g[0],
        w_out=od_w_out[0].astype(BF16), router_w=od_router_w[0], router_b=od_router_b[0],
        exp_gate_up=pack_gate_up(od_exp_w_gate[0], od_exp_w_up[0]),
        exp_down=od_exp_w_down[0].astype(BF16),
        dk=qk_w // GLA_HEADS, dv=v_w // GLA_HEADS,
    )
    return _odd_layer_last(x_lat, x_ctx, _mods(mod_all, 1, 0, d), _mods(mod_all, 1, 1, d),
                           norm1_g[1], norm2_g[1], final_norm_g, od)[None]
```

```python
import functools
import math

import jax
import jax.numpy as jnp
from jax import lax
from jax.experimental import pallas as pl
from jax.experimental.pallas import tpu as pltpu

F32 = jnp.float32
BF16 = jnp.bfloat16
I32 = jnp.int32
HI = lax.Precision.HIGHEST

EPS = 1e-6
N_MOD = 6
GRID_W = 64
CHUNK = 64
SUB = 8
LANES = 128
ROW_TILE = 256
DN_HEADS = 8
DN_DH = 128
LRU_BLOCKS = 8
LRU_BW = 128
LRU_C = 8.0
LRU_UNROLL = 4
GLA_HEADS = 4
GLA_RANK = 16
GLA_TAU = 16.0
GLA_SAFE_STEP = 7.5
N_EXPERTS = 8
MOE_ROWS = 256
VMEM_LIMIT = 56 * 1024 * 1024

NT_DIMS = (((1,), (1,)), ((), ()))
TN_DIMS = (((0,), (0,)), ((), ()))


def _params(*sem):
    return pltpu.CompilerParams(dimension_semantics=sem, vmem_limit_bytes=VMEM_LIMIT)


def _softplus(x):
    return jnp.maximum(x, 0.0) + jnp.log1p(jnp.exp(-jnp.abs(x)))


def _silu(x):
    return x * jax.nn.sigmoid(x)


def _bdot(a, b):
    return jnp.dot(a.astype(BF16), b.astype(BF16), preferred_element_type=F32)


def _adaln_kernel(cond_ref, w_ref, b_ref, o_ref):
    s = _silu(cond_ref[...])
    o_ref[...] = jnp.dot(s, w_ref[...], precision=HI, preferred_element_type=F32) + b_ref[...]


def adaln_all(cond8, mod_w, mod_b):
    n_layers, d, n6 = mod_w.shape
    tn = 1024
    return pl.pallas_call(
        _adaln_kernel,
        grid=(n_layers, n6 // tn),
        in_specs=[pl.BlockSpec((8, d), lambda l, j: (0, 0)),
                  pl.BlockSpec((None, d, tn), lambda l, j: (l, 0, j)),
                  pl.BlockSpec((None, 1, tn), lambda l, j: (l, 0, j))],
        out_specs=pl.BlockSpec((None, 8, tn), lambda l, j: (l, 0, j)),
        out_shape=jax.ShapeDtypeStruct((n_layers, 8, n6), F32),
        compiler_params=_params("arbitrary", "arbitrary"),
        name="adaln",
    )(cond8, mod_w, mod_b.reshape(n_layers, 1, n6))


def _norm_mod_kernel(x_ref, g_ref, sh_ref, sc_ref, o_ref):
    x = x_ref[...]
    y = x * lax.rsqrt(jnp.mean(x * x, axis=-1, keepdims=True) + EPS)
    o_ref[...] = ((y * g_ref[...]) * (1.0 + sc_ref[...]) + sh_ref[...]).astype(o_ref.dtype)


def _raster_spec(rows, d):
    return pl.BlockSpec((rows, d), lambda c: (0, c))


def norm_mod(x, g, sh, sc):
    n, d = x.shape
    tm = min(ROW_TILE, n)
    vec = pl.BlockSpec((1, d), lambda i: (0, 0))
    row = pl.BlockSpec((tm, d), lambda i: (i, 0))
    return pl.pallas_call(
        _norm_mod_kernel, grid=(n // tm,), in_specs=[row, vec, vec, vec], out_specs=row,
        out_shape=jax.ShapeDtypeStruct((n, d), BF16),
        compiler_params=_params("parallel"), name="norm_mod",
    )(x, g.reshape(1, d), sh.reshape(1, d), sc.reshape(1, d))


def _dense_rows(r):
    return 2 * ROW_TILE if r % (2 * ROW_TILE) == 0 else min(ROW_TILE, r)


def _norm_proj_kernel(x_ref, g_ref, sh_ref, sc_ref, w_ref, wg_ref, p_ref, pg_ref, h_s, *, d):
    @pl.when(pl.program_id(1) == 0)
    def _():
        rows = x_ref.shape[0]
        for col in range(x_ref.shape[1] // d):
            x = x_ref[:, col * d:(col + 1) * d]
            y = x * lax.rsqrt(jnp.mean(x * x, axis=-1, keepdims=True) + EPS)
            h_s[col * rows:(col + 1) * rows, :] = ((y * g_ref[...]) * (1.0 + sc_ref[...]) + sh_ref[...]).astype(BF16)
        pg_ref[...] = jnp.dot(h_s[...], wg_ref[...], preferred_element_type=F32)

    p_ref[...] = jnp.dot(h_s[...], w_ref[...], preferred_element_type=F32)


def norm_proj(x, g, sh, sc, w_main, w_gate, *, raster=False, tn=1024):
    n, d = x.shape
    n_main = w_main.shape[1]
    tm = next(t for t in (4 * ROW_TILE, 2 * ROW_TILE, min(ROW_TILE, n)) if n % t == 0)
    if raster:
        rows = n // GRID_W
        n_col = max(tm // rows, 1)
        tm = n_col * rows
        x_in, x_spec = x.reshape(rows, GRID_W * d), pl.BlockSpec((rows, n_col * d), lambda i, j: (0, i))
    else:
        x_in, x_spec = x, pl.BlockSpec((tm, d), lambda i, j: (i, 0))
    vec = pl.BlockSpec((1, d), lambda i, j: (0, 0))
    return pl.pallas_call(
        functools.partial(_norm_proj_kernel, d=d), grid=(n // tm, n_main // tn),
        in_specs=[x_spec, vec, vec, vec, pl.BlockSpec((d, tn), lambda i, j: (0, j)),
                  pl.BlockSpec((d, LANES), lambda i, j: (0, 0))],
        out_specs=[pl.BlockSpec((tm, tn), lambda i, j: (i, j)), pl.BlockSpec((tm, LANES), lambda i, j: (i, 0))],
        out_shape=[jax.ShapeDtypeStruct((n, n_main), F32), jax.ShapeDtypeStruct((n, LANES), F32)],
        scratch_shapes=[pltpu.VMEM((tm, d), BF16)],
        compiler_params=_params("parallel", "arbitrary"), name="norm_proj",
    )(x_in, g.reshape(1, d), sh.reshape(1, d), sc.reshape(1, d), w_main, w_gate)


def _evprep_kernel(p_ref, prev_ref, next_ref, pg_ref, cw_ref, cb_ref, alog_ref, dtb_ref,
                   q_ref, k_ref, v_ref, xc_ref, gx_ref, gxt_ref, ext_ref, *, n_tiles):
    i = pl.program_id(0)
    tm, width = p_ref.shape
    ext_ref[8:8 + tm, :] = p_ref[...]
    ext_ref[0:8, :] = jnp.where(i > 0, prev_ref[...], 0.0)
    ext_ref[8 + tm:16 + tm, :] = jnp.where(i < n_tiles - 1, next_ref[...], 0.0)
    acc = ext_ref[pl.ds(6, tm), :] * cw_ref[0:1, :]
    for j in range(1, 4):
        acc = acc + ext_ref[pl.ds(6 + j, tm), :] * cw_ref[j:j + 1, :]
    acc = acc + cb_ref[...]
    qk_w = DN_HEADS * DN_DH
    for hd in range(DN_HEADS):
        for part, ref, scale in ((0, q_ref, DN_DH ** -0.5), (1, k_ref, 1.0)):
            lo = part * qk_w + hd * DN_DH
            t = _silu(acc[:, lo:lo + DN_DH])
            t = t * lax.rsqrt(jnp.sum(t * t, axis=-1, keepdims=True) + EPS)
            ref[:, hd * DN_DH:(hd + 1) * DN_DH] = t * scale
    v_ref[...] = _silu(acc[:, 2 * qk_w:3 * qk_w])
    xc_ref[...] = acc[:, 3 * qk_w:]
    pg = pg_ref[...]
    lane = lax.broadcasted_iota(I32, pg.shape, 1)
    g = -jnp.exp(alog_ref[...]) * _softplus(pg + dtb_ref[...])
    ri = lax.broadcasted_iota(I32, (tm, tm), 0)
    ci = lax.broadcasted_iota(I32, (tm, tm), 1)
    same = (ri // CHUNK) == (ci // CHUNK)
    cum_f = _dot01((same & (ci <= ri)).astype(BF16), g)
    cum_b = _dot01((same & (ci >= ri)).astype(BF16), g)
    tot = _dot01(same.astype(BF16), g)
    gc = jnp.where(lane < DN_HEADS, cum_f, cum_b)
    gx = jnp.where(lane < 2 * DN_HEADS, gc,
                   jnp.where(lane < 4 * DN_HEADS, jax.nn.sigmoid(pg), pltpu.roll(tot, 4 * DN_HEADS, 1)))
    gx_ref[...] = gx
    gxt_ref[...] = gx.T


def _dot01(m01, x):
    x1 = x.astype(BF16)
    r1 = x - x1.astype(F32)
    x2 = r1.astype(BF16)
    x3 = (r1 - x2.astype(F32)).astype(BF16)
    dot = lambda t: jnp.dot(m01, t, preferred_element_type=F32)
    return dot(x1) + dot(x2) + dot(x3)


def even_prep(p, pg, conv_w, conv_b, a_log, dt_bias):
    r = p.shape[0]
    width = conv_w.shape[1]
    tm = min(ROW_TILE, r)
    n_tiles = r // tm
    hb = tm // 8
    out_w = DN_HEADS * DN_DH
    row = lambda w: pl.BlockSpec((tm, w), lambda i: (i, 0))
    vec = lambda w: pl.BlockSpec((1, w), lambda i: (0, 0))
    return pl.pallas_call(
        functools.partial(_evprep_kernel, n_tiles=n_tiles),
        grid=(n_tiles,),
        in_specs=[row(width),
                  pl.BlockSpec((8, width), lambda i: (jnp.maximum(i * hb - 1, 0), 0)),
                  pl.BlockSpec((8, width), lambda i: (jnp.minimum((i + 1) * hb, r // 8 - 1), 0)),
                  row(LANES),
                  pl.BlockSpec((4, width), lambda i: (0, 0)), vec(width), vec(LANES), vec(LANES)],
        out_specs=[row(out_w), row(out_w), row(out_w), row(out_w), row(LANES),
                   pl.BlockSpec((LANES, tm), lambda i: (0, i))],
        out_shape=[jax.ShapeDtypeStruct((r, out_w), F32)] * 4
        + [jax.ShapeDtypeStruct((r, LANES), F32), jax.ShapeDtypeStruct((LANES, r), F32)],
        scratch_shapes=[pltpu.VMEM((tm + 16, width), F32)],
        compiler_params=_params("parallel"), name="even_prep",
    )(p, p, p, pg, conv_w, conv_b, a_log, dt_bias)


DN_BASE = 16
DN_HEADS_PER_STEP = 4


def _dnprep_kernel(q_ref, k_ref, v_ref, gx_ref, gxt_ref, u_ref, w_ref, qd_ref, kd_ref, qk_ref, gl_ref, *, n_chunks):
    c = CHUNK
    tm = q_ref.shape[0]
    ri = lax.broadcasted_iota(I32, (tm, tm), 0)
    ci = lax.broadcasted_iota(I32, (tm, tm), 1)
    same = lambda s: (ri // s) == (ci // s)
    eye = (ri == ci).astype(F32)
    lane = lax.broadcasted_iota(I32, (tm, LANES), 1)
    gx = gx_ref[...]
    pick = lambda idx: jnp.sum(jnp.where(lane == idx, gx, 0.0), axis=1, keepdims=True)
    chains = []
    for hh in range(DN_HEADS_PER_STEP):
        hd = pl.program_id(1) * DN_HEADS_PER_STEP + hh
        cols = slice(hh * DN_DH, (hh + 1) * DN_DH)
        q = q_ref[:, cols]
        k = k_ref[:, cols]
        kb16 = k.astype(BF16)
        gram_k = lax.dot_general(kb16, kb16, NT_DIMS, preferred_element_type=F32)
        gram_q = lax.dot_general(q.astype(BF16), kb16, NT_DIMS, preferred_element_type=F32)
        for d in range(2):
            incl = same(c) & ((ci >= ri) if d else (ci <= ri))
            gcol = pick(d * DN_HEADS + hd)
            bcol = pick((2 + d) * DN_HEADS + hd)
            tcol = pick((4 + d) * DN_HEADS + hd)
            grow = gxt_ref[pl.ds(d * DN_HEADS + hd, 1), :]
            decay = jnp.where(incl, jnp.exp(gcol - grow), 0.0)
            a = jnp.where(ri == ci, 0.0, gram_k * bcol * decay)
            eg = jnp.exp(gcol)
            qk = (gram_q * decay).astype(BF16)
            for n in range(n_chunks):
                qk_ref[d, hh, n * c:(n + 1) * c, :] = qk[n * c:(n + 1) * c, n * c:(n + 1) * c]
                gl_ref[d, hh, n] = jnp.broadcast_to(jnp.exp(tcol[n * c:n * c + 1, :]), (8, LANES))
            qd_ref[d, :, cols] = (q * eg).astype(BF16)
            kd_ref[d, :, cols] = (k * jnp.exp(tcol - gcol)).astype(BF16)
            diag = jnp.where(same(DN_BASE), a, 0.0)
            chains.append(dict(d=d, cols=cols, a=a, t=eye - diag, p=diag, scale=bcol, scale_k=bcol * eg))
    size = 2
    while size < DN_BASE:
        for ch in chains:
            ch["p"] = _bdot(ch["p"], ch["p"])
        for ch in chains:
            ch["t"] = ch["t"] + _bdot(ch["t"], ch["p"])
        size *= 2
    size = DN_BASE
    while size < c:
        couple = same(2 * size) & ~same(size)
        for ch in chains:
            ch["et"] = _bdot(jnp.where(couple, ch["a"], 0.0), ch["t"])
        for ch in chains:
            ch["t"] = ch["t"] - _bdot(ch["t"], ch["et"])
        size *= 2
    for ch in chains:
        d, cols = ch["d"], ch["cols"]
        x = _bdot(ch["t"], jnp.concatenate([v_ref[:, cols] * ch["scale"], k_ref[:, cols] * ch["scale_k"]], axis=1))
        u_ref[d, :, cols] = x[:, :DN_DH]
        w_ref[d, :, cols] = x[:, DN_DH:].astype(BF16)


def deltanet_prep(q, k, v, gx, gxt):
    r = q.shape[0]
    tm = min(ROW_TILE, r)
    n_chunks = tm // CHUNK
    hw = DN_HEADS * DN_DH
    hps = DN_HEADS_PER_STEP
    head = pl.BlockSpec((tm, hps * DN_DH), lambda i, h: (i, h))
    dhead = pl.BlockSpec((2, tm, hps * DN_DH), lambda i, h: (0, i, h))
    return pl.pallas_call(
        functools.partial(_dnprep_kernel, n_chunks=n_chunks),
        grid=(r // tm, DN_HEADS // hps),
        in_specs=[head, head, head, pl.BlockSpec((tm, LANES), lambda i, h: (i, 0)),
                  pl.BlockSpec((LANES, tm), lambda i, h: (0, i))],
        out_specs=[dhead, dhead, dhead, dhead,
                   pl.BlockSpec((2, hps, tm, CHUNK), lambda i, h: (0, h, i, 0)),
                   pl.BlockSpec((2, hps, n_chunks, 8, LANES), lambda i, h: (0, h, i, 0, 0))],
        out_shape=[jax.ShapeDtypeStruct((2, r, hw), F32),
                   jax.ShapeDtypeStruct((2, r, hw), BF16),
                   jax.ShapeDtypeStruct((2, r, hw), BF16),
                   jax.ShapeDtypeStruct((2, r, hw), BF16),
                   jax.ShapeDtypeStruct((2, DN_HEADS, r, CHUNK), BF16),
                   jax.ShapeDtypeStruct((2, DN_HEADS, r // CHUNK, 8, LANES), F32)],
        compiler_params=_params("parallel", "parallel"), name="deltanet_prep",
    )(q, k, v, gx, gxt)


def _dnscan_kernel(*refs, n_steps):
    ins = (refs[0:6], refs[6:12])
    s0_ref, o_refs, sf_ref, s_ref = refs[12], refs[13:15], refs[15], refs[16]
    step = pl.program_id(0)

    @pl.when(step == 0)
    def _():
        s_ref[...] = s0_ref[...]

    c = CHUNK
    n_sub = ins[0][0].shape[0] // c
    col = lambda hd: slice(hd * DN_DH, (hd + 1) * DN_DH)
    dot = lambda a, b: jnp.dot(a, b, preferred_element_type=F32)
    for t in range(n_sub):
        chains = []
        for d in range(2):
            u_ref, w_ref, qd_ref, kd_ref, qk_ref, gl_ref = ins[d]
            n = n_sub - 1 - t if d else t
            rows = slice(n * c, (n + 1) * c)
            for hd in range(DN_HEADS):
                chains.append(dict(d=d, hd=hd, rows=rows, u=u_ref.at[rows, col(hd)], w=w_ref.at[rows, col(hd)],
                                   qd=qd_ref.at[rows, col(hd)], kd=kd_ref.at[rows, col(hd)],
                                   qk=qk_ref.at[hd, rows, :], gl=gl_ref.at[hd, n, 0:1, :]))
        for ch in chains:
            ch["s"] = s_ref[ch["d"], ch["hd"]]
            ch["sb"] = ch["s"].astype(BF16)
        for ch in chains:
            ch["ws"] = dot(ch["w"][...], ch["sb"])
        for ch in chains:
            ch["qs"] = dot(ch["qd"][...], ch["sb"])
        for ch in chains:
            ch["vb"] = (ch["u"][...] - ch["ws"]).astype(BF16)
        for ch in chains:
            o_refs[ch["d"]][ch["rows"], col(ch["hd"])] = ch["qs"] + dot(ch["qk"][...], ch["vb"])
        for ch in chains:
            ch["ds"] = lax.dot_general(ch["kd"][...], ch["vb"], TN_DIMS, preferred_element_type=F32)
        for ch in chains:
            s_ref[ch["d"], ch["hd"]] = ch["s"] * ch["gl"][...] + ch["ds"]

    @pl.when(step == n_steps - 1)
    def _():
        sf_ref[...] = s_ref[...]


def deltanet_scan(u, w, qd, kd, qk, gl, s0):
    r = u.shape[1]
    hw = DN_HEADS * DN_DH
    tm = min(ROW_TILE, r)
    n_steps = r // tm
    state = pl.BlockSpec((2, DN_HEADS, DN_DH, DN_DH), lambda i: (0, 0, 0, 0))
    in_specs, out_specs = [], []
    for d in range(2):
        idx = (lambda i: n_steps - 1 - i) if d else (lambda i: i)
        big = pl.BlockSpec((None, tm, hw), lambda i, d=d, idx=idx: (d, idx(i), 0))
        in_specs += [big, big, big, big,
                     pl.BlockSpec((None, DN_HEADS, tm, CHUNK), lambda i, d=d, idx=idx: (d, 0, idx(i), 0)),
                     pl.BlockSpec((None, DN_HEADS, tm // CHUNK, 8, LANES), lambda i, d=d, idx=idx: (d, 0, idx(i), 0, 0))]
        out_specs.append(pl.BlockSpec((tm, hw), lambda i, idx=idx: (idx(i), 0)))
    return pl.pallas_call(
        functools.partial(_dnscan_kernel, n_steps=n_steps),
        grid=(n_steps,),
        in_specs=in_specs + [state],
        out_specs=out_specs + [state],
        out_shape=[jax.ShapeDtypeStruct((r, hw), F32), jax.ShapeDtypeStruct((r, hw), F32),
                   jax.ShapeDtypeStruct((2, DN_HEADS, DN_DH, DN_DH), F32)],
        scratch_shapes=[pltpu.VMEM((2, DN_HEADS, DN_DH, DN_DH), F32)],
        compiler_params=_params("arbitrary"), name="deltanet_scan",
    )(u, w, qd, kd, qk, gl, u, w, qd, kd, qk, gl, s0)


def _lru_kernel(xf_ref, xb_ref, wa_ref, wx_ref, ba_ref, bx_ref, lam_ref, h0_ref, hf_ref, hb_ref, hl_ref,
                a_s, b_s, carry_s, *, n_steps):
    step = pl.program_id(0)

    @pl.when(step == 0)
    def _():
        carry_s[...] = h0_ref[...]

    tm = xf_ref.shape[0]
    for d, xc_ref in enumerate((xf_ref, xb_ref)):
        sp = _softplus(-lam_ref[d])
        for n in range(LRU_BLOCKS):
            cols = slice(n * LRU_BW, (n + 1) * LRU_BW)
            xb = xc_ref[:, cols]
            xbb = xb.astype(BF16)
            r = jax.nn.sigmoid(jnp.dot(xbb, wa_ref[d, n], preferred_element_type=F32) + ba_ref[d, :, cols])
            gi = jax.nn.sigmoid(jnp.dot(xbb, wx_ref[d, n], preferred_element_type=F32) + bx_ref[d, :, cols])
            log_a = -LRU_C * r * sp[:, cols]
            a = jnp.exp(log_a)
            a_s[d, :, cols] = a
            b_s[d, :, cols] = jnp.sqrt(-jnp.tanh(log_a) * (a * a + 1.0)) * (gi * xb)

    rid = lax.broadcasted_iota(I32, (8, a_s.shape[2]), 0)
    n_groups = tm // 8
    h_refs = (hf_ref, hb_ref)

    def group(gidx, carries):
        out = []
        for d in range(2):
            g = (n_groups - 1 - gidx) if d else gidx
            base = pl.multiple_of(g * 8, 8)
            a = a_s[d, pl.ds(base, 8), :]
            b = b_s[d, pl.ds(base, 8), :]
            for sh in (1, 2, 4):
                if d:
                    keep = rid < 8 - sh
                    a_n = jnp.where(keep, pltpu.roll(a, 8 - sh, 0), 1.0)
                    b_n = jnp.where(keep, pltpu.roll(b, 8 - sh, 0), 0.0)
                else:
                    keep = rid >= sh
                    a_n = jnp.where(keep, pltpu.roll(a, sh, 0), 1.0)
                    b_n = jnp.where(keep, pltpu.roll(b, sh, 0), 0.0)
                b = a * b_n + b
                a = a * a_n
            h = a * carries[d] + b
            h_refs[d][pl.ds(base, 8), :] = h
            edge = h[0:1, :] if d else h[7:8, :]
            out.append(jnp.broadcast_to(edge, h.shape))
        return tuple(out)

    carries = lax.fori_loop(0, n_groups, group, (carry_s[0], carry_s[1]), unroll=LRU_UNROLL)
    carry_s[0] = carries[0]
    carry_s[1] = carries[1]

    @pl.when(step == n_steps - 1)
    def _():
        hl_ref[...] = carry_s[...]


def lru_scan(xc, wa, wx, ba, bx, lam, h0):
    r, width = xc.shape
    tm = min(ROW_TILE, r)
    n_steps = r // tm
    vec = pl.BlockSpec((2, 1, width), lambda i: (0, 0, 0))
    wspec = pl.BlockSpec((2, LRU_BLOCKS, LRU_BW, LRU_BW), lambda i: (0, 0, 0, 0))
    st = pl.BlockSpec((2, 8, width), lambda i: (0, 0, 0))
    fwd = pl.BlockSpec((tm, width), lambda i: (i, 0))
    bwd = pl.BlockSpec((tm, width), lambda i: (n_steps - 1 - i, 0))
    return pl.pallas_call(
        functools.partial(_lru_kernel, n_steps=n_steps),
        grid=(n_steps,),
        in_specs=[fwd, bwd, wspec, wspec, vec, vec, vec, st],
        out_specs=[fwd, bwd, st],
        out_shape=[jax.ShapeDtypeStruct((r, width), F32), jax.ShapeDtypeStruct((r, width), F32),
                   jax.ShapeDtypeStruct((2, 8, width), F32)],
        scratch_shapes=[pltpu.VMEM((2, tm, width), F32), pltpu.VMEM((2, tm, width), F32),
                        pltpu.VMEM((2, 8, width), F32)],
        compiler_params=_params("arbitrary"), name="lru_scan",
    )(xc, xc, wa, wx, ba, bx, lam, h0)


def _gelu_tanh(x):
    return 0.5 * x * (1.0 + jnp.tanh(math.sqrt(2.0 / math.pi) * (x + 0.044715 * (x * x * x))))


def _evfin_kernel(of_ref, ob_ref, hf_ref, hb_ref, zg_ref, ng_ref, wout_ref, x_ref, gt_ref, o_ref, mix_s):
    hw = DN_HEADS * DN_DH
    for hd in range(DN_HEADS):
        cols = slice(hd * DN_DH, (hd + 1) * DN_DH)
        o = of_ref[:, cols] + ob_ref[:, cols]
        y = o * lax.rsqrt(jnp.mean(o * o, axis=-1, keepdims=True) + EPS) * ng_ref[...]
        mix_s[:, cols] = (y * _silu(zg_ref[:, cols])).astype(BF16)
    mix_s[:, hw:] = ((hf_ref[...] + hb_ref[...]) * _gelu_tanh(zg_ref[:, hw:])).astype(BF16)
    y = jnp.dot(mix_s[...], wout_ref[...], preferred_element_type=F32)
    o_ref[...] = x_ref[...] + gt_ref[...] * y


def even_finish(o_f, o_b, h_f, h_b, p, norm_g, w_out, x, gate):
    r, d = x.shape
    tm = min(ROW_TILE, r)
    hw = DN_HEADS * DN_DH
    row = lambda w: pl.BlockSpec((tm, w), lambda i: (i, 0))
    return pl.pallas_call(
        _evfin_kernel, grid=(r // tm,),
        in_specs=[row(hw), row(hw), row(hw), row(hw),
                  pl.BlockSpec((tm, 2 * hw), lambda i: (i, 2)),
                  pl.BlockSpec((1, DN_DH), lambda i: (0, 0)),
                  pl.BlockSpec(w_out.shape, lambda i: (0, 0)),
                  row(d), pl.BlockSpec((1, d), lambda i: (0, 0))],
        out_specs=row(d),
        out_shape=jax.ShapeDtypeStruct((r, d), F32),
        scratch_shapes=[pltpu.VMEM((tm, 2 * hw), BF16)],
        compiler_params=_params("parallel"), name="even_finish",
    )(o_f, o_b, h_f, h_b, p, norm_g.reshape(1, DN_DH), w_out, x, gate.reshape(1, d))


def _odfin_kernel(of_ref, ob_ref, go_ref, ng_ref, wout_ref, x_ref, gt_ref, o_ref, mix_s, *, dv):
    for hd in range(GLA_HEADS):
        cols = slice(hd * dv, (hd + 1) * dv)
        o = of_ref[:, cols] + ob_ref[:, cols]
        y = o * lax.rsqrt(jnp.mean(o * o, axis=-1, keepdims=True) + EPS) * ng_ref[...]
        mix_s[:, cols] = (y * _silu(go_ref[:, cols])).astype(BF16)
    y = jnp.dot(mix_s[...], wout_ref[...], preferred_element_type=F32)
    o_ref[...] = x_ref[...] + gt_ref[...] * y


def odd_finish(o_f, o_b, p, norm_g, w_out, x, gate):
    r, d = x.shape
    rows = r // GRID_W
    vw = o_f.shape[1]
    dv = vw // GLA_HEADS
    row = lambda w: pl.BlockSpec((rows, w), lambda c: (c, 0))
    return pl.pallas_call(
        functools.partial(_odfin_kernel, dv=dv), grid=(GRID_W,),
        in_specs=[row(vw), row(vw),
                  pl.BlockSpec((rows, vw), lambda c: (c, 2)),
                  pl.BlockSpec((1, dv), lambda c: (0, 0)),
                  pl.BlockSpec(w_out.shape, lambda c: (0, 0)),
                  _raster_spec(rows, d), pl.BlockSpec((1, d), lambda c: (0, 0))],
        out_specs=_raster_spec(rows, d),
        out_shape=jax.ShapeDtypeStruct((rows, GRID_W * d), F32),
        scratch_shapes=[pltpu.VMEM((rows, vw), BF16)],
        compiler_params=_params("parallel"), name="odd_finish",
    )(o_f, o_b, p, norm_g.reshape(1, dv), w_out, x.reshape(rows, GRID_W * d), gate.reshape(1, d)).reshape(r, d)


def _gla_kernel(*refs, n_steps, dk):
    ins = (refs[0:6], refs[6:12])
    s0_ref, o_refs, sf_ref, s_ref, gc_s = refs[12], refs[13:15], refs[15], refs[16], refs[17]
    step = pl.program_id(1)

    @pl.when(step == 0)
    def _():
        s_ref[...] = s0_ref[...]

    c = CHUNK
    tm = ins[0][0].shape[0]
    n_chunks = tm // c
    n_sub = c // SUB
    ri = lax.broadcasted_iota(I32, (tm, tm), 0)
    ci = lax.broadcasted_iota(I32, (tm, tm), 1)
    r64 = lax.broadcasted_iota(I32, (c, c), 0)
    c64 = lax.broadcasted_iota(I32, (c, c), 1)
    rr = lax.broadcasted_iota(I32, (c, 1), 0) % SUB
    units = []
    g_min = None
    for d in range(2):
        q_ref, k_ref, v_ref, gd_ref, wg_ref, bg_ref = ins[d]
        tri = (((ri // c) == (ci // c)) & ((ci >= ri) if d else (ci <= ri))).astype(BF16)
        logit = _bdot(gd_ref[...], wg_ref[...]) + bg_ref[...]
        g = -_softplus(-logit) * (1.0 / GLA_TAU)
        gc = _dot01(tri, g)
        gc_s[d] = gc
        g_min = jnp.min(g) if g_min is None else jnp.minimum(g_min, jnp.min(g))
        q = q_ref[...] * dk ** -0.5
        for n in range(n_chunks):
            rows = slice(n * c, (n + 1) * c)
            units.append(dict(d=d, n=n, rows=rows, q=q[rows], k=k_ref[rows, :], g=g[rows], gc=gc[rows],
                              att=jnp.zeros((c, c), F32)))
    size = c // 2
    while size >= SUB:
        r_hi, c_hi = (r64 & size) != 0, (c64 & size) != 0
        same = (r64 // (2 * size)) == (c64 // (2 * size))
        pair = (same & r_hi & ~c_hi, same & ~r_hi & c_hi)
        for u in units:
            gc = u["gc"]
            pieces = []
            for b in range(c // (2 * size)):
                mid = b * 2 * size + size
                edge = gc[mid:mid + 1] if u["d"] else gc[mid - 1:mid]
                pieces.append(jnp.broadcast_to(edge, (2 * size, dk)))
            edge = pieces[0] if len(pieces) == 1 else jnp.concatenate(pieces, axis=0)
            qs = (u["q"] * jnp.exp(gc - edge)).astype(BF16)
            ks = (u["k"] * jnp.exp(edge - gc)).astype(BF16)
            u["att"] = u["att"] + jnp.where(pair[u["d"]],
                                            lax.dot_general(qs, ks, NT_DIMS, preferred_element_type=F32), 0.0)
        size //= 2

    def diag_direct():
        blocks = [jnp.zeros((c, c), F32) for _ in units]
        for jj in range(SUB):
            here = c64 == (r64 // SUB) * SUB + jj
            seen = (here & (rr >= jj), here & (rr <= jj))
            for i, u in enumerate(units):
                rep = lambda ref: jnp.concatenate(
                    [jnp.broadcast_to(ref[pl.ds(u["n"] * c + b * SUB + jj, 1), :], (SUB, dk)) for b in range(n_sub)],
                    axis=0)
                col = jnp.sum(u["q"] * rep(ins[u["d"]][1]) * jnp.exp(u["gc"] - rep(gc_s.at[u["d"]])),
                              axis=-1, keepdims=True)
                blocks[i] = jnp.where(seen[u["d"]], col, blocks[i])
        return tuple(blocks)

    def diag_factored():
        blocks = []
        same = (r64 // SUB) == (c64 // SUB)
        inside = (same & (c64 <= r64), same & (c64 >= r64))
        for u in units:
            before = (u["gc"] - u["g"]).reshape(n_sub, SUB, dk)
            edge = before[:, SUB - 1:SUB, :] if u["d"] else before[:, 0:1, :]
            edge = jnp.broadcast_to(edge, before.shape).reshape(c, dk)
            qs = (u["q"] * jnp.exp(u["gc"] - edge)).astype(BF16)
            ks = (u["k"] * jnp.exp(edge - u["gc"])).astype(BF16)
            blocks.append(jnp.where(inside[u["d"]],
                                    lax.dot_general(qs, ks, NT_DIMS, preferred_element_type=F32), 0.0))
        return tuple(blocks)

    diag = lax.cond(g_min >= -GLA_SAFE_STEP, diag_factored, diag_direct)
    for u in units:
        u["vb"] = ins[u["d"]][2][u["rows"], :].astype(BF16)
        u["tot"] = u["gc"][0:1] if u["d"] else u["gc"][c - 1:c]
    for i, u in enumerate(units):
        u["o"] = jnp.dot((u["att"] + diag[i]).astype(BF16), u["vb"], preferred_element_type=F32)
    for u in units:
        u["qd"] = (u["q"] * jnp.exp(u["gc"])).astype(BF16)
        u["ds"] = lax.dot_general(u["vb"], (u["k"] * jnp.exp(u["tot"] - u["gc"])).astype(BF16), TN_DIMS,
                                  preferred_element_type=F32)
    for t in range(n_chunks):
        for d in range(2):
            u = units[d * n_chunks + (n_chunks - 1 - t if d else t)]
            s = s_ref[d]
            o_refs[d][u["rows"], :] = u["o"] + lax.dot_general(u["qd"], s.astype(BF16), NT_DIMS,
                                                               preferred_element_type=F32)
            s_ref[d] = s * jnp.exp(u["tot"]) + u["ds"]

    @pl.when(step == n_steps - 1)
    def _():
        sf_ref[...] = s_ref[...]


def gla_scan(p, pg, wg_pad, bg, s0, *, dk, dv):
    r = p.shape[0]
    tm = min(ROW_TILE, r)
    n_steps = r // tm
    qk_blocks = GLA_HEADS
    v_block0 = 2 * GLA_HEADS * dk // dv
    state = pl.BlockSpec((2, None, dv, dk), lambda h, i: (0, h, 0, 0))
    in_specs, out_specs = [], []
    for d in range(2):
        idx = (lambda i: n_steps - 1 - i) if d else (lambda i: i)
        in_specs += [pl.BlockSpec((tm, dk), lambda h, i, idx=idx: (idx(i), h)),
                     pl.BlockSpec((tm, dk), lambda h, i, idx=idx: (idx(i), qk_blocks + h)),
                     pl.BlockSpec((tm, dv), lambda h, i, idx=idx: (idx(i), v_block0 + h)),
                     pl.BlockSpec((tm, LANES), lambda h, i, idx=idx: (idx(i), 0)),
                     pl.BlockSpec((None, LANES, dk), lambda h, i, d=d: (d, 0, h)),
                     pl.BlockSpec((None, 1, dk), lambda h, i, d=d: (d, 0, h))]
        out_specs.append(pl.BlockSpec((tm, dv), lambda h, i, idx=idx: (idx(i), h)))
    args = (p, p, p, pg, wg_pad, bg)
    return pl.pallas_call(
        functools.partial(_gla_kernel, n_steps=n_steps, dk=dk),
        grid=(GLA_HEADS, n_steps),
        in_specs=in_specs + [state],
        out_specs=out_specs + [state],
        out_shape=[jax.ShapeDtypeStruct((r, GLA_HEADS * dv), F32), jax.ShapeDtypeStruct((r, GLA_HEADS * dv), F32),
                   jax.ShapeDtypeStruct((2, GLA_HEADS, dv, dk), F32)],
        scratch_shapes=[pltpu.VMEM((2, dv, dk), F32), pltpu.VMEM((2, tm, dk), F32)],
        compiler_params=_params("parallel", "arbitrary"), name="gla_scan",
    )(*args, *args, s0)


def _when_block_used(nu_ref, o_ref, body):
    used = pl.program_id(1) < nu_ref[0]

    @pl.when(used)
    def _():
        body()

    @pl.when(jnp.logical_not(used))
    def _():
        o_ref[...] = jnp.zeros_like(o_ref)


FFN_TN = 1408


def _ffn1_kernel(be_ref, nu_ref, x_ref, wg_ref, wu_ref, o_ref, w_s):
    i = pl.program_id(1)
    tn = o_ref.shape[1]

    @pl.when((i == 0) | (be_ref[i] != be_ref[jnp.maximum(i - 1, 0)]))
    def _():
        w_s[:, :tn] = wg_ref[...]
        w_s[:, tn:] = wu_ref[...]

    def body():
        ab = jnp.dot(x_ref[...], w_s[...], preferred_element_type=F32)
        o_ref[...] = (_silu(ab[:, :tn]) * ab[:, tn:]).astype(o_ref.dtype)

    _when_block_used(nu_ref, o_ref, body)


def ffn_up(x, w_gate, w_up, block_expert, n_used, *, tm):
    r, d = x.shape
    hidden = w_gate.shape[2]
    tn = FFN_TN
    wspec = pl.BlockSpec((None, d, tn), lambda j, i, be, nu: (be[i], 0, j))
    return pl.pallas_call(
        _ffn1_kernel,
        grid_spec=pltpu.PrefetchScalarGridSpec(
            num_scalar_prefetch=2, grid=(hidden // tn, r // tm),
            in_specs=[pl.BlockSpec((tm, d), lambda j, i, be, nu: (i, 0)), wspec, wspec],
            out_specs=pl.BlockSpec((tm, tn), lambda j, i, be, nu: (i, j)),
            scratch_shapes=[pltpu.VMEM((d, 2 * tn), BF16)]),
        out_shape=jax.ShapeDtypeStruct((r, hidden), BF16),
        compiler_params=_params("parallel", "arbitrary"), name="ffn_up",
    )(block_expert, n_used, x, w_gate, w_up)


def _ffn2_res_kernel(be_ref, nu_ref, h_ref, w_ref, x_ref, gt_ref, o_ref):
    def body():
        y = jnp.dot(h_ref[...], w_ref[...], preferred_element_type=F32)
        o_ref[...] = x_ref[...] + gt_ref[...] * y

    _when_block_used(nu_ref, o_ref, body)


def _ffn2_scale_kernel(be_ref, nu_ref, h_ref, w_ref, sw_ref, o_ref):
    def body():
        y = jnp.dot(h_ref[...], w_ref[...], preferred_element_type=F32)
        o_ref[...] = (y * sw_ref[...]).astype(o_ref.dtype)

    _when_block_used(nu_ref, o_ref, body)


def ffn_down_residual(h, w_down, block_expert, n_used, x, gate, *, tm, tn=1024):
    r, hidden = h.shape
    d = w_down.shape[2]
    return pl.pallas_call(
        _ffn2_res_kernel,
        grid_spec=pltpu.PrefetchScalarGridSpec(
            num_scalar_prefetch=2, grid=(d // tn, r // tm),
            in_specs=[pl.BlockSpec((tm, hidden), lambda j, i, be, nu: (i, 0)),
                      pl.BlockSpec((None, hidden, tn), lambda j, i, be, nu: (be[i], 0, j)),
                      pl.BlockSpec((tm, tn), lambda j, i, be, nu: (i, j)),
                      pl.BlockSpec((1, tn), lambda j, i, be, nu: (0, j))],
            out_specs=pl.BlockSpec((tm, tn), lambda j, i, be, nu: (i, j))),
        out_shape=jax.ShapeDtypeStruct((r, d), F32),
        compiler_params=_params("parallel", "arbitrary"), name="ffn_down_residual",
    )(block_expert, n_used, h, w_down, x, gate.reshape(1, d))


def ffn_down_scaled(h, w_down, block_expert, n_used, slot_w, *, tn=1024):
    r, hidden = h.shape
    d = w_down.shape[2]
    tm = min(MOE_ROWS, r)
    return pl.pallas_call(
        _ffn2_scale_kernel,
        grid_spec=pltpu.PrefetchScalarGridSpec(
            num_scalar_prefetch=2, grid=(d // tn, r // tm),
            in_specs=[pl.BlockSpec((tm, hidden), lambda j, i, be, nu: (i, 0)),
                      pl.BlockSpec((None, hidden, tn), lambda j, i, be, nu: (be[i], 0, j)),
                      pl.BlockSpec((tm, 1), lambda j, i, be, nu: (i, 0))],
            out_specs=pl.BlockSpec((tm, tn), lambda j, i, be, nu: (i, j))),
        out_shape=jax.ShapeDtypeStruct((r, d), BF16),
        compiler_params=_params("parallel", "arbitrary"), name="ffn_down_scaled",
    )(block_expert, n_used, h, w_down, slot_w.reshape(r, 1))


def _router_kernel(x_ref, g_ref, sh_ref, sc_ref, rw_ref, rb_ref, h_ref, idx_ref, wt_ref, rank_ref, cnt_ref):
    x = x_ref[...]
    y = x * lax.rsqrt(jnp.mean(x * x, axis=-1, keepdims=True) + EPS)
    h = (y * g_ref[...]) * (1.0 + sc_ref[...]) + sh_ref[...]
    hb = h.astype(BF16)
    h_ref[...] = hb
    logits = jnp.dot(hb, rw_ref[...], preferred_element_type=F32) + rb_ref[...]
    lane = lax.broadcasted_iota(I32, logits.shape, 1)
    neg = jnp.float32(-jnp.inf)
    logits = jnp.where(lane < N_EXPERTS, logits, neg)
    m0 = jnp.max(logits, axis=-1, keepdims=True)
    i0 = jnp.min(jnp.where(logits == m0, lane, LANES), axis=-1, keepdims=True)
    rest = jnp.where(lane == i0, neg, logits)
    m1 = jnp.max(rest, axis=-1, keepdims=True)
    i1 = jnp.min(jnp.where(rest == m1, lane, LANES), axis=-1, keepdims=True)
    e1 = jnp.exp(m1 - m0)
    w0 = 1.0 / (1.0 + e1)
    idx_ref[...] = jnp.where(lane == 0, i0, jnp.where(lane == 1, i1, 0))
    wt_ref[...] = jnp.where(lane == 0, w0, jnp.where(lane == 1, e1 * w0, 0.0))
    tm = x.shape[0]
    hot = (lane == i0) | (lane == i1)
    earlier = (lax.broadcasted_iota(I32, (tm, tm), 1) < lax.broadcasted_iota(I32, (tm, tm), 0)).astype(BF16)
    before = jnp.dot(earlier, hot.astype(BF16), preferred_element_type=F32)
    r0 = jnp.sum(jnp.where(lane == i0, before, 0.0), axis=-1, keepdims=True)
    r1 = jnp.sum(jnp.where(lane == i1, before, 0.0), axis=-1, keepdims=True)
    rank_ref[...] = jnp.where(lane == 0, r0, jnp.where(lane == 1, r1, 0.0)).astype(I32)
    cnt_ref[...] = jnp.broadcast_to(jnp.sum(hot.astype(F32), axis=0, keepdims=True), cnt_ref.shape).astype(I32)


def route(x, g, sh, sc, router_w, router_b):
    t, d = x.shape
    tm = min(ROW_TILE, t)
    vec = pl.BlockSpec((1, d), lambda i: (0, 0))
    row = lambda w: pl.BlockSpec((tm, w), lambda i: (i, 0))
    rw = jnp.zeros((d, LANES), BF16).at[:, :N_EXPERTS].set(router_w.astype(BF16))
    rb = jnp.zeros((1, LANES), F32).at[0, :N_EXPERTS].set(router_b)
    h, idx, wt, rank, cnt = pl.pallas_call(
        _router_kernel, grid=(t // tm,),
        in_specs=[row(d), vec, vec, vec, pl.BlockSpec((d, LANES), lambda i: (0, 0)),
                  pl.BlockSpec((1, LANES), lambda i: (0, 0))],
        out_specs=[row(d), row(LANES), row(LANES), row(LANES), pl.BlockSpec((8, LANES), lambda i: (i, 0))],
        out_shape=[jax.ShapeDtypeStruct((t, d), BF16), jax.ShapeDtypeStruct((t, LANES), I32),
                   jax.ShapeDtypeStruct((t, LANES), F32), jax.ShapeDtypeStruct((t, LANES), I32),
                   jax.ShapeDtypeStruct((t // tm * 8, LANES), I32)],
        compiler_params=_params("parallel"), name="route",
    )(x, g.reshape(1, d), sh.reshape(1, d), sc.reshape(1, d), rw, rb)
    return h, idx[:, :2], wt[:, :2], rank[:, :2], cnt.reshape(t // tm, 8, LANES)[:, 0, :N_EXPERTS]


def _gather_kernel(pb_ref, pt_ref, pf_ref, h_ref, dest_ref, wt_ref, o_ref, sw_ref, acc_ref, swacc_ref):
    i = pl.program_id(0)
    flags = pf_ref[i]

    @pl.when((flags & 1) != 0)
    def _():
        acc_ref[...] = jnp.zeros_like(acc_ref)
        swacc_ref[...] = jnp.zeros_like(swacc_ref)

    @pl.when((flags & 4) != 0)
    def _():
        slot = pb_ref[i] * MOE_ROWS + lax.broadcasted_iota(I32, (MOE_ROWS, 1), 0)
        hit0 = dest_ref[0:1, :] == slot
        hit1 = dest_ref[1:2, :] == slot
        acc_ref[...] += jnp.dot((hit0 | hit1).astype(BF16), h_ref[...], preferred_element_type=F32)
        swacc_ref[...] += jnp.sum(jnp.where(hit0, wt_ref[0:1, :], 0.0) + jnp.where(hit1, wt_ref[1:2, :], 0.0),
                                  axis=1, keepdims=True)

    @pl.when((flags & 2) != 0)
    def _():
        o_ref[...] = acc_ref[...].astype(o_ref.dtype)
        sw_ref[...] = swacc_ref[...]


def moe_gather(h, dest_t, wt_t, n_slots, pair_block, pair_tile, pair_flags):
    t, d = h.shape
    tm = min(ROW_TILE, t)
    n_pairs = pair_block.shape[0]
    tok = pl.BlockSpec((2, tm), lambda i, pb, pt, pf: (0, pt[i]))
    return pl.pallas_call(
        _gather_kernel,
        grid_spec=pltpu.PrefetchScalarGridSpec(
            num_scalar_prefetch=3, grid=(n_pairs,),
            in_specs=[pl.BlockSpec((tm, d), lambda i, pb, pt, pf: (pt[i], 0)), tok, tok],
            out_specs=[pl.BlockSpec((MOE_ROWS, d), lambda i, pb, pt, pf: (pb[i], 0)),
                       pl.BlockSpec((MOE_ROWS, 1), lambda i, pb, pt, pf: (pb[i], 0))],
            scratch_shapes=[pltpu.VMEM((MOE_ROWS, d), F32), pltpu.VMEM((MOE_ROWS, 1), F32)]),
        out_shape=[jax.ShapeDtypeStruct((n_slots, d), BF16), jax.ShapeDtypeStruct((n_slots, 1), F32)],
        compiler_params=_params("arbitrary"), name="moe_gather",
    )(pair_block, pair_tile, pair_flags, h, dest_t, wt_t)


def _combine_kernel(pb_ref, pt_ref, pf_ref, yb_ref, dest_ref, x_ref, gt_ref, ng_ref, o_ref, acc_ref):
    i = pl.program_id(0)
    flags = pf_ref[i]

    @pl.when((flags & 1) != 0)
    def _():
        acc_ref[...] = jnp.zeros_like(acc_ref)

    @pl.when((flags & 4) != 0)
    def _():
        slot = pb_ref[i] * MOE_ROWS + lax.broadcasted_iota(I32, (dest_ref.shape[0], MOE_ROWS), 1)
        hit = (dest_ref[:, 0:1] == slot) | (dest_ref[:, 1:2] == slot)
        acc_ref[...] += jnp.dot(hit.astype(BF16), yb_ref[...], preferred_element_type=F32)

    @pl.when((flags & 2) != 0)
    def _():
        y = x_ref[...] + gt_ref[...] * acc_ref[...]
        o_ref[...] = y * lax.rsqrt(jnp.mean(y * y, axis=-1, keepdims=True) + EPS) * ng_ref[...]


def moe_combine_norm(yb, dest, pair_block, pair_tile, pair_flags, x, gate, norm_g):
    t, d = x.shape
    tm = min(ROW_TILE, t)
    n_pairs = pair_block.shape[0]
    return pl.pallas_call(
        _combine_kernel,
        grid_spec=pltpu.PrefetchScalarGridSpec(
            num_scalar_prefetch=3, grid=(n_pairs,),
            in_specs=[pl.BlockSpec((MOE_ROWS, d), lambda i, pb, pt, pf: (pb[i], 0)),
                      pl.BlockSpec((tm, 2), lambda i, pb, pt, pf: (pt[i], 0)),
                      pl.BlockSpec((tm, d), lambda i, pb, pt, pf: (pt[i], 0)),
                      pl.BlockSpec((1, d), lambda i, pb, pt, pf: (0, 0)),
                      pl.BlockSpec((1, d), lambda i, pb, pt, pf: (0, 0))],
            out_specs=pl.BlockSpec((tm, d), lambda i, pb, pt, pf: (pt[i], 0)),
            scratch_shapes=[pltpu.VMEM((tm, d), F32)]),
        out_shape=jax.ShapeDtypeStruct((t, d), F32),
        compiler_params=_params("arbitrary"), name="moe_combine",
    )(pair_block, pair_tile, pair_flags, yb, dest, x, gate.reshape(1, d), norm_g.reshape(1, d))


def _pair_lists(lo, hi, nonempty, n_pairs):
    cnt = jnp.where(nonempty, hi - lo + 1, 1)
    end = jnp.cumsum(cnt)
    start = end - cnt
    i = jnp.arange(n_pairs, dtype=I32)
    ic = jnp.minimum(i, end[-1] - 1)
    grp = jnp.sum((end[None, :] <= ic[:, None]).astype(I32), axis=1)
    off = ic - start[grp]
    member = jnp.where(nonempty[grp], lo[grp] + off, 0).astype(I32)
    return grp, member, off == 0, off == cnt[grp] - 1, nonempty[grp], i < end[-1]


def _pair_flags(first, last, data, valid):
    flags = jnp.where(first, 1, 0) | jnp.where(last, 2, 0) | jnp.where(data, 4, 0)
    return jnp.where(valid, flags, 0).astype(I32)


def moe_plan(top_idx, rank, tile_cnt, n_tok):
    tm = min(ROW_TILE, n_tok)
    n_tiles = n_tok // tm
    n_blocks = 2 * n_tok // MOE_ROWS + N_EXPERTS
    experts = jnp.arange(N_EXPERTS, dtype=I32)
    tile_off = jnp.cumsum(tile_cnt, axis=0) - tile_cnt
    counts = jnp.sum(tile_cnt, axis=0)
    padded = (counts + MOE_ROWS - 1) // MOE_ROWS * MOE_ROWS
    pad_end = jnp.cumsum(padded)
    base = (pad_end - padded)[None, :] + tile_off
    hot = top_idx[:, :, None] == experts[None, None, :]
    dest = (jnp.sum(jnp.where(hot, jnp.repeat(base, tm, axis=0)[:, None, :], 0), axis=-1) + rank).astype(I32)
    blk = jnp.arange(n_blocks, dtype=I32)
    block_expert = jnp.minimum(jnp.sum((pad_end[None, :] <= (blk * MOE_ROWS)[:, None]).astype(I32), axis=1),
                               N_EXPERTS - 1)
    n_pairs = N_EXPERTS * n_tiles + n_blocks
    c_grp, c_block, first, last, data, valid = _pair_lists(
        (base // MOE_ROWS).reshape(-1), ((base + tile_cnt - 1) // MOE_ROWS).reshape(-1), (tile_cnt > 0).reshape(-1),
        n_pairs)
    c_flags = _pair_flags(first & (c_grp % N_EXPERTS == 0), last & (c_grp % N_EXPERTS == N_EXPERTS - 1), data, valid)
    mine = block_expert[:, None, None] == experts[None, None, :]
    base_b = jnp.sum(jnp.where(mine, base[None], 0), axis=-1)
    cnt_b = jnp.sum(jnp.where(mine, tile_cnt[None], 0), axis=-1)
    sends = (cnt_b > 0) & (base_b + cnt_b > (blk * MOE_ROWS)[:, None]) & (base_b < ((blk + 1) * MOE_ROWS)[:, None])
    tiles = jnp.arange(n_tiles, dtype=I32)[None, :]
    g_block, g_tile, *g_bits = _pair_lists(jnp.min(jnp.where(sends, tiles, n_tiles), axis=1),
                                           jnp.max(jnp.where(sends, tiles, -1), axis=1), jnp.any(sends, axis=1), n_pairs)
    return dict(dest=dest, block_expert=block_expert.astype(I32), n_slots=n_blocks * MOE_ROWS,
                n_used=(pad_end[-1:] // MOE_ROWS).astype(I32),
                gather=(g_block, g_tile, _pair_flags(*g_bits)), combine=(c_block, c_grp // N_EXPERTS, c_flags))


def _mods(mod_all, layer, row, d):
    m = mod_all[layer, row]
    return tuple(m[j * d:(j + 1) * d] for j in range(N_MOD))


def _even_layer(x_lat, x_ctx, mods_lat, mods_ctx, norm1_g, norm2_g, w):
    d = x_lat.shape[1]
    hw = DN_HEADS * DN_DH
    s_dn = jnp.zeros((2, DN_HEADS, DN_DH, DN_DH), F32)
    s_lru = jnp.zeros((2, 8, hw), F32)
    outs = []
    for x, mods in ((x_ctx, mods_ctx), (x_lat, mods_lat)):
        sh1, sc1, gt1, sh2, sc2, gt2 = mods
        p, pg = norm_proj(x, norm1_g, sh1, sc1, w["w_main"], w["w_gate"])
        q, k, v, xc, gx, gxt = even_prep(p, pg, w["conv_w"], w["conv_b"], w["a_log"], w["dt_bias"])
        u, wm, qd, kd, qk, gl = deltanet_prep(q, k, v, gx, gxt)
        o_f, o_b, s_dn = deltanet_scan(u, wm, qd, kd, qk, gl, s_dn)
        h_f, h_b, s_lru = lru_scan(xc, w["lru_wa"], w["lru_wx"], w["lru_ba"], w["lru_bx"], w["lru_lam"], s_lru)
        x = even_finish(o_f, o_b, h_f, h_b, p, w["dn_norm_g"], w["w_out"], x, gt1)
        h2 = norm_mod(x, norm2_g, sh2, sc2)
        tm = _dense_rows(h2.shape[0])
        be = jnp.zeros((h2.shape[0] // tm,), I32)
        every = jnp.full((1,), h2.shape[0] // tm, I32)
        hh = ffn_up(h2, w["ffn_gate"], w["ffn_up"], be, every, tm=tm)
        x = ffn_down_residual(hh, w["ffn_down"], be, every, x, gt2, tm=tm)
        outs.append(x)
    return outs[1], outs[0]


def _odd_layer_last(x_lat, x_ctx, mods_lat, mods_ctx, norm1_g, norm2_g, final_g, w):
    dk, dv = w["dk"], w["dv"]
    states = jnp.zeros((2, GLA_HEADS, dv, dk), F32)
    sh1, sc1 = mods_ctx[0], mods_ctx[1]
    p, pg = norm_proj(x_ctx, norm1_g, sh1, sc1, w["w_main"], w["w_gate"])
    _, _, states = gla_scan(p, pg, w["wg_pad"], w["bg"], states, dk=dk, dv=dv)
    sh1, sc1, gt1, sh2, sc2, gt2 = mods_lat
    p, pg = norm_proj(x_lat, norm1_g, sh1, sc1, w["w_main"], w["w_gate"], raster=True)
    o_f, o_b, _ = gla_scan(p, pg, w["wg_pad"], w["bg"], states, dk=dk, dv=dv)
    x = odd_finish(o_f, o_b, p, w["gla_norm_g"], w["w_out"], x_lat, gt1)
    n_tok = x.shape[0]
    h2, top_idx, top_w, rank, tile_cnt = route(x, norm2_g, sh2, sc2, w["router_w"], w["router_b"])
    plan = moe_plan(top_idx, rank, tile_cnt, n_tok)
    xb, slot_w = moe_gather(h2, plan["dest"].T, top_w.T, plan["n_slots"], *plan["gather"])
    hh = ffn_up(xb, w["exp_gate"], w["exp_up"], plan["block_expert"], plan["n_used"], tm=MOE_ROWS)
    yb = ffn_down_scaled(hh, w["exp_down"], plan["block_expert"], plan["n_used"], slot_w)
    return moe_combine_norm(yb, plan["dest"], *plan["combine"], x, gt2, final_g)


def kernel(x, c, ctx, c_ctx, mod_w, mod_b, norm1_g, norm2_g, ev_w_in, ev_conv_qkv, ev_dn_a_log, ev_dn_dt_bias,
           ev_dn_norm_g, ev_lru_conv_w, ev_lru_conv_b, ev_lru_wa, ev_lru_ba, ev_lru_wx, ev_lru_bx, ev_lru_lambda,
           ev_w_out, ev_ffn_w_gate, ev_ffn_w_up, ev_ffn_w_down, od_w_in, od_gla_wg2, od_gla_bg, od_gla_norm_g,
           od_w_out, od_router_w, od_router_b, od_exp_w_gate, od_exp_w_up, od_exp_w_down, final_norm_g):
    b_, length, d = x.shape
    assert b_ == 1 and mod_w.shape[0] == 2, "this kernel implements the batch-1, depth-2 configuration"
    hw = DN_HEADS * DN_DH
    x_lat, x_ctx = x[0], ctx[0]

    cond8 = jnp.zeros((8, d), F32).at[0].set(c[0]).at[1].set(c_ctx)
    mod_all = adaln_all(cond8, mod_w, mod_b)

    w_in = ev_w_in[0]
    qkv_w, z0, ab0, xr0, gr0 = 3 * hw, 3 * hw, 4 * hw, 4 * hw + 4 * DN_HEADS, 5 * hw + 4 * DN_HEADS
    w_main = jnp.concatenate([w_in[:, :qkv_w], w_in[:, xr0:xr0 + hw], w_in[:, z0:z0 + hw], w_in[:, gr0:gr0 + hw]],
                             axis=1).astype(BF16)
    w_gate = jnp.zeros((d, LANES), BF16).at[:, :4 * DN_HEADS].set(w_in[:, ab0:ab0 + 4 * DN_HEADS].astype(BF16))
    pad16 = lambda t: jnp.zeros((1, LANES), F32).at[0, :2 * DN_HEADS].set(t.reshape(-1))
    ev = dict(
        w_main=w_main, w_gate=w_gate,
        conv_w=jnp.concatenate([ev_conv_qkv[0], ev_lru_conv_w[0]], axis=1),
        conv_b=jnp.concatenate([jnp.zeros((qkv_w,), F32), ev_lru_conv_b[0]]).reshape(1, -1),
        a_log=pad16(ev_dn_a_log[0]), dt_bias=pad16(ev_dn_dt_bias[0]),
        dn_norm_g=ev_dn_norm_g[0],
        lru_wa=ev_lru_wa[0].astype(BF16), lru_wx=ev_lru_wx[0].astype(BF16),
        lru_ba=ev_lru_ba[0].reshape(2, 1, hw), lru_bx=ev_lru_bx[0].reshape(2, 1, hw),
        lru_lam=ev_lru_lambda[0].reshape(2, 1, hw),
        w_out=ev_w_out[0].astype(BF16),
        ffn_gate=ev_ffn_w_gate.astype(BF16), ffn_up=ev_ffn_w_up.astype(BF16), ffn_down=ev_ffn_w_down.astype(BF16),
    )
    x_lat, x_ctx = _even_layer(x_lat, x_ctx, _mods(mod_all, 0, 0, d), _mods(mod_all, 0, 1, d),
                               norm1_g[0], norm2_g[0], ev)

    w_in = od_w_in[0]
    qk_w = od_gla_wg2.shape[-1]
    v_w = od_w_out.shape[1]
    main_w = 2 * qk_w + 2 * v_w
    wg_pad = jnp.zeros((2, LANES, qk_w), F32)
    for dirn in range(2):
        wg_pad = wg_pad.at[dirn, dirn * GLA_RANK:(dirn + 1) * GLA_RANK].set(od_gla_wg2[0, dirn])
    od = dict(
        w_main=w_in[:, :main_w].astype(BF16),
        w_gate=jnp.zeros((d, LANES), BF16).at[:, :2 * GLA_RANK].set(w_in[:, main_w:].astype(BF16)),
        wg_pad=wg_pad, bg=od_gla_bg[0].reshape(2, 1, qk_w), gla_norm_g=od_gla_norm_g[0],
        w_out=od_w_out[0].astype(BF16), router_w=od_router_w[0], router_b=od_router_b[0],
        exp_gate=od_exp_w_gate[0].astype(BF16), exp_up=od_exp_w_up[0].astype(BF16),
        exp_down=od_exp_w_down[0].astype(BF16),
        dk=qk_w // GLA_HEADS, dv=v_w // GLA_HEADS,
    )
    return _odd_layer_last(x_lat, x_ctx, _mods(mod_all, 1, 0, d), _mods(mod_all, 1, 1, d),
                           norm1_g[1], norm2_g[1], final_norm_g, od)[None]
```

```python
import functools
import math

import jax
import jax.numpy as jnp
from jax import lax
from jax.experimental import pallas as pl
from jax.experimental.pallas import tpu as pltpu

F32 = jnp.float32
BF16 = jnp.bfloat16
I32 = jnp.int32
HI = lax.Precision.HIGHEST

EPS = 1e-6
N_MOD = 6
GRID_W = 64
CHUNK = 64
SUB = 8
LANES = 128
ROW_TILE = 256
DN_HEADS = 8
DN_DH = 128
LRU_BLOCKS = 8
LRU_BW = 128
LRU_C = 8.0
LRU_UNROLL = 4
GLA_HEADS = 4
GLA_RANK = 16
GLA_TAU = 16.0
GLA_SAFE_STEP = 7.5
N_EXPERTS = 8
MOE_ROWS = 256
VMEM_LIMIT = 56 * 1024 * 1024

NT_DIMS = (((1,), (1,)), ((), ()))
TN_DIMS = (((0,), (0,)), ((), ()))


def _params(*sem):
    return pltpu.CompilerParams(dimension_semantics=sem, vmem_limit_bytes=VMEM_LIMIT)


def _softplus(x):
    return jnp.maximum(x, 0.0) + jnp.log1p(jnp.exp(-jnp.abs(x)))


def _silu(x):
    return x * jax.nn.sigmoid(x)


def _bdot(a, b):
    return jnp.dot(a.astype(BF16), b.astype(BF16), preferred_element_type=F32)


def _adaln_kernel(cond_ref, w_ref, b_ref, o_ref):
    s = _silu(cond_ref[...])
    o_ref[...] = jnp.dot(s, w_ref[...], precision=HI, preferred_element_type=F32) + b_ref[...]


def adaln_all(cond8, mod_w, mod_b):
    n_layers, d, n6 = mod_w.shape
    tn = 1024
    return pl.pallas_call(
        _adaln_kernel,
        grid=(n_layers, n6 // tn),
        in_specs=[pl.BlockSpec((8, d), lambda l, j: (0, 0)),
                  pl.BlockSpec((None, d, tn), lambda l, j: (l, 0, j)),
                  pl.BlockSpec((None, 1, tn), lambda l, j: (l, 0, j))],
        out_specs=pl.BlockSpec((None, 8, tn), lambda l, j: (l, 0, j)),
        out_shape=jax.ShapeDtypeStruct((n_layers, 8, n6), F32),
        compiler_params=_params("arbitrary", "arbitrary"),
        name="adaln",
    )(cond8, mod_w, mod_b.reshape(n_layers, 1, n6))


def _norm_mod_kernel(x_ref, g_ref, sh_ref, sc_ref, o_ref):
    x = x_ref[...]
    y = x * lax.rsqrt(jnp.mean(x * x, axis=-1, keepdims=True) + EPS)
    o_ref[...] = ((y * g_ref[...]) * (1.0 + sc_ref[...]) + sh_ref[...]).astype(o_ref.dtype)


def _raster_spec(rows, d):
    return pl.BlockSpec((rows, d), lambda c: (0, c))


def norm_mod(x, g, sh, sc):
    n, d = x.shape
    tm = min(ROW_TILE, n)
    vec = pl.BlockSpec((1, d), lambda i: (0, 0))
    row = pl.BlockSpec((tm, d), lambda i: (i, 0))
    return pl.pallas_call(
        _norm_mod_kernel, grid=(n // tm,), in_specs=[row, vec, vec, vec], out_specs=row,
        out_shape=jax.ShapeDtypeStruct((n, d), BF16),
        compiler_params=_params("parallel"), name="norm_mod",
    )(x, g.reshape(1, d), sh.reshape(1, d), sc.reshape(1, d))


def _dense_rows(r):
    return 2 * ROW_TILE if r % (2 * ROW_TILE) == 0 else min(ROW_TILE, r)


def _norm_proj_kernel(x_ref, g_ref, sh_ref, sc_ref, w_ref, wg_ref, p_ref, pg_ref, h_s, *, d):
    @pl.when(pl.program_id(1) == 0)
    def _():
        rows = x_ref.shape[0]
        for col in range(x_ref.shape[1] // d):
            x = x_ref[:, col * d:(col + 1) * d]
            y = x * lax.rsqrt(jnp.mean(x * x, axis=-1, keepdims=True) + EPS)
            h_s[col * rows:(col + 1) * rows, :] = ((y * g_ref[...]) * (1.0 + sc_ref[...]) + sh_ref[...]).astype(BF16)
        pg_ref[...] = jnp.dot(h_s[...], wg_ref[...], preferred_element_type=F32)

    p_ref[...] = jnp.dot(h_s[...], w_ref[...], preferred_element_type=F32)


def norm_proj(x, g, sh, sc, w_main, w_gate, *, raster=False, tn=1024):
    n, d = x.shape
    n_main = w_main.shape[1]
    tm = next(t for t in (4 * ROW_TILE, 2 * ROW_TILE, min(ROW_TILE, n)) if n % t == 0)
    if raster:
        rows = n // GRID_W
        n_col = max(tm // rows, 1)
        tm = n_col * rows
        x_in, x_spec = x.reshape(rows, GRID_W * d), pl.BlockSpec((rows, n_col * d), lambda i, j: (0, i))
    else:
        x_in, x_spec = x, pl.BlockSpec((tm, d), lambda i, j: (i, 0))
    vec = pl.BlockSpec((1, d), lambda i, j: (0, 0))
    return pl.pallas_call(
        functools.partial(_norm_proj_kernel, d=d), grid=(n // tm, n_main // tn),
        in_specs=[x_spec, vec, vec, vec, pl.BlockSpec((d, tn), lambda i, j: (0, j)),
                  pl.BlockSpec((d, LANES), lambda i, j: (0, 0))],
        out_specs=[pl.BlockSpec((tm, tn), lambda i, j: (i, j)), pl.BlockSpec((tm, LANES), lambda i, j: (i, 0))],
        out_shape=[jax.ShapeDtypeStruct((n, n_main), F32), jax.ShapeDtypeStruct((n, LANES), F32)],
        scratch_shapes=[pltpu.VMEM((tm, d), BF16)],
        compiler_params=_params("parallel", "arbitrary"), name="norm_proj",
    )(x_in, g.reshape(1, d), sh.reshape(1, d), sc.reshape(1, d), w_main, w_gate)


def _evprep_kernel(p_ref, prev_ref, next_ref, pg_ref, cw_ref, cb_ref, alog_ref, dtb_ref,
                   q_ref, k_ref, v_ref, xc_ref, gx_ref, gxt_ref, ext_ref, *, n_tiles):
    i = pl.program_id(0)
    tm, width = p_ref.shape
    ext_ref[8:8 + tm, :] = p_ref[...]
    ext_ref[0:8, :] = jnp.where(i > 0, prev_ref[...], 0.0)
    ext_ref[8 + tm:16 + tm, :] = jnp.where(i < n_tiles - 1, next_ref[...], 0.0)
    acc = ext_ref[pl.ds(6, tm), :] * cw_ref[0:1, :]
    for j in range(1, 4):
        acc = acc + ext_ref[pl.ds(6 + j, tm), :] * cw_ref[j:j + 1, :]
    acc = acc + cb_ref[...]
    qk_w = DN_HEADS * DN_DH
    for hd in range(DN_HEADS):
        for part, ref, scale in ((0, q_ref, DN_DH ** -0.5), (1, k_ref, 1.0)):
            lo = part * qk_w + hd * DN_DH
            t = _silu(acc[:, lo:lo + DN_DH])
            t = t * lax.rsqrt(jnp.sum(t * t, axis=-1, keepdims=True) + EPS)
            ref[:, hd * DN_DH:(hd + 1) * DN_DH] = t * scale
    v_ref[...] = _silu(acc[:, 2 * qk_w:3 * qk_w])
    xc_ref[...] = acc[:, 3 * qk_w:]
    pg = pg_ref[...]
    lane = lax.broadcasted_iota(I32, pg.shape, 1)
    g = -jnp.exp(alog_ref[...]) * _softplus(pg + dtb_ref[...])
    ri = lax.broadcasted_iota(I32, (tm, tm), 0)
    ci = lax.broadcasted_iota(I32, (tm, tm), 1)
    same = (ri // CHUNK) == (ci // CHUNK)
    cum_f = _dot01((same & (ci <= ri)).astype(BF16), g)
    cum_b = _dot01((same & (ci >= ri)).astype(BF16), g)
    tot = _dot01(same.astype(BF16), g)
    gc = jnp.where(lane < DN_HEADS, cum_f, cum_b)
    gx = jnp.where(lane < 2 * DN_HEADS, gc,
                   jnp.where(lane < 4 * DN_HEADS, jax.nn.sigmoid(pg), pltpu.roll(tot, 4 * DN_HEADS, 1)))
    gx_ref[...] = gx
    gxt_ref[...] = gx.T


def _dot01(m01, x):
    x1 = x.astype(BF16)
    r1 = x - x1.astype(F32)
    x2 = r1.astype(BF16)
    x3 = (r1 - x2.astype(F32)).astype(BF16)
    dot = lambda t: jnp.dot(m01, t, preferred_element_type=F32)
    return dot(x1) + dot(x2) + dot(x3)


def even_prep(p, pg, conv_w, conv_b, a_log, dt_bias):
    r = p.shape[0]
    width = conv_w.shape[1]
    tm = min(ROW_TILE, r)
    n_tiles = r // tm
    hb = tm // 8
    out_w = DN_HEADS * DN_DH
    row = lambda w: pl.BlockSpec((tm, w), lambda i: (i, 0))
    vec = lambda w: pl.BlockSpec((1, w), lambda i: (0, 0))
    return pl.pallas_call(
        functools.partial(_evprep_kernel, n_tiles=n_tiles),
        grid=(n_tiles,),
        in_specs=[row(width),
                  pl.BlockSpec((8, width), lambda i: (jnp.maximum(i * hb - 1, 0), 0)),
                  pl.BlockSpec((8, width), lambda i: (jnp.minimum((i + 1) * hb, r // 8 - 1), 0)),
                  row(LANES),
                  pl.BlockSpec((4, width), lambda i: (0, 0)), vec(width), vec(LANES), vec(LANES)],
        out_specs=[row(out_w), row(out_w), row(out_w), row(out_w), row(LANES),
                   pl.BlockSpec((LANES, tm), lambda i: (0, i))],
        out_shape=[jax.ShapeDtypeStruct((r, out_w), F32)] * 4
        + [jax.ShapeDtypeStruct((r, LANES), F32), jax.ShapeDtypeStruct((LANES, r), F32)],
        scratch_shapes=[pltpu.VMEM((tm + 16, width), F32)],
        compiler_params=_params("parallel"), name="even_prep",
    )(p, p, p, pg, conv_w, conv_b, a_log, dt_bias)


DN_BASE = 16
DN_HEADS_PER_STEP = 4


def _dnprep_kernel(q_ref, k_ref, v_ref, gx_ref, gxt_ref, u_ref, w_ref, qd_ref, kd_ref, qk_ref, gl_ref, *, n_chunks):
    c = CHUNK
    tm = q_ref.shape[0]
    ri = lax.broadcasted_iota(I32, (tm, tm), 0)
    ci = lax.broadcasted_iota(I32, (tm, tm), 1)
    same = lambda s: (ri // s) == (ci // s)
    eye = (ri == ci).astype(F32)
    lane = lax.broadcasted_iota(I32, (tm, LANES), 1)
    gx = gx_ref[...]
    pick = lambda idx: jnp.sum(jnp.where(lane == idx, gx, 0.0), axis=1, keepdims=True)
    chains = []
    for hh in range(DN_HEADS_PER_STEP):
        hd = pl.program_id(1) * DN_HEADS_PER_STEP + hh
        cols = slice(hh * DN_DH, (hh + 1) * DN_DH)
        q = q_ref[:, cols]
        k = k_ref[:, cols]
        kb16 = k.astype(BF16)
        gram_k = lax.dot_general(kb16, kb16, NT_DIMS, preferred_element_type=F32)
        gram_q = lax.dot_general(q.astype(BF16), kb16, NT_DIMS, preferred_element_type=F32)
        for d in range(2):
            incl = same(c) & ((ci >= ri) if d else (ci <= ri))
            gcol = pick(d * DN_HEADS + hd)
            bcol = pick((2 + d) * DN_HEADS + hd)
            tcol = pick((4 + d) * DN_HEADS + hd)
            grow = gxt_ref[pl.ds(d * DN_HEADS + hd, 1), :]
            decay = jnp.where(incl, jnp.exp(gcol - grow), 0.0)
            a = jnp.where(ri == ci, 0.0, gram_k * bcol * decay)
            eg = jnp.exp(gcol)
            qk = (gram_q * decay).astype(BF16)
            for n in range(n_chunks):
                qk_ref[d, hh, n * c:(n + 1) * c, :] = qk[n * c:(n + 1) * c, n * c:(n + 1) * c]
                gl_ref[d, hh, n] = jnp.broadcast_to(jnp.exp(tcol[n * c:n * c + 1, :]), (8, LANES))
            qd_ref[d, :, cols] = (q * eg).astype(BF16)
            kd_ref[d, :, cols] = (k * jnp.exp(tcol - gcol)).astype(BF16)
            diag = jnp.where(same(DN_BASE), a, 0.0)
            chains.append(dict(d=d, cols=cols, a=a, t=eye - diag, p=diag, scale=bcol, scale_k=bcol * eg))
    size = 2
    while size < DN_BASE:
        for ch in chains:
            ch["p"] = _bdot(ch["p"], ch["p"])
        for ch in chains:
            ch["t"] = ch["t"] + _bdot(ch["t"], ch["p"])
        size *= 2
    size = DN_BASE
    while size < c:
        couple = same(2 * size) & ~same(size)
        for ch in chains:
            ch["et"] = _bdot(jnp.where(couple, ch["a"], 0.0), ch["t"])
        for ch in chains:
            ch["t"] = ch["t"] - _bdot(ch["t"], ch["et"])
        size *= 2
    for ch in chains:
        d, cols = ch["d"], ch["cols"]
        x = _bdot(ch["t"], jnp.concatenate([v_ref[:, cols] * ch["scale"], k_ref[:, cols] * ch["scale_k"]], axis=1))
        u_ref[d, :, cols] = x[:, :DN_DH]
        w_ref[d, :, cols] = x[:, DN_DH:].astype(BF16)


def deltanet_prep(q, k, v, gx, gxt):
    r = q.shape[0]
    tm = min(ROW_TILE, r)
    n_chunks = tm // CHUNK
    hw = DN_HEADS * DN_DH
    hps = DN_HEADS_PER_STEP
    head = pl.BlockSpec((tm, hps * DN_DH), lambda i, h: (i, h))
    dhead = pl.BlockSpec((2, tm, hps * DN_DH), lambda i, h: (0, i, h))
    return pl.pallas_call(
        functools.partial(_dnprep_kernel, n_chunks=n_chunks),
        grid=(r // tm, DN_HEADS // hps),
        in_specs=[head, head, head, pl.BlockSpec((tm, LANES), lambda i, h: (i, 0)),
                  pl.BlockSpec((LANES, tm), lambda i, h: (0, i))],
        out_specs=[dhead, dhead, dhead, dhead,
                   pl.BlockSpec((2, hps, tm, CHUNK), lambda i, h: (0, h, i, 0)),
                   pl.BlockSpec((2, hps, n_chunks, 8, LANES), lambda i, h: (0, h, i, 0, 0))],
        out_shape=[jax.ShapeDtypeStruct((2, r, hw), F32),
                   jax.ShapeDtypeStruct((2, r, hw), BF16),
                   jax.ShapeDtypeStruct((2, r, hw), BF16),
                   jax.ShapeDtypeStruct((2, r, hw), BF16),
                   jax.ShapeDtypeStruct((2, DN_HEADS, r, CHUNK), BF16),
                   jax.ShapeDtypeStruct((2, DN_HEADS, r // CHUNK, 8, LANES), F32)],
        compiler_params=_params("parallel", "parallel"), name="deltanet_prep",
    )(q, k, v, gx, gxt)


def _dnscan_kernel(*refs, n_steps):
    ins = (refs[0:6], refs[6:12])
    s0_ref, o_refs, sf_ref, s_ref = refs[12], refs[13:15], refs[15], refs[16]
    step = pl.program_id(0)

    @pl.when(step == 0)
    def _():
        s_ref[...] = s0_ref[...]

    c = CHUNK
    n_sub = ins[0][0].shape[0] // c
    col = lambda hd: slice(hd * DN_DH, (hd + 1) * DN_DH)
    dot = lambda a, b: jnp.dot(a, b, preferred_element_type=F32)
    for t in range(n_sub):
        chains = []
        for d in range(2):
            u_ref, w_ref, qd_ref, kd_ref, qk_ref, gl_ref = ins[d]
            n = n_sub - 1 - t if d else t
            rows = slice(n * c, (n + 1) * c)
            for hd in range(DN_HEADS):
                chains.append(dict(d=d, hd=hd, rows=rows, u=u_ref.at[rows, col(hd)], w=w_ref.at[rows, col(hd)],
                                   qd=qd_ref.at[rows, col(hd)], kd=kd_ref.at[rows, col(hd)],
                                   qk=qk_ref.at[hd, rows, :], gl=gl_ref.at[hd, n, 0:1, :]))
        for ch in chains:
            ch["s"] = s_ref[ch["d"], ch["hd"]]
            ch["sb"] = ch["s"].astype(BF16)
        for ch in chains:
            ch["ws"] = dot(ch["w"][...], ch["sb"])
        for ch in chains:
            ch["qs"] = dot(ch["qd"][...], ch["sb"])
        for ch in chains:
            ch["vb"] = (ch["u"][...] - ch["ws"]).astype(BF16)
        for ch in chains:
            o_refs[ch["d"]][ch["rows"], col(ch["hd"])] = ch["qs"] + dot(ch["qk"][...], ch["vb"])
        for ch in chains:
            ch["ds"] = lax.dot_general(ch["kd"][...], ch["vb"], TN_DIMS, preferred_element_type=F32)
        for ch in chains:
            s_ref[ch["d"], ch["hd"]] = ch["s"] * ch["gl"][...] + ch["ds"]

    @pl.when(step == n_steps - 1)
    def _():
        sf_ref[...] = s_ref[...]


def deltanet_scan(u, w, qd, kd, qk, gl, s0):
    r = u.shape[1]
    hw = DN_HEADS * DN_DH
    tm = min(ROW_TILE, r)
    n_steps = r // tm
    state = pl.BlockSpec((2, DN_HEADS, DN_DH, DN_DH), lambda i: (0, 0, 0, 0))
    in_specs, out_specs = [], []
    for d in range(2):
        idx = (lambda i: n_steps - 1 - i) if d else (lambda i: i)
        big = pl.BlockSpec((None, tm, hw), lambda i, d=d, idx=idx: (d, idx(i), 0))
        in_specs += [big, big, big, big,
                     pl.BlockSpec((None, DN_HEADS, tm, CHUNK), lambda i, d=d, idx=idx: (d, 0, idx(i), 0)),
                     pl.BlockSpec((None, DN_HEADS, tm // CHUNK, 8, LANES), lambda i, d=d, idx=idx: (d, 0, idx(i), 0, 0))]
        out_specs.append(pl.BlockSpec((tm, hw), lambda i, idx=idx: (idx(i), 0)))
    return pl.pallas_call(
        functools.partial(_dnscan_kernel, n_steps=n_steps),
        grid=(n_steps,),
        in_specs=in_specs + [state],
        out_specs=out_specs + [state],
        out_shape=[jax.ShapeDtypeStruct((r, hw), F32), jax.ShapeDtypeStruct((r, hw), F32),
                   jax.ShapeDtypeStruct((2, DN_HEADS, DN_DH, DN_DH), F32)],
        scratch_shapes=[pltpu.VMEM((2, DN_HEADS, DN_DH, DN_DH), F32)],
        compiler_params=_params("arbitrary"), name="deltanet_scan",
    )(u, w, qd, kd, qk, gl, u, w, qd, kd, qk, gl, s0)


def _lru_kernel(xf_ref, xb_ref, wa_ref, wx_ref, ba_ref, bx_ref, lam_ref, h0_ref, hf_ref, hb_ref, hl_ref,
                a_s, b_s, carry_s, *, n_steps):
    step = pl.program_id(0)

    @pl.when(step == 0)
    def _():
        carry_s[...] = h0_ref[...]

    tm = xf_ref.shape[0]
    for d, xc_ref in enumerate((xf_ref, xb_ref)):
        sp = _softplus(-lam_ref[d])
        for n in range(LRU_BLOCKS):
            cols = slice(n * LRU_BW, (n + 1) * LRU_BW)
            xb = xc_ref[:, cols]
            xbb = xb.astype(BF16)
            r = jax.nn.sigmoid(jnp.dot(xbb, wa_ref[d, n], preferred_element_type=F32) + ba_ref[d, :, cols])
            gi = jax.nn.sigmoid(jnp.dot(xbb, wx_ref[d, n], preferred_element_type=F32) + bx_ref[d, :, cols])
            log_a = -LRU_C * r * sp[:, cols]
            a = jnp.exp(log_a)
            a_s[d, :, cols] = a
            b_s[d, :, cols] = jnp.sqrt(-jnp.tanh(log_a) * (a * a + 1.0)) * (gi * xb)

    rid = lax.broadcasted_iota(I32, (8, a_s.shape[2]), 0)
    n_groups = tm // 8
    h_refs = (hf_ref, hb_ref)

    def group(gidx, carries):
        out = []
        for d in range(2):
            g = (n_groups - 1 - gidx) if d else gidx
            base = pl.multiple_of(g * 8, 8)
            a = a_s[d, pl.ds(base, 8), :]
            b = b_s[d, pl.ds(base, 8), :]
            for sh in (1, 2, 4):
                if d:
                    keep = rid < 8 - sh
                    a_n = jnp.where(keep, pltpu.roll(a, 8 - sh, 0), 1.0)
                    b_n = jnp.where(keep, pltpu.roll(b, 8 - sh, 0), 0.0)
                else:
                    keep = rid >= sh
                    a_n = jnp.where(keep, pltpu.roll(a, sh, 0), 1.0)
                    b_n = jnp.where(keep, pltpu.roll(b, sh, 0), 0.0)
                b = a * b_n + b
                a = a * a_n
            h = a * carries[d] + b
            h_refs[d][pl.ds(base, 8), :] = h
            edge = h[0:1, :] if d else h[7:8, :]
            out.append(jnp.broadcast_to(edge, h.shape))
        return tuple(out)

    carries = lax.fori_loop(0, n_groups, group, (carry_s[0], carry_s[1]), unroll=LRU_UNROLL)
    carry_s[0] = carries[0]
    carry_s[1] = carries[1]

    @pl.when(step == n_steps - 1)
    def _():
        hl_ref[...] = carry_s[...]


def lru_scan(xc, wa, wx, ba, bx, lam, h0):
    r, width = xc.shape
    tm = min(ROW_TILE, r)
    n_steps = r // tm
    vec = pl.BlockSpec((2, 1, width), lambda i: (0, 0, 0))
    wspec = pl.BlockSpec((2, LRU_BLOCKS, LRU_BW, LRU_BW), lambda i: (0, 0, 0, 0))
    st = pl.BlockSpec((2, 8, width), lambda i: (0, 0, 0))
    fwd = pl.BlockSpec((tm, width), lambda i: (i, 0))
    bwd = pl.BlockSpec((tm, width), lambda i: (n_steps - 1 - i, 0))
    return pl.pallas_call(
        functools.partial(_lru_kernel, n_steps=n_steps),
        grid=(n_steps,),
        in_specs=[fwd, bwd, wspec, wspec, vec, vec, vec, st],
        out_specs=[fwd, bwd, st],
        out_shape=[jax.ShapeDtypeStruct((r, width), F32), jax.ShapeDtypeStruct((r, width), F32),
                   jax.ShapeDtypeStruct((2, 8, width), F32)],
        scratch_shapes=[pltpu.VMEM((2, tm, width), F32), pltpu.VMEM((2, tm, width), F32),
                        pltpu.VMEM((2, 8, width), F32)],
        compiler_params=_params("arbitrary"), name="lru_scan",
    )(xc, xc, wa, wx, ba, bx, lam, h0)


def _gelu_tanh(x):
    return 0.5 * x * (1.0 + jnp.tanh(math.sqrt(2.0 / math.pi) * (x + 0.044715 * (x * x * x))))


def _evfin_kernel(of_ref, ob_ref, hf_ref, hb_ref, zg_ref, ng_ref, wout_ref, x_ref, gt_ref, o_ref, mix_s):
    hw = DN_HEADS * DN_DH
    for hd in range(DN_HEADS):
        cols = slice(hd * DN_DH, (hd + 1) * DN_DH)
        o = of_ref[:, cols] + ob_ref[:, cols]
        y = o * lax.rsqrt(jnp.mean(o * o, axis=-1, keepdims=True) + EPS) * ng_ref[...]
        mix_s[:, cols] = (y * _silu(zg_ref[:, cols])).astype(BF16)
    mix_s[:, hw:] = ((hf_ref[...] + hb_ref[...]) * _gelu_tanh(zg_ref[:, hw:])).astype(BF16)
    y = jnp.dot(mix_s[...], wout_ref[...], preferred_element_type=F32)
    o_ref[...] = x_ref[...] + gt_ref[...] * y


def even_finish(o_f, o_b, h_f, h_b, p, norm_g, w_out, x, gate):
    r, d = x.shape
    tm = min(ROW_TILE, r)
    hw = DN_HEADS * DN_DH
    row = lambda w: pl.BlockSpec((tm, w), lambda i: (i, 0))
    return pl.pallas_call(
        _evfin_kernel, grid=(r // tm,),
        in_specs=[row(hw), row(hw), row(hw), row(hw),
                  pl.BlockSpec((tm, 2 * hw), lambda i: (i, 2)),
                  pl.BlockSpec((1, DN_DH), lambda i: (0, 0)),
                  pl.BlockSpec(w_out.shape, lambda i: (0, 0)),
                  row(d), pl.BlockSpec((1, d), lambda i: (0, 0))],
        out_specs=row(d),
        out_shape=jax.ShapeDtypeStruct((r, d), F32),
        scratch_shapes=[pltpu.VMEM((tm, 2 * hw), BF16)],
        compiler_params=_params("parallel"), name="even_finish",
    )(o_f, o_b, h_f, h_b, p, norm_g.reshape(1, DN_DH), w_out, x, gate.reshape(1, d))


def _odfin_kernel(of_ref, ob_ref, go_ref, ng_ref, wout_ref, x_ref, gt_ref, o_ref, mix_s, *, dv):
    for hd in range(GLA_HEADS):
        cols = slice(hd * dv, (hd + 1) * dv)
        o = of_ref[:, cols] + ob_ref[:, cols]
        y = o * lax.rsqrt(jnp.mean(o * o, axis=-1, keepdims=True) + EPS) * ng_ref[...]
        mix_s[:, cols] = (y * _silu(go_ref[:, cols])).astype(BF16)
    y = jnp.dot(mix_s[...], wout_ref[...], preferred_element_type=F32)
    o_ref[...] = x_ref[...] + gt_ref[...] * y


def odd_finish(o_f, o_b, p, norm_g, w_out, x, gate):
    r, d = x.shape
    rows = r // GRID_W
    vw = o_f.shape[1]
    dv = vw // GLA_HEADS
    row = lambda w: pl.BlockSpec((rows, w), lambda c: (c, 0))
    return pl.pallas_call(
        functools.partial(_odfin_kernel, dv=dv), grid=(GRID_W,),
        in_specs=[row(vw), row(vw),
                  pl.BlockSpec((rows, vw), lambda c: (c, 2)),
                  pl.BlockSpec((1, dv), lambda c: (0, 0)),
                  pl.BlockSpec(w_out.shape, lambda c: (0, 0)),
                  _raster_spec(rows, d), pl.BlockSpec((1, d), lambda c: (0, 0))],
        out_specs=_raster_spec(rows, d),
        out_shape=jax.ShapeDtypeStruct((rows, GRID_W * d), F32),
        scratch_shapes=[pltpu.VMEM((rows, vw), BF16)],
        compiler_params=_params("parallel"), name="odd_finish",
    )(o_f, o_b, p, norm_g.reshape(1, dv), w_out, x.reshape(rows, GRID_W * d), gate.reshape(1, d)).reshape(r, d)


def _gla_kernel(*refs, n_steps, dk):
    ins = (refs[0:6], refs[6:12])
    s0_ref, o_refs, sf_ref, s_ref, gc_s = refs[12], refs[13:15], refs[15], refs[16], refs[17]
    step = pl.program_id(1)

    @pl.when(step == 0)
    def _():
        s_ref[...] = s0_ref[...]

    c = CHUNK
    tm = ins[0][0].shape[0]
    n_chunks = tm // c
    n_sub = c // SUB
    ri = lax.broadcasted_iota(I32, (tm, tm), 0)
    ci = lax.broadcasted_iota(I32, (tm, tm), 1)
    r64 = lax.broadcasted_iota(I32, (c, c), 0)
    c64 = lax.broadcasted_iota(I32, (c, c), 1)
    rr = lax.broadcasted_iota(I32, (c, 1), 0) % SUB
    units = []
    g_min = None
    for d in range(2):
        q_ref, k_ref, v_ref, gd_ref, wg_ref, bg_ref = ins[d]
        tri = (((ri // c) == (ci // c)) & ((ci >= ri) if d else (ci <= ri))).astype(BF16)
        logit = _bdot(gd_ref[...], wg_ref[...]) + bg_ref[...]
        g = -_softplus(-logit) * (1.0 / GLA_TAU)
        gc = _dot01(tri, g)
        gc_s[d] = gc
        g_min = jnp.min(g) if g_min is None else jnp.minimum(g_min, jnp.min(g))
        q = q_ref[...] * dk ** -0.5
        for n in range(n_chunks):
            rows = slice(n * c, (n + 1) * c)
            units.append(dict(d=d, n=n, rows=rows, q=q[rows], k=k_ref[rows, :], g=g[rows], gc=gc[rows],
                              att=jnp.zeros((c, c), F32)))
    size = c // 2
    while size >= SUB:
        r_hi, c_hi = (r64 & size) != 0, (c64 & size) != 0
        same = (r64 // (2 * size)) == (c64 // (2 * size))
        pair = (same & r_hi & ~c_hi, same & ~r_hi & c_hi)
        for u in units:
            gc = u["gc"]
            pieces = []
            for b in range(c // (2 * size)):
                mid = b * 2 * size + size
                edge = gc[mid:mid + 1] if u["d"] else gc[mid - 1:mid]
                pieces.append(jnp.broadcast_to(edge, (2 * size, dk)))
            edge = pieces[0] if len(pieces) == 1 else jnp.concatenate(pieces, axis=0)
            qs = (u["q"] * jnp.exp(gc - edge)).astype(BF16)
            ks = (u["k"] * jnp.exp(edge - gc)).astype(BF16)
            u["att"] = u["att"] + jnp.where(pair[u["d"]],
                                            lax.dot_general(qs, ks, NT_DIMS, preferred_element_type=F32), 0.0)
        size //= 2

    def diag_direct():
        blocks = [jnp.zeros((c, c), F32) for _ in units]
        for jj in range(SUB):
            here = c64 == (r64 // SUB) * SUB + jj
            seen = (here & (rr >= jj), here & (rr <= jj))
            for i, u in enumerate(units):
                rep = lambda ref: jnp.concatenate(
                    [jnp.broadcast_to(ref[pl.ds(u["n"] * c + b * SUB + jj, 1), :], (SUB, dk)) for b in range(n_sub)],
                    axis=0)
                col = jnp.sum(u["q"] * rep(ins[u["d"]][1]) * jnp.exp(u["gc"] - rep(gc_s.at[u["d"]])),
                              axis=-1, keepdims=True)
                blocks[i] = jnp.where(seen[u["d"]], col, blocks[i])
        return tuple(blocks)

    def diag_factored():
        blocks = []
        same = (r64 // SUB) == (c64 // SUB)
        inside = (same & (c64 <= r64), same & (c64 >= r64))
        for u in units:
            before = (u["gc"] - u["g"]).reshape(n_sub, SUB, dk)
            edge = before[:, SUB - 1:SUB, :] if u["d"] else before[:, 0:1, :]
            edge = jnp.broadcast_to(edge, before.shape).reshape(c, dk)
            qs = (u["q"] * jnp.exp(u["gc"] - edge)).astype(BF16)
            ks = (u["k"] * jnp.exp(edge - u["gc"])).astype(BF16)
            blocks.append(jnp.where(inside[u["d"]],
                                    lax.dot_general(qs, ks, NT_DIMS, preferred_element_type=F32), 0.0))
        return tuple(blocks)

    diag = lax.cond(g_min >= -GLA_SAFE_STEP, diag_factored, diag_direct)
    for u in units:
        u["vb"] = ins[u["d"]][2][u["rows"], :].astype(BF16)
        u["tot"] = u["gc"][0:1] if u["d"] else u["gc"][c - 1:c]
    for i, u in enumerate(units):
        u["o"] = jnp.dot((u["att"] + diag[i]).astype(BF16), u["vb"], preferred_element_type=F32)
    for u in units:
        u["qd"] = (u["q"] * jnp.exp(u["gc"])).astype(BF16)
        u["ds"] = lax.dot_general(u["vb"], (u["k"] * jnp.exp(u["tot"] - u["gc"])).astype(BF16), TN_DIMS,
                                  preferred_element_type=F32)
    for t in range(n_chunks):
        for d in range(2):
            u = units[d * n_chunks + (n_chunks - 1 - t if d else t)]
            s = s_ref[d]
            o_refs[d][u["rows"], :] = u["o"] + lax.dot_general(u["qd"], s.astype(BF16), NT_DIMS,
                                                               preferred_element_type=F32)
            s_ref[d] = s * jnp.exp(u["tot"]) + u["ds"]

    @pl.when(step == n_steps - 1)
    def _():
        sf_ref[...] = s_ref[...]


def gla_scan(p, pg, wg_pad, bg, s0, *, dk, dv):
    r = p.shape[0]
    tm = min(ROW_TILE, r)
    n_steps = r // tm
    qk_blocks = GLA_HEADS
    v_block0 = 2 * GLA_HEADS * dk // dv
    state = pl.BlockSpec((2, None, dv, dk), lambda h, i: (0, h, 0, 0))
    in_specs, out_specs = [], []
    for d in range(2):
        idx = (lambda i: n_steps - 1 - i) if d else (lambda i: i)
        in_specs += [pl.BlockSpec((tm, dk), lambda h, i, idx=idx: (idx(i), h)),
                     pl.BlockSpec((tm, dk), lambda h, i, idx=idx: (idx(i), qk_blocks + h)),
                     pl.BlockSpec((tm, dv), lambda h, i, idx=idx: (idx(i), v_block0 + h)),
                     pl.BlockSpec((tm, LANES), lambda h, i, idx=idx: (idx(i), 0)),
                     pl.BlockSpec((None, LANES, dk), lambda h, i, d=d: (d, 0, h)),
                     pl.BlockSpec((None, 1, dk), lambda h, i, d=d: (d, 0, h))]
        out_specs.append(pl.BlockSpec((tm, dv), lambda h, i, idx=idx: (idx(i), h)))
    args = (p, p, p, pg, wg_pad, bg)
    return pl.pallas_call(
        functools.partial(_gla_kernel, n_steps=n_steps, dk=dk),
        grid=(GLA_HEADS, n_steps),
        in_specs=in_specs + [state],
        out_specs=out_specs + [state],
        out_shape=[jax.ShapeDtypeStruct((r, GLA_HEADS * dv), F32), jax.ShapeDtypeStruct((r, GLA_HEADS * dv), F32),
                   jax.ShapeDtypeStruct((2, GLA_HEADS, dv, dk), F32)],
        scratch_shapes=[pltpu.VMEM((2, dv, dk), F32), pltpu.VMEM((2, tm, dk), F32)],
        compiler_params=_params("parallel", "arbitrary"), name="gla_scan",
    )(*args, *args, s0)


def _when_block_used(nu_ref, o_ref, body):
    used = pl.program_id(1) < nu_ref[0]

    @pl.when(used)
    def _():
        body()

    @pl.when(jnp.logical_not(used))
    def _():
        o_ref[...] = jnp.zeros_like(o_ref)


FFN_TN = 1408


def _ffn1_kernel(be_ref, nu_ref, x_ref, wg_ref, wu_ref, o_ref, w_s):
    i = pl.program_id(1)
    tn = o_ref.shape[1]

    @pl.when((i == 0) | (be_ref[i] != be_ref[jnp.maximum(i - 1, 0)]))
    def _():
        w_s[:, :tn] = wg_ref[...]
        w_s[:, tn:] = wu_ref[...]

    def body():
        ab = jnp.dot(x_ref[...], w_s[...], preferred_element_type=F32)
        o_ref[...] = (_silu(ab[:, :tn]) * ab[:, tn:]).astype(o_ref.dtype)

    _when_block_used(nu_ref, o_ref, body)


def ffn_up(x, w_gate, w_up, block_expert, n_used, *, tm):
    r, d = x.shape
    hidden = w_gate.shape[2]
    tn = FFN_TN
    wspec = pl.BlockSpec((None, d, tn), lambda j, i, be, nu: (be[i], 0, j))
    return pl.pallas_call(
        _ffn1_kernel,
        grid_spec=pltpu.PrefetchScalarGridSpec(
            num_scalar_prefetch=2, grid=(hidden // tn, r // tm),
            in_specs=[pl.BlockSpec((tm, d), lambda j, i, be, nu: (i, 0)), wspec, wspec],
            out_specs=pl.BlockSpec((tm, tn), lambda j, i, be, nu: (i, j)),
            scratch_shapes=[pltpu.VMEM((d, 2 * tn), BF16)]),
        out_shape=jax.ShapeDtypeStruct((r, hidden), BF16),
        compiler_params=_params("parallel", "arbitrary"), name="ffn_up",
    )(block_expert, n_used, x, w_gate, w_up)


def _ffn2_res_kernel(be_ref, nu_ref, h_ref, w_ref, x_ref, gt_ref, o_ref):
    def body():
        y = jnp.dot(h_ref[...], w_ref[...], preferred_element_type=F32)
        o_ref[...] = x_ref[...] + gt_ref[...] * y

    _when_block_used(nu_ref, o_ref, body)


def _ffn2_scale_kernel(be_ref, nu_ref, h_ref, w_ref, sw_ref, o_ref):
    def body():
        y = jnp.dot(h_ref[...], w_ref[...], preferred_element_type=F32)
        o_ref[...] = (y * sw_ref[...]).astype(o_ref.dtype)

    _when_block_used(nu_ref, o_ref, body)


def ffn_down_residual(h, w_down, block_expert, n_used, x, gate, *, tm, tn=1024):
    r, hidden = h.shape
    d = w_down.shape[2]
    return pl.pallas_call(
        _ffn2_res_kernel,
        grid_spec=pltpu.PrefetchScalarGridSpec(
            num_scalar_prefetch=2, grid=(d // tn, r // tm),
            in_specs=[pl.BlockSpec((tm, hidden), lambda j, i, be, nu: (i, 0)),
                      pl.BlockSpec((None, hidden, tn), lambda j, i, be, nu: (be[i], 0, j)),
                      pl.BlockSpec((tm, tn), lambda j, i, be, nu: (i, j)),
                      pl.BlockSpec((1, tn), lambda j, i, be, nu: (0, j))],
            out_specs=pl.BlockSpec((tm, tn), lambda j, i, be, nu: (i, j))),
        out_shape=jax.ShapeDtypeStruct((r, d), F32),
        compiler_params=_params("parallel", "arbitrary"), name="ffn_down_residual",
    )(block_expert, n_used, h, w_down, x, gate.reshape(1, d))


def ffn_down_scaled(h, w_down, block_expert, n_used, slot_w, *, tn=1024):
    r, hidden = h.shape
    d = w_down.shape[2]
    tm = min(MOE_ROWS, r)
    return pl.pallas_call(
        _ffn2_scale_kernel,
        grid_spec=pltpu.PrefetchScalarGridSpec(
            num_scalar_prefetch=2, grid=(d // tn, r // tm),
            in_specs=[pl.BlockSpec((tm, hidden), lambda j, i, be, nu: (i, 0)),
                      pl.BlockSpec((None, hidden, tn), lambda j, i, be, nu: (be[i], 0, j)),
                      pl.BlockSpec((tm, 1), lambda j, i, be, nu: (i, 0))],
            out_specs=pl.BlockSpec((tm, tn), lambda j, i, be, nu: (i, j))),
        out_shape=jax.ShapeDtypeStruct((r, d), BF16),
        compiler_params=_params("parallel", "arbitrary"), name="ffn_down_scaled",
    )(block_expert, n_used, h, w_down, slot_w.reshape(r, 1))


def _router_kernel(x_ref, g_ref, sh_ref, sc_ref, rw_ref, rb_ref, h_ref, idx_ref, wt_ref, rank_ref, cnt_ref):
    x = x_ref[...]
    y = x * lax.rsqrt(jnp.mean(x * x, axis=-1, keepdims=True) + EPS)
    h = (y * g_ref[...]) * (1.0 + sc_ref[...]) + sh_ref[...]
    hb = h.astype(BF16)
    h_ref[...] = hb
    logits = jnp.dot(hb, rw_ref[...], preferred_element_type=F32) + rb_ref[...]
    lane = lax.broadcasted_iota(I32, logits.shape, 1)
    neg = jnp.float32(-jnp.inf)
    logits = jnp.where(lane < N_EXPERTS, logits, neg)
    m0 = jnp.max(logits, axis=-1, keepdims=True)
    i0 = jnp.min(jnp.where(logits == m0, lane, LANES), axis=-1, keepdims=True)
    rest = jnp.where(lane == i0, neg, logits)
    m1 = jnp.max(rest, axis=-1, keepdims=True)
    i1 = jnp.min(jnp.where(rest == m1, lane, LANES), axis=-1, keepdims=True)
    e1 = jnp.exp(m1 - m0)
    w0 = 1.0 / (1.0 + e1)
    idx_ref[...] = jnp.where(lane == 0, i0, jnp.where(lane == 1, i1, 0))
    wt_ref[...] = jnp.where(lane == 0, w0, jnp.where(lane == 1, e1 * w0, 0.0))
    tm = x.shape[0]
    hot = (lane == i0) | (lane == i1)
    earlier = (lax.broadcasted_iota(I32, (tm, tm), 1) < lax.broadcasted_iota(I32, (tm, tm), 0)).astype(BF16)
    before = jnp.dot(earlier, hot.astype(BF16), preferred_element_type=F32)
    r0 = jnp.sum(jnp.where(lane == i0, before, 0.0), axis=-1, keepdims=True)
    r1 = jnp.sum(jnp.where(lane == i1, before, 0.0), axis=-1, keepdims=True)
    rank_ref[...] = jnp.where(lane == 0, r0, jnp.where(lane == 1, r1, 0.0)).astype(I32)
    cnt_ref[...] = jnp.broadcast_to(jnp.sum(hot.astype(F32), axis=0, keepdims=True), cnt_ref.shape).astype(I32)


def route(x, g, sh, sc, router_w, router_b):
    t, d = x.shape
    tm = min(ROW_TILE, t)
    vec = pl.BlockSpec((1, d), lambda i: (0, 0))
    row = lambda w: pl.BlockSpec((tm, w), lambda i: (i, 0))
    rw = jnp.zeros((d, LANES), BF16).at[:, :N_EXPERTS].set(router_w.astype(BF16))
    rb = jnp.zeros((1, LANES), F32).at[0, :N_EXPERTS].set(router_b)
    h, idx, wt, rank, cnt = pl.pallas_call(
        _router_kernel, grid=(t // tm,),
        in_specs=[row(d), vec, vec, vec, pl.BlockSpec((d, LANES), lambda i: (0, 0)),
                  pl.BlockSpec((1, LANES), lambda i: (0, 0))],
        out_specs=[row(d), row(LANES), row(LANES), row(LANES), pl.BlockSpec((8, LANES), lambda i: (i, 0))],
        out_shape=[jax.ShapeDtypeStruct((t, d), BF16), jax.ShapeDtypeStruct((t, LANES), I32),
                   jax.ShapeDtypeStruct((t, LANES), F32), jax.ShapeDtypeStruct((t, LANES), I32),
                   jax.ShapeDtypeStruct((t // tm * 8, LANES), I32)],
        compiler_params=_params("parallel"), name="route",
    )(x, g.reshape(1, d), sh.reshape(1, d), sc.reshape(1, d), rw, rb)
    return h, idx[:, :2], wt[:, :2], rank[:, :2], cnt.reshape(t // tm, 8, LANES)[:, 0, :N_EXPERTS]


GATHER_WINDOW_TILES = 6


def _gather_kernel(pb_ref, ps_ref, pn_ref, pf_ref, h_ref, dest_ref, wt_ref, o_ref, sw_ref, acc_ref, swacc_ref):
    i = pl.program_id(0)
    flags = pf_ref[i]
    win = h_ref.shape[0]

    @pl.when((flags & 1) != 0)
    def _():
        acc_ref[...] = jnp.zeros_like(acc_ref)
        swacc_ref[...] = jnp.zeros_like(swacc_ref)

    @pl.when((flags & 4) != 0)
    def _():
        slot = pb_ref[i] * MOE_ROWS + lax.broadcasted_iota(I32, (MOE_ROWS, 1), 0)
        fresh = ps_ref[i] + lax.broadcasted_iota(I32, (1, win), 1) >= pn_ref[i]
        hit0 = (dest_ref[0:1, :] == slot) & fresh
        hit1 = (dest_ref[1:2, :] == slot) & fresh
        acc_ref[...] += jnp.dot((hit0 | hit1).astype(BF16), h_ref[...], preferred_element_type=F32)
        swacc_ref[...] += jnp.sum(jnp.where(hit0, wt_ref[0:1, :], 0.0) + jnp.where(hit1, wt_ref[1:2, :], 0.0),
                                  axis=1, keepdims=True)

    @pl.when((flags & 2) != 0)
    def _():
        o_ref[...] = acc_ref[...].astype(o_ref.dtype)
        sw_ref[...] = swacc_ref[...]


def moe_gather(h, dest_t, wt_t, n_slots, win, pair_block, pair_start, pair_new, pair_flags):
    t, d = h.shape
    n_pairs = pair_block.shape[0]
    start = lambda ps, i: pl.multiple_of(ps[i], ROW_TILE)
    tok = pl.BlockSpec((pl.Element(2), pl.Element(win)), lambda i, pb, ps, pn, pf: (0, start(ps, i)))
    return pl.pallas_call(
        _gather_kernel,
        grid_spec=pltpu.PrefetchScalarGridSpec(
            num_scalar_prefetch=4, grid=(n_pairs,),
            in_specs=[pl.BlockSpec((pl.Element(win), pl.Element(d)),
                                   lambda i, pb, ps, pn, pf: (start(ps, i), 0)), tok, tok],
            out_specs=[pl.BlockSpec((MOE_ROWS, d), lambda i, pb, ps, pn, pf: (pb[i], 0)),
                       pl.BlockSpec((MOE_ROWS, 1), lambda i, pb, ps, pn, pf: (pb[i], 0))],
            scratch_shapes=[pltpu.VMEM((MOE_ROWS, d), F32), pltpu.VMEM((MOE_ROWS, 1), F32)]),
        out_shape=[jax.ShapeDtypeStruct((n_slots, d), BF16), jax.ShapeDtypeStruct((n_slots, 1), F32)],
        compiler_params=_params("arbitrary"), name="moe_gather",
    )(pair_block, pair_start, pair_new, pair_flags, h, dest_t, wt_t)


def _combine_kernel(pb_ref, pt_ref, pf_ref, yb_ref, dest_ref, x_ref, gt_ref, ng_ref, o_ref, acc_ref):
    i = pl.program_id(0)
    flags = pf_ref[i]

    @pl.when((flags & 1) != 0)
    def _():
        acc_ref[...] = jnp.zeros_like(acc_ref)

    @pl.when((flags & 4) != 0)
    def _():
        slot = pb_ref[i] * MOE_ROWS + lax.broadcasted_iota(I32, (dest_ref.shape[0], MOE_ROWS), 1)
        hit = (dest_ref[:, 0:1] == slot) | (dest_ref[:, 1:2] == slot)
        acc_ref[...] += jnp.dot(hit.astype(BF16), yb_ref[...], preferred_element_type=F32)

    @pl.when((flags & 2) != 0)
    def _():
        y = x_ref[...] + gt_ref[...] * acc_ref[...]
        o_ref[...] = y * lax.rsqrt(jnp.mean(y * y, axis=-1, keepdims=True) + EPS) * ng_ref[...]


def moe_combine_norm(yb, dest, pair_block, pair_tile, pair_flags, x, gate, norm_g):
    t, d = x.shape
    tm = min(ROW_TILE, t)
    n_pairs = pair_block.shape[0]
    return pl.pallas_call(
        _combine_kernel,
        grid_spec=pltpu.PrefetchScalarGridSpec(
            num_scalar_prefetch=3, grid=(n_pairs,),
            in_specs=[pl.BlockSpec((MOE_ROWS, d), lambda i, pb, pt, pf: (pb[i], 0)),
                      pl.BlockSpec((tm, 2), lambda i, pb, pt, pf: (pt[i], 0)),
                      pl.BlockSpec((tm, d), lambda i, pb, pt, pf: (pt[i], 0)),
                      pl.BlockSpec((1, d), lambda i, pb, pt, pf: (0, 0)),
                      pl.BlockSpec((1, d), lambda i, pb, pt, pf: (0, 0))],
            out_specs=pl.BlockSpec((tm, d), lambda i, pb, pt, pf: (pt[i], 0)),
            scratch_shapes=[pltpu.VMEM((tm, d), F32)]),
        out_shape=jax.ShapeDtypeStruct((t, d), F32),
        compiler_params=_params("arbitrary"), name="moe_combine",
    )(pair_block, pair_tile, pair_flags, yb, dest, x, gate.reshape(1, d), norm_g.reshape(1, d))


def _pair_lists(lo, hi, nonempty, n_pairs):
    cnt = jnp.where(nonempty, hi - lo + 1, 1)
    end = jnp.cumsum(cnt)
    start = end - cnt
    i = jnp.arange(n_pairs, dtype=I32)
    ic = jnp.minimum(i, end[-1] - 1)
    grp = jnp.sum((end[None, :] <= ic[:, None]).astype(I32), axis=1)
    off = ic - start[grp]
    member = jnp.where(nonempty[grp], lo[grp] + off, 0).astype(I32)
    return grp, member, off == 0, off == cnt[grp] - 1, nonempty[grp], i < end[-1]


def _pair_flags(first, last, data, valid):
    flags = jnp.where(first, 1, 0) | jnp.where(last, 2, 0) | jnp.where(data, 4, 0)
    return jnp.where(valid, flags, 0).astype(I32)


def moe_plan(top_idx, rank, tile_cnt, n_tok):
    tm = min(ROW_TILE, n_tok)
    n_tiles = n_tok // tm
    n_blocks = 2 * n_tok // MOE_ROWS + N_EXPERTS
    experts = jnp.arange(N_EXPERTS, dtype=I32)
    tile_off = jnp.cumsum(tile_cnt, axis=0) - tile_cnt
    counts = jnp.sum(tile_cnt, axis=0)
    padded = (counts + MOE_ROWS - 1) // MOE_ROWS * MOE_ROWS
    pad_end = jnp.cumsum(padded)
    base = (pad_end - padded)[None, :] + tile_off
    hot = top_idx[:, :, None] == experts[None, None, :]
    dest = (jnp.sum(jnp.where(hot, jnp.repeat(base, tm, axis=0)[:, None, :], 0), axis=-1) + rank).astype(I32)
    blk = jnp.arange(n_blocks, dtype=I32)
    block_expert = jnp.minimum(jnp.sum((pad_end[None, :] <= (blk * MOE_ROWS)[:, None]).astype(I32), axis=1),
                               N_EXPERTS - 1)
    n_pairs = N_EXPERTS * n_tiles + n_blocks
    c_grp, c_block, first, last, data, valid = _pair_lists(
        (base // MOE_ROWS).reshape(-1), ((base + tile_cnt - 1) // MOE_ROWS).reshape(-1), (tile_cnt > 0).reshape(-1),
        n_pairs)
    c_flags = _pair_flags(first & (c_grp % N_EXPERTS == 0), last & (c_grp % N_EXPERTS == N_EXPERTS - 1), data, valid)
    mine = block_expert[:, None, None] == experts[None, None, :]
    base_b = jnp.sum(jnp.where(mine, base[None], 0), axis=-1)
    cnt_b = jnp.sum(jnp.where(mine, tile_cnt[None], 0), axis=-1)
    sends = (cnt_b > 0) & (base_b + cnt_b > (blk * MOE_ROWS)[:, None]) & (base_b < ((blk + 1) * MOE_ROWS)[:, None])
    tiles = jnp.arange(n_tiles, dtype=I32)[None, :]
    t_lo = jnp.min(jnp.where(sends, tiles, n_tiles), axis=1)
    t_hi = jnp.max(jnp.where(sends, tiles, -1), axis=1)
    wt = min(GATHER_WINDOW_TILES, n_tiles)
    n_windows = n_blocks + (N_EXPERTS * n_tiles + n_blocks + wt - 1) // wt
    g_block, g_win, *g_bits = _pair_lists(jnp.zeros_like(t_lo), (t_hi - t_lo) // wt, jnp.any(sends, axis=1),
                                          n_windows)
    g_new = (t_lo[g_block] + g_win * wt) * tm
    g_start = jnp.minimum(g_new, (n_tiles - wt) * tm)
    return dict(dest=dest, block_expert=block_expert.astype(I32), n_slots=n_blocks * MOE_ROWS,
                n_used=(pad_end[-1:] // MOE_ROWS).astype(I32), window=wt * tm,
                gather=(g_block, g_start.astype(I32), g_new.astype(I32), _pair_flags(*g_bits)),
                combine=(c_block, c_grp // N_EXPERTS, c_flags))


def _mods(mod_all, layer, row, d):
    m = mod_all[layer, row]
    return tuple(m[j * d:(j + 1) * d] for j in range(N_MOD))


def _even_layer(x_lat, x_ctx, mods_lat, mods_ctx, norm1_g, norm2_g, w):
    d = x_lat.shape[1]
    hw = DN_HEADS * DN_DH
    s_dn = jnp.zeros((2, DN_HEADS, DN_DH, DN_DH), F32)
    s_lru = jnp.zeros((2, 8, hw), F32)
    outs = []
    for x, mods in ((x_ctx, mods_ctx), (x_lat, mods_lat)):
        sh1, sc1, gt1, sh2, sc2, gt2 = mods
        p, pg = norm_proj(x, norm1_g, sh1, sc1, w["w_main"], w["w_gate"])
        q, k, v, xc, gx, gxt = even_prep(p, pg, w["conv_w"], w["conv_b"], w["a_log"], w["dt_bias"])
        u, wm, qd, kd, qk, gl = deltanet_prep(q, k, v, gx, gxt)
        o_f, o_b, s_dn = deltanet_scan(u, wm, qd, kd, qk, gl, s_dn)
        h_f, h_b, s_lru = lru_scan(xc, w["lru_wa"], w["lru_wx"], w["lru_ba"], w["lru_bx"], w["lru_lam"], s_lru)
        x = even_finish(o_f, o_b, h_f, h_b, p, w["dn_norm_g"], w["w_out"], x, gt1)
        h2 = norm_mod(x, norm2_g, sh2, sc2)
        tm = _dense_rows(h2.shape[0])
        be = jnp.zeros((h2.shape[0] // tm,), I32)
        every = jnp.full((1,), h2.shape[0] // tm, I32)
        hh = ffn_up(h2, w["ffn_gate"], w["ffn_up"], be, every, tm=tm)
        x = ffn_down_residual(hh, w["ffn_down"], be, every, x, gt2, tm=tm)
        outs.append(x)
    return outs[1], outs[0]


def _odd_layer_last(x_lat, x_ctx, mods_lat, mods_ctx, norm1_g, norm2_g, final_g, w):
    dk, dv = w["dk"], w["dv"]
    states = jnp.zeros((2, GLA_HEADS, dv, dk), F32)
    sh1, sc1 = mods_ctx[0], mods_ctx[1]
    p, pg = norm_proj(x_ctx, norm1_g, sh1, sc1, w["w_main"], w["w_gate"])
    _, _, states = gla_scan(p, pg, w["wg_pad"], w["bg"], states, dk=dk, dv=dv)
    sh1, sc1, gt1, sh2, sc2, gt2 = mods_lat
    p, pg = norm_proj(x_lat, norm1_g, sh1, sc1, w["w_main"], w["w_gate"], raster=True)
    o_f, o_b, _ = gla_scan(p, pg, w["wg_pad"], w["bg"], states, dk=dk, dv=dv)
    x = odd_finish(o_f, o_b, p, w["gla_norm_g"], w["w_out"], x_lat, gt1)
    n_tok = x.shape[0]
    h2, top_idx, top_w, rank, tile_cnt = route(x, norm2_g, sh2, sc2, w["router_w"], w["router_b"])
    plan = moe_plan(top_idx, rank, tile_cnt, n_tok)
    xb, slot_w = moe_gather(h2, plan["dest"].T, top_w.T, plan["n_slots"], plan["window"], *plan["gather"])
    hh = ffn_up(xb, w["exp_gate"], w["exp_up"], plan["block_expert"], plan["n_used"], tm=MOE_ROWS)
    yb = ffn_down_scaled(hh, w["exp_down"], plan["block_expert"], plan["n_used"], slot_w)
    return moe_combine_norm(yb, plan["dest"], *plan["combine"], x, gt2, final_g)


def kernel(x, c, ctx, c_ctx, mod_w, mod_b, norm1_g, norm2_g, ev_w_in, ev_conv_qkv, ev_dn_a_log, ev_dn_dt_bias,
           ev_dn_norm_g, ev_lru_conv_w, ev_lru_conv_b, ev_lru_wa, ev_lru_ba, ev_lru_wx, ev_lru_bx, ev_lru_lambda,
           ev_w_out, ev_ffn_w_gate, ev_ffn_w_up, ev_ffn_w_down, od_w_in, od_gla_wg2, od_gla_bg, od_gla_norm_g,
           od_w_out, od_router_w, od_router_b, od_exp_w_gate, od_exp_w_up, od_exp_w_down, final_norm_g):
    b_, length, d = x.shape
    assert b_ == 1 and mod_w.shape[0] == 2, "this kernel implements the batch-1, depth-2 configuration"
    hw = DN_HEADS * DN_DH
    x_lat, x_ctx = x[0], ctx[0]

    cond8 = jnp.zeros((8, d), F32).at[0].set(c[0]).at[1].set(c_ctx)
    mod_all = adaln_all(cond8, mod_w, mod_b)

    w_in = ev_w_in[0]
    qkv_w, z0, ab0, xr0, gr0 = 3 * hw, 3 * hw, 4 * hw, 4 * hw + 4 * DN_HEADS, 5 * hw + 4 * DN_HEADS
    w_main = jnp.concatenate([w_in[:, :qkv_w], w_in[:, xr0:xr0 + hw], w_in[:, z0:z0 + hw], w_in[:, gr0:gr0 + hw]],
                             axis=1).astype(BF16)
    w_gate = jnp.zeros((d, LANES), BF16).at[:, :4 * DN_HEADS].set(w_in[:, ab0:ab0 + 4 * DN_HEADS].astype(BF16))
    pad16 = lambda t: jnp.zeros((1, LANES), F32).at[0, :2 * DN_HEADS].set(t.reshape(-1))
    ev = dict(
        w_main=w_main, w_gate=w_gate,
        conv_w=jnp.concatenate([ev_conv_qkv[0], ev_lru_conv_w[0]], axis=1),
        conv_b=jnp.concatenate([jnp.zeros((qkv_w,), F32), ev_lru_conv_b[0]]).reshape(1, -1),
        a_log=pad16(ev_dn_a_log[0]), dt_bias=pad16(ev_dn_dt_bias[0]),
        dn_norm_g=ev_dn_norm_g[0],
        lru_wa=ev_lru_wa[0].astype(BF16), lru_wx=ev_lru_wx[0].astype(BF16),
        lru_ba=ev_lru_ba[0].reshape(2, 1, hw), lru_bx=ev_lru_bx[0].reshape(2, 1, hw),
        lru_lam=ev_lru_lambda[0].reshape(2, 1, hw),
        w_out=ev_w_out[0].astype(BF16),
        ffn_gate=ev_ffn_w_gate.astype(BF16), ffn_up=ev_ffn_w_up.astype(BF16), ffn_down=ev_ffn_w_down.astype(BF16),
    )
    x_lat, x_ctx = _even_layer(x_lat, x_ctx, _mods(mod_all, 0, 0, d), _mods(mod_all, 0, 1, d),
                               norm1_g[0], norm2_g[0], ev)

    w_in = od_w_in[0]
    qk_w = od_gla_wg2.shape[-1]
    v_w = od_w_out.shape[1]
    main_w = 2 * qk_w + 2 * v_w
    wg_pad = jnp.zeros((2, LANES, qk_w), F32)
    for dirn in range(2):
        wg_pad = wg_pad.at[dirn, dirn * GLA_RANK:(dirn + 1) * GLA_RANK].set(od_gla_wg2[0, dirn])
    od = dict(
        w_main=w_in[:, :main_w].astype(BF16),
        w_gate=jnp.zeros((d, LANES), BF16).at[:, :2 * GLA_RANK].set(w_in[:, main_w:].astype(BF16)),
        wg_pad=wg_pad, bg=od_gla_bg[0].reshape(2, 1, qk_w), gla_norm_g=od_gla_norm_g[0],
        w_out=od_w_out[0].astype(BF16), router_w=od_router_w[0], router_b=od_router_b[0],
        exp_gate=od_exp_w_gate[0].astype(BF16), exp_up=od_exp_w_up[0].astype(BF16),
        exp_down=od_exp_w_down[0].astype(BF16),
        dk=qk_w // GLA_HEADS, dv=v_w // GLA_HEADS,
    )
    return _odd_layer_last(x_lat, x_ctx, _mods(mod_all, 1, 0, d), _mods(mod_all, 1, 1, d),
                           norm1_g[1], norm2_g[1], final_norm_g, od)[None]
```

```python
import functools
import math

import jax
import jax.numpy as jnp
from jax import lax
from jax.experimental import pallas as pl
from jax.experimental.pallas import tpu as pltpu

F32 = jnp.float32
BF16 = jnp.bfloat16
I32 = jnp.int32
HI = lax.Precision.HIGHEST

EPS = 1e-6
N_MOD = 6
GRID_W = 64
CHUNK = 64
SUB = 8
LANES = 128
ROW_TILE = 256
DN_HEADS = 8
DN_DH = 128
LRU_BLOCKS = 8
LRU_BW = 128
LRU_C = 8.0
LRU_UNROLL = 4
GLA_HEADS = 4
GLA_RANK = 16
GLA_TAU = 16.0
GLA_SAFE_STEP = 7.5
N_EXPERTS = 8
MOE_ROWS = 256
VMEM_LIMIT = 56 * 1024 * 1024

NT_DIMS = (((1,), (1,)), ((), ()))
TN_DIMS = (((0,), (0,)), ((), ()))


def _params(*sem):
    return pltpu.CompilerParams(dimension_semantics=sem, vmem_limit_bytes=VMEM_LIMIT)


def _softplus(x):
    return jnp.maximum(x, 0.0) + jnp.log1p(jnp.exp(-jnp.abs(x)))


def _silu(x):
    return x * jax.nn.sigmoid(x)


def _bdot(a, b):
    return jnp.dot(a.astype(BF16), b.astype(BF16), preferred_element_type=F32)


def _adaln_kernel(cond_ref, w_ref, b_ref, o_ref):
    s = _silu(cond_ref[...])
    o_ref[...] = jnp.dot(s, w_ref[...], precision=HI, preferred_element_type=F32) + b_ref[...]


def adaln_all(cond8, mod_w, mod_b):
    n_layers, d, n6 = mod_w.shape
    tn = 1024
    return pl.pallas_call(
        _adaln_kernel,
        grid=(n_layers, n6 // tn),
        in_specs=[pl.BlockSpec((8, d), lambda l, j: (0, 0)),
                  pl.BlockSpec((None, d, tn), lambda l, j: (l, 0, j)),
                  pl.BlockSpec((None, 1, tn), lambda l, j: (l, 0, j))],
        out_specs=pl.BlockSpec((None, 8, tn), lambda l, j: (l, 0, j)),
        out_shape=jax.ShapeDtypeStruct((n_layers, 8, n6), F32),
        compiler_params=_params("arbitrary", "arbitrary"),
        name="adaln",
    )(cond8, mod_w, mod_b.reshape(n_layers, 1, n6))


def _norm_mod_kernel(x_ref, g_ref, sh_ref, sc_ref, o_ref):
    x = x_ref[...]
    y = x * lax.rsqrt(jnp.mean(x * x, axis=-1, keepdims=True) + EPS)
    o_ref[...] = ((y * g_ref[...]) * (1.0 + sc_ref[...]) + sh_ref[...]).astype(o_ref.dtype)


def _raster_spec(rows, d):
    return pl.BlockSpec((rows, d), lambda c: (0, c))


def norm_mod(x, g, sh, sc):
    n, d = x.shape
    tm = min(ROW_TILE, n)
    vec = pl.BlockSpec((1, d), lambda i: (0, 0))
    row = pl.BlockSpec((tm, d), lambda i: (i, 0))
    return pl.pallas_call(
        _norm_mod_kernel, grid=(n // tm,), in_specs=[row, vec, vec, vec], out_specs=row,
        out_shape=jax.ShapeDtypeStruct((n, d), BF16),
        compiler_params=_params("parallel"), name="norm_mod",
    )(x, g.reshape(1, d), sh.reshape(1, d), sc.reshape(1, d))


def _dense_rows(r):
    return 2 * ROW_TILE if r % (2 * ROW_TILE) == 0 else min(ROW_TILE, r)


def _norm_proj_kernel(x_ref, g_ref, sh_ref, sc_ref, w_ref, wg_ref, p_ref, pg_ref, h_s, *, d):
    @pl.when(pl.program_id(1) == 0)
    def _():
        rows = x_ref.shape[0]
        for col in range(x_ref.shape[1] // d):
            x = x_ref[:, col * d:(col + 1) * d]
            y = x * lax.rsqrt(jnp.mean(x * x, axis=-1, keepdims=True) + EPS)
            h_s[col * rows:(col + 1) * rows, :] = ((y * g_ref[...]) * (1.0 + sc_ref[...]) + sh_ref[...]).astype(BF16)
        pg_ref[...] = jnp.dot(h_s[...], wg_ref[...], preferred_element_type=F32)

    p_ref[...] = jnp.dot(h_s[...], w_ref[...], preferred_element_type=F32)


def norm_proj(x, g, sh, sc, w_main, w_gate, *, raster=False, tn=1024):
    n, d = x.shape
    n_main = w_main.shape[1]
    tm = next(t for t in (4 * ROW_TILE, 2 * ROW_TILE, min(ROW_TILE, n)) if n % t == 0)
    if raster:
        rows = n // GRID_W
        n_col = max(tm // rows, 1)
        tm = n_col * rows
        x_in, x_spec = x.reshape(rows, GRID_W * d), pl.BlockSpec((rows, n_col * d), lambda i, j: (0, i))
    else:
        x_in, x_spec = x, pl.BlockSpec((tm, d), lambda i, j: (i, 0))
    vec = pl.BlockSpec((1, d), lambda i, j: (0, 0))
    return pl.pallas_call(
        functools.partial(_norm_proj_kernel, d=d), grid=(n // tm, n_main // tn),
        in_specs=[x_spec, vec, vec, vec, pl.BlockSpec((d, tn), lambda i, j: (0, j)),
                  pl.BlockSpec((d, LANES), lambda i, j: (0, 0))],
        out_specs=[pl.BlockSpec((tm, tn), lambda i, j: (i, j)), pl.BlockSpec((tm, LANES), lambda i, j: (i, 0))],
        out_shape=[jax.ShapeDtypeStruct((n, n_main), F32), jax.ShapeDtypeStruct((n, LANES), F32)],
        scratch_shapes=[pltpu.VMEM((tm, d), BF16)],
        compiler_params=_params("parallel", "arbitrary"), name="norm_proj",
    )(x_in, g.reshape(1, d), sh.reshape(1, d), sc.reshape(1, d), w_main, w_gate)


def _evprep_kernel(p_ref, prev_ref, next_ref, pg_ref, cw_ref, cb_ref, alog_ref, dtb_ref,
                   q_ref, k_ref, v_ref, xc_ref, gx_ref, gxt_ref, ext_ref, *, n_tiles):
    i = pl.program_id(0)
    tm, width = p_ref.shape
    ext_ref[8:8 + tm, :] = p_ref[...]
    ext_ref[0:8, :] = jnp.where(i > 0, prev_ref[...], 0.0)
    ext_ref[8 + tm:16 + tm, :] = jnp.where(i < n_tiles - 1, next_ref[...], 0.0)
    acc = ext_ref[pl.ds(6, tm), :] * cw_ref[0:1, :]
    for j in range(1, 4):
        acc = acc + ext_ref[pl.ds(6 + j, tm), :] * cw_ref[j:j + 1, :]
    acc = acc + cb_ref[...]
    qk_w = DN_HEADS * DN_DH
    for hd in range(DN_HEADS):
        for part, ref, scale in ((0, q_ref, DN_DH ** -0.5), (1, k_ref, 1.0)):
            lo = part * qk_w + hd * DN_DH
            t = _silu(acc[:, lo:lo + DN_DH])
            t = t * lax.rsqrt(jnp.sum(t * t, axis=-1, keepdims=True) + EPS)
            ref[:, hd * DN_DH:(hd + 1) * DN_DH] = t * scale
    v_ref[...] = _silu(acc[:, 2 * qk_w:3 * qk_w])
    xc_ref[...] = acc[:, 3 * qk_w:]
    pg = pg_ref[...]
    lane = lax.broadcasted_iota(I32, pg.shape, 1)
    g = -jnp.exp(alog_ref[...]) * _softplus(pg + dtb_ref[...])
    ri = lax.broadcasted_iota(I32, (tm, tm), 0)
    ci = lax.broadcasted_iota(I32, (tm, tm), 1)
    same = (ri // CHUNK) == (ci // CHUNK)
    cum_f = _dot01((same & (ci <= ri)).astype(BF16), g)
    cum_b = _dot01((same & (ci >= ri)).astype(BF16), g)
    tot = _dot01(same.astype(BF16), g)
    gc = jnp.where(lane < DN_HEADS, cum_f, cum_b)
    gx = jnp.where(lane < 2 * DN_HEADS, gc,
                   jnp.where(lane < 4 * DN_HEADS, jax.nn.sigmoid(pg), pltpu.roll(tot, 4 * DN_HEADS, 1)))
    gx_ref[...] = gx
    gxt_ref[...] = gx.T


def _dot01(m01, x):
    x1 = x.astype(BF16)
    r1 = x - x1.astype(F32)
    x2 = r1.astype(BF16)
    x3 = (r1 - x2.astype(F32)).astype(BF16)
    dot = lambda t: jnp.dot(m01, t, preferred_element_type=F32)
    return dot(x1) + dot(x2) + dot(x3)


def even_prep(p, pg, conv_w, conv_b, a_log, dt_bias):
    r = p.shape[0]
    width = conv_w.shape[1]
    tm = min(ROW_TILE, r)
    n_tiles = r // tm
    hb = tm // 8
    out_w = DN_HEADS * DN_DH
    row = lambda w: pl.BlockSpec((tm, w), lambda i: (i, 0))
    vec = lambda w: pl.BlockSpec((1, w), lambda i: (0, 0))
    return pl.pallas_call(
        functools.partial(_evprep_kernel, n_tiles=n_tiles),
        grid=(n_tiles,),
        in_specs=[row(width),
                  pl.BlockSpec((8, width), lambda i: (jnp.maximum(i * hb - 1, 0), 0)),
                  pl.BlockSpec((8, width), lambda i: (jnp.minimum((i + 1) * hb, r // 8 - 1), 0)),
                  row(LANES),
                  pl.BlockSpec((4, width), lambda i: (0, 0)), vec(width), vec(LANES), vec(LANES)],
        out_specs=[row(out_w), row(out_w), row(out_w), row(out_w), row(LANES),
                   pl.BlockSpec((LANES, tm), lambda i: (0, i))],
        out_shape=[jax.ShapeDtypeStruct((r, out_w), F32)] * 4
        + [jax.ShapeDtypeStruct((r, LANES), F32), jax.ShapeDtypeStruct((LANES, r), F32)],
        scratch_shapes=[pltpu.VMEM((tm + 16, width), F32)],
        compiler_params=_params("parallel"), name="even_prep",
    )(p, p, p, pg, conv_w, conv_b, a_log, dt_bias)


DN_BASE = 16
DN_HEADS_PER_STEP = 4


def _dnprep_kernel(q_ref, k_ref, v_ref, gx_ref, gxt_ref, u_ref, w_ref, qd_ref, kd_ref, qk_ref, gl_ref, *, n_chunks):
    c = CHUNK
    tm = q_ref.shape[0]
    ri = lax.broadcasted_iota(I32, (tm, tm), 0)
    ci = lax.broadcasted_iota(I32, (tm, tm), 1)
    same = lambda s: (ri // s) == (ci // s)
    eye = (ri == ci).astype(F32)
    lane = lax.broadcasted_iota(I32, (tm, LANES), 1)
    gx = gx_ref[...]
    pick = lambda idx: jnp.sum(jnp.where(lane == idx, gx, 0.0), axis=1, keepdims=True)
    chains = []
    for hh in range(DN_HEADS_PER_STEP):
        hd = pl.program_id(1) * DN_HEADS_PER_STEP + hh
        cols = slice(hh * DN_DH, (hh + 1) * DN_DH)
        q = q_ref[:, cols]
        k = k_ref[:, cols]
        kb16 = k.astype(BF16)
        gram_k = lax.dot_general(kb16, kb16, NT_DIMS, preferred_element_type=F32)
        gram_q = lax.dot_general(q.astype(BF16), kb16, NT_DIMS, preferred_element_type=F32)
        for d in range(2):
            incl = same(c) & ((ci >= ri) if d else (ci <= ri))
            gcol = pick(d * DN_HEADS + hd)
            bcol = pick((2 + d) * DN_HEADS + hd)
            tcol = pick((4 + d) * DN_HEADS + hd)
            grow = gxt_ref[pl.ds(d * DN_HEADS + hd, 1), :]
            decay = jnp.where(incl, jnp.exp(gcol - grow), 0.0)
            a = jnp.where(ri == ci, 0.0, gram_k * bcol * decay)
            eg = jnp.exp(gcol)
            qk = (gram_q * decay).astype(BF16)
            for n in range(n_chunks):
                qk_ref[d, hh, n * c:(n + 1) * c, :] = qk[n * c:(n + 1) * c, n * c:(n + 1) * c]
                gl_ref[d, hh, n] = jnp.broadcast_to(jnp.exp(tcol[n * c:n * c + 1, :]), (8, LANES))
            qd_ref[d, :, cols] = (q * eg).astype(BF16)
            kd_ref[d, :, cols] = (k * jnp.exp(tcol - gcol)).astype(BF16)
            diag = jnp.where(same(DN_BASE), a, 0.0)
            chains.append(dict(d=d, cols=cols, a=a, t=eye - diag, p=diag, scale=bcol, scale_k=bcol * eg))
    size = 2
    while size < DN_BASE:
        for ch in chains:
            ch["p"] = _bdot(ch["p"], ch["p"])
        for ch in chains:
            ch["t"] = ch["t"] + _bdot(ch["t"], ch["p"])
        size *= 2
    size = DN_BASE
    while size < c:
        couple = same(2 * size) & ~same(size)
        for ch in chains:
            ch["et"] = _bdot(jnp.where(couple, ch["a"], 0.0), ch["t"])
        for ch in chains:
            ch["t"] = ch["t"] - _bdot(ch["t"], ch["et"])
        size *= 2
    for ch in chains:
        d, cols = ch["d"], ch["cols"]
        x = _bdot(ch["t"], jnp.concatenate([v_ref[:, cols] * ch["scale"], k_ref[:, cols] * ch["scale_k"]], axis=1))
        u_ref[d, :, cols] = x[:, :DN_DH]
        w_ref[d, :, cols] = x[:, DN_DH:].astype(BF16)


def deltanet_prep(q, k, v, gx, gxt):
    r = q.shape[0]
    tm = min(ROW_TILE, r)
    n_chunks = tm // CHUNK
    hw = DN_HEADS * DN_DH
    hps = DN_HEADS_PER_STEP
    head = pl.BlockSpec((tm, hps * DN_DH), lambda i, h: (i, h))
    dhead = pl.BlockSpec((2, tm, hps * DN_DH), lambda i, h: (0, i, h))
    return pl.pallas_call(
        functools.partial(_dnprep_kernel, n_chunks=n_chunks),
        grid=(r // tm, DN_HEADS // hps),
        in_specs=[head, head, head, pl.BlockSpec((tm, LANES), lambda i, h: (i, 0)),
                  pl.BlockSpec((LANES, tm), lambda i, h: (0, i))],
        out_specs=[dhead, dhead, dhead, dhead,
                   pl.BlockSpec((2, hps, tm, CHUNK), lambda i, h: (0, h, i, 0)),
                   pl.BlockSpec((2, hps, n_chunks, 8, LANES), lambda i, h: (0, h, i, 0, 0))],
        out_shape=[jax.ShapeDtypeStruct((2, r, hw), F32),
                   jax.ShapeDtypeStruct((2, r, hw), BF16),
                   jax.ShapeDtypeStruct((2, r, hw), BF16),
                   jax.ShapeDtypeStruct((2, r, hw), BF16),
                   jax.ShapeDtypeStruct((2, DN_HEADS, r, CHUNK), BF16),
                   jax.ShapeDtypeStruct((2, DN_HEADS, r // CHUNK, 8, LANES), F32)],
        compiler_params=_params("parallel", "parallel"), name="deltanet_prep",
    )(q, k, v, gx, gxt)


def _dnscan_kernel(*refs, n_steps):
    ins = (refs[0:6], refs[6:12])
    s0_ref, o_refs, sf_ref, s_ref = refs[12], refs[13:15], refs[15], refs[16]
    step = pl.program_id(0)

    @pl.when(step == 0)
    def _():
        s_ref[...] = s0_ref[...]

    c = CHUNK
    n_sub = ins[0][0].shape[0] // c
    col = lambda hd: slice(hd * DN_DH, (hd + 1) * DN_DH)
    dot = lambda a, b: jnp.dot(a, b, preferred_element_type=F32)
    for t in range(n_sub):
        chains = []
        for d in range(2):
            u_ref, w_ref, qd_ref, kd_ref, qk_ref, gl_ref = ins[d]
            n = n_sub - 1 - t if d else t
            rows = slice(n * c, (n + 1) * c)
            for hd in range(DN_HEADS):
                chains.append(dict(d=d, hd=hd, rows=rows, u=u_ref.at[rows, col(hd)], w=w_ref.at[rows, col(hd)],
                                   qd=qd_ref.at[rows, col(hd)], kd=kd_ref.at[rows, col(hd)],
                                   qk=qk_ref.at[hd, rows, :], gl=gl_ref.at[hd, n, 0:1, :]))
        for ch in chains:
            ch["s"] = s_ref[ch["d"], ch["hd"]]
            ch["sb"] = ch["s"].astype(BF16)
        for ch in chains:
            ch["ws"] = dot(ch["w"][...], ch["sb"])
        for ch in chains:
            ch["qs"] = dot(ch["qd"][...], ch["sb"])
        for ch in chains:
            ch["vb"] = (ch["u"][...] - ch["ws"]).astype(BF16)
        for ch in chains:
            o_refs[ch["d"]][ch["rows"], col(ch["hd"])] = ch["qs"] + dot(ch["qk"][...], ch["vb"])
        for ch in chains:
            ch["ds"] = lax.dot_general(ch["kd"][...], ch["vb"], TN_DIMS, preferred_element_type=F32)
        for ch in chains:
            s_ref[ch["d"], ch["hd"]] = ch["s"] * ch["gl"][...] + ch["ds"]

    @pl.when(step == n_steps - 1)
    def _():
        sf_ref[...] = s_ref[...]


def deltanet_scan(u, w, qd, kd, qk, gl, s0):
    r = u.shape[1]
    hw = DN_HEADS * DN_DH
    tm = min(ROW_TILE, r)
    n_steps = r // tm
    state = pl.BlockSpec((2, DN_HEADS, DN_DH, DN_DH), lambda i: (0, 0, 0, 0))
    in_specs, out_specs = [], []
    for d in range(2):
        idx = (lambda i: n_steps - 1 - i) if d else (lambda i: i)
        big = pl.BlockSpec((None, tm, hw), lambda i, d=d, idx=idx: (d, idx(i), 0))
        in_specs += [big, big, big, big,
                     pl.BlockSpec((None, DN_HEADS, tm, CHUNK), lambda i, d=d, idx=idx: (d, 0, idx(i), 0)),
                     pl.BlockSpec((None, DN_HEADS, tm // CHUNK, 8, LANES), lambda i, d=d, idx=idx: (d, 0, idx(i), 0, 0))]
        out_specs.append(pl.BlockSpec((tm, hw), lambda i, idx=idx: (idx(i), 0)))
    return pl.pallas_call(
        functools.partial(_dnscan_kernel, n_steps=n_steps),
        grid=(n_steps,),
        in_specs=in_specs + [state],
        out_specs=out_specs + [state],
        out_shape=[jax.ShapeDtypeStruct((r, hw), F32), jax.ShapeDtypeStruct((r, hw), F32),
                   jax.ShapeDtypeStruct((2, DN_HEADS, DN_DH, DN_DH), F32)],
        scratch_shapes=[pltpu.VMEM((2, DN_HEADS, DN_DH, DN_DH), F32)],
        compiler_params=_params("arbitrary"), name="deltanet_scan",
    )(u, w, qd, kd, qk, gl, u, w, qd, kd, qk, gl, s0)


def _lru_kernel(xf_ref, xb_ref, wa_ref, wx_ref, ba_ref, bx_ref, lam_ref, h0_ref, hf_ref, hb_ref, hl_ref,
                a_s, b_s, carry_s, *, n_steps):
    step = pl.program_id(0)

    @pl.when(step == 0)
    def _():
        carry_s[...] = h0_ref[...]

    tm = xf_ref.shape[0]
    for d, xc_ref in enumerate((xf_ref, xb_ref)):
        sp = _softplus(-lam_ref[d])
        for n in range(LRU_BLOCKS):
            cols = slice(n * LRU_BW, (n + 1) * LRU_BW)
            xb = xc_ref[:, cols]
            xbb = xb.astype(BF16)
            r = jax.nn.sigmoid(jnp.dot(xbb, wa_ref[d, n], preferred_element_type=F32) + ba_ref[d, :, cols])
            gi = jax.nn.sigmoid(jnp.dot(xbb, wx_ref[d, n], preferred_element_type=F32) + bx_ref[d, :, cols])
            log_a = -LRU_C * r * sp[:, cols]
            a = jnp.exp(log_a)
            a_s[d, :, cols] = a
            b_s[d, :, cols] = jnp.sqrt(-jnp.tanh(log_a) * (a * a + 1.0)) * (gi * xb)

    rid = lax.broadcasted_iota(I32, (8, a_s.shape[2]), 0)
    n_groups = tm // 8
    h_refs = (hf_ref, hb_ref)

    def group(gidx, carries):
        out = []
        for d in range(2):
            g = (n_groups - 1 - gidx) if d else gidx
            base = pl.multiple_of(g * 8, 8)
            a = a_s[d, pl.ds(base, 8), :]
            b = b_s[d, pl.ds(base, 8), :]
            for sh in (1, 2, 4):
                if d:
                    keep = rid < 8 - sh
                    a_n = jnp.where(keep, pltpu.roll(a, 8 - sh, 0), 1.0)
                    b_n = jnp.where(keep, pltpu.roll(b, 8 - sh, 0), 0.0)
                else:
                    keep = rid >= sh
                    a_n = jnp.where(keep, pltpu.roll(a, sh, 0), 1.0)
                    b_n = jnp.where(keep, pltpu.roll(b, sh, 0), 0.0)
                b = a * b_n + b
                a = a * a_n
            h = a * carries[d] + b
            h_refs[d][pl.ds(base, 8), :] = h
            edge = h[0:1, :] if d else h[7:8, :]
            out.append(jnp.broadcast_to(edge, h.shape))
        return tuple(out)

    carries = lax.fori_loop(0, n_groups, group, (carry_s[0], carry_s[1]), unroll=LRU_UNROLL)
    carry_s[0] = carries[0]
    carry_s[1] = carries[1]

    @pl.when(step == n_steps - 1)
    def _():
        hl_ref[...] = carry_s[...]


def lru_scan(xc, wa, wx, ba, bx, lam, h0):
    r, width = xc.shape
    tm = min(ROW_TILE, r)
    n_steps = r // tm
    vec = pl.BlockSpec((2, 1, width), lambda i: (0, 0, 0))
    wspec = pl.BlockSpec((2, LRU_BLOCKS, LRU_BW, LRU_BW), lambda i: (0, 0, 0, 0))
    st = pl.BlockSpec((2, 8, width), lambda i: (0, 0, 0))
    fwd = pl.BlockSpec((tm, width), lambda i: (i, 0))
    bwd = pl.BlockSpec((tm, width), lambda i: (n_steps - 1 - i, 0))
    return pl.pallas_call(
        functools.partial(_lru_kernel, n_steps=n_steps),
        grid=(n_steps,),
        in_specs=[fwd, bwd, wspec, wspec, vec, vec, vec, st],
        out_specs=[fwd, bwd, st],
        out_shape=[jax.ShapeDtypeStruct((r, width), F32), jax.ShapeDtypeStruct((r, width), F32),
                   jax.ShapeDtypeStruct((2, 8, width), F32)],
        scratch_shapes=[pltpu.VMEM((2, tm, width), F32), pltpu.VMEM((2, tm, width), F32),
                        pltpu.VMEM((2, 8, width), F32)],
        compiler_params=_params("arbitrary"), name="lru_scan",
    )(xc, xc, wa, wx, ba, bx, lam, h0)


def _gelu_tanh(x):
    return 0.5 * x * (1.0 + jnp.tanh(math.sqrt(2.0 / math.pi) * (x + 0.044715 * (x * x * x))))


def _evfin_kernel(of_ref, ob_ref, hf_ref, hb_ref, zg_ref, ng_ref, wout_ref, x_ref, gt_ref, o_ref, mix_s):
    hw = DN_HEADS * DN_DH
    for hd in range(DN_HEADS):
        cols = slice(hd * DN_DH, (hd + 1) * DN_DH)
        o = of_ref[:, cols] + ob_ref[:, cols]
        y = o * lax.rsqrt(jnp.mean(o * o, axis=-1, keepdims=True) + EPS) * ng_ref[...]
        mix_s[:, cols] = (y * _silu(zg_ref[:, cols])).astype(BF16)
    mix_s[:, hw:] = ((hf_ref[...] + hb_ref[...]) * _gelu_tanh(zg_ref[:, hw:])).astype(BF16)
    y = jnp.dot(mix_s[...], wout_ref[...], preferred_element_type=F32)
    o_ref[...] = x_ref[...] + gt_ref[...] * y


def even_finish(o_f, o_b, h_f, h_b, p, norm_g, w_out, x, gate):
    r, d = x.shape
    tm = min(ROW_TILE, r)
    hw = DN_HEADS * DN_DH
    row = lambda w: pl.BlockSpec((tm, w), lambda i: (i, 0))
    return pl.pallas_call(
        _evfin_kernel, grid=(r // tm,),
        in_specs=[row(hw), row(hw), row(hw), row(hw),
                  pl.BlockSpec((tm, 2 * hw), lambda i: (i, 2)),
                  pl.BlockSpec((1, DN_DH), lambda i: (0, 0)),
                  pl.BlockSpec(w_out.shape, lambda i: (0, 0)),
                  row(d), pl.BlockSpec((1, d), lambda i: (0, 0))],
        out_specs=row(d),
        out_shape=jax.ShapeDtypeStruct((r, d), F32),
        scratch_shapes=[pltpu.VMEM((tm, 2 * hw), BF16)],
        compiler_params=_params("parallel"), name="even_finish",
    )(o_f, o_b, h_f, h_b, p, norm_g.reshape(1, DN_DH), w_out, x, gate.reshape(1, d))


def _odfin_kernel(of_ref, ob_ref, go_ref, ng_ref, wout_ref, x_ref, gt_ref, o_ref, mix_s, *, dv):
    for hd in range(GLA_HEADS):
        cols = slice(hd * dv, (hd + 1) * dv)
        o = of_ref[:, cols] + ob_ref[:, cols]
        y = o * lax.rsqrt(jnp.mean(o * o, axis=-1, keepdims=True) + EPS) * ng_ref[...]
        mix_s[:, cols] = (y * _silu(go_ref[:, cols])).astype(BF16)
    y = jnp.dot(mix_s[...], wout_ref[...], preferred_element_type=F32)
    o_ref[...] = x_ref[...] + gt_ref[...] * y


def odd_finish(o_f, o_b, p, norm_g, w_out, x, gate):
    r, d = x.shape
    rows = r // GRID_W
    vw = o_f.shape[1]
    dv = vw // GLA_HEADS
    row = lambda w: pl.BlockSpec((rows, w), lambda c: (c, 0))
    return pl.pallas_call(
        functools.partial(_odfin_kernel, dv=dv), grid=(GRID_W,),
        in_specs=[row(vw), row(vw),
                  pl.BlockSpec((rows, vw), lambda c: (c, 2)),
                  pl.BlockSpec((1, dv), lambda c: (0, 0)),
                  pl.BlockSpec(w_out.shape, lambda c: (0, 0)),
                  _raster_spec(rows, d), pl.BlockSpec((1, d), lambda c: (0, 0))],
        out_specs=_raster_spec(rows, d),
        out_shape=jax.ShapeDtypeStruct((rows, GRID_W * d), F32),
        scratch_shapes=[pltpu.VMEM((rows, vw), BF16)],
        compiler_params=_params("parallel"), name="odd_finish",
    )(o_f, o_b, p, norm_g.reshape(1, dv), w_out, x.reshape(rows, GRID_W * d), gate.reshape(1, d)).reshape(r, d)


def _gla_kernel(*refs, n_steps, dk):
    ins = (refs[0:6], refs[6:12])
    s0_ref, o_refs, sf_ref, s_ref, gc_s = refs[12], refs[13:15], refs[15], refs[16], refs[17]
    step = pl.program_id(1)

    @pl.when(step == 0)
    def _():
        s_ref[...] = s0_ref[...]

    c = CHUNK
    tm = ins[0][0].shape[0]
    n_chunks = tm // c
    n_sub = c // SUB
    ri = lax.broadcasted_iota(I32, (tm, tm), 0)
    ci = lax.broadcasted_iota(I32, (tm, tm), 1)
    r64 = lax.broadcasted_iota(I32, (c, c), 0)
    c64 = lax.broadcasted_iota(I32, (c, c), 1)
    rr = lax.broadcasted_iota(I32, (c, 1), 0) % SUB
    units = []
    g_min = None
    for d in range(2):
        q_ref, k_ref, v_ref, gd_ref, wg_ref, bg_ref = ins[d]
        tri = (((ri // c) == (ci // c)) & ((ci >= ri) if d else (ci <= ri))).astype(BF16)
        logit = _bdot(gd_ref[...], wg_ref[...]) + bg_ref[...]
        g = -_softplus(-logit) * (1.0 / GLA_TAU)
        gc = _dot01(tri, g)
        gc_s[d] = gc
        g_min = jnp.min(g) if g_min is None else jnp.minimum(g_min, jnp.min(g))
        q = q_ref[...] * dk ** -0.5
        for n in range(n_chunks):
            rows = slice(n * c, (n + 1) * c)
            units.append(dict(d=d, n=n, rows=rows, q=q[rows], k=k_ref[rows, :], g=g[rows], gc=gc[rows],
                              att=jnp.zeros((c, c), F32)))
    size = c // 2
    while size >= SUB:
        r_hi, c_hi = (r64 & size) != 0, (c64 & size) != 0
        same = (r64 // (2 * size)) == (c64 // (2 * size))
        pair = (same & r_hi & ~c_hi, same & ~r_hi & c_hi)
        for u in units:
            gc = u["gc"]
            pieces = []
            for b in range(c // (2 * size)):
                mid = b * 2 * size + size
                edge = gc[mid:mid + 1] if u["d"] else gc[mid - 1:mid]
                pieces.append(jnp.broadcast_to(edge, (2 * size, dk)))
            edge = pieces[0] if len(pieces) == 1 else jnp.concatenate(pieces, axis=0)
            qs = (u["q"] * jnp.exp(gc - edge)).astype(BF16)
            ks = (u["k"] * jnp.exp(edge - gc)).astype(BF16)
            u["att"] = u["att"] + jnp.where(pair[u["d"]],
                                            lax.dot_general(qs, ks, NT_DIMS, preferred_element_type=F32), 0.0)
        size //= 2

    def diag_direct():
        blocks = [jnp.zeros((c, c), F32) for _ in units]
        for jj in range(SUB):
            here = c64 == (r64 // SUB) * SUB + jj
            seen = (here & (rr >= jj), here & (rr <= jj))
            for i, u in enumerate(units):
                rep = lambda ref: jnp.concatenate(
                    [jnp.broadcast_to(ref[pl.ds(u["n"] * c + b * SUB + jj, 1), :], (SUB, dk)) for b in range(n_sub)],
                    axis=0)
                col = jnp.sum(u["q"] * rep(ins[u["d"]][1]) * jnp.exp(u["gc"] - rep(gc_s.at[u["d"]])),
                              axis=-1, keepdims=True)
                blocks[i] = jnp.where(seen[u["d"]], col, blocks[i])
        return tuple(blocks)

    def diag_factored():
        blocks = []
        same = (r64 // SUB) == (c64 // SUB)
        inside = (same & (c64 <= r64), same & (c64 >= r64))
        for u in units:
            before = (u["gc"] - u["g"]).reshape(n_sub, SUB, dk)
            edge = before[:, SUB - 1:SUB, :] if u["d"] else before[:, 0:1, :]
            edge = jnp.broadcast_to(edge, before.shape).reshape(c, dk)
            qs = (u["q"] * jnp.exp(u["gc"] - edge)).astype(BF16)
            ks = (u["k"] * jnp.exp(edge - u["gc"])).astype(BF16)
            blocks.append(jnp.where(inside[u["d"]],
                                    lax.dot_general(qs, ks, NT_DIMS, preferred_element_type=F32), 0.0))
        return tuple(blocks)

    diag = lax.cond(g_min >= -GLA_SAFE_STEP, diag_factored, diag_direct)
    for u in units:
        u["vb"] = ins[u["d"]][2][u["rows"], :].astype(BF16)
        u["tot"] = u["gc"][0:1] if u["d"] else u["gc"][c - 1:c]
    for i, u in enumerate(units):
        u["o"] = jnp.dot((u["att"] + diag[i]).astype(BF16), u["vb"], preferred_element_type=F32)
    for u in units:
        u["qd"] = (u["q"] * jnp.exp(u["gc"])).astype(BF16)
        u["ds"] = lax.dot_general(u["vb"], (u["k"] * jnp.exp(u["tot"] - u["gc"])).astype(BF16), TN_DIMS,
                                  preferred_element_type=F32)
    for t in range(n_chunks):
        for d in range(2):
            u = units[d * n_chunks + (n_chunks - 1 - t if d else t)]
            s = s_ref[d]
            o_refs[d][u["rows"], :] = u["o"] + lax.dot_general(u["qd"], s.astype(BF16), NT_DIMS,
                                                               preferred_element_type=F32)
            s_ref[d] = s * jnp.exp(u["tot"]) + u["ds"]

    @pl.when(step == n_steps - 1)
    def _():
        sf_ref[...] = s_ref[...]


def gla_scan(p, pg, wg_pad, bg, s0, *, dk, dv):
    r = p.shape[0]
    tm = min(ROW_TILE, r)
    n_steps = r // tm
    qk_blocks = GLA_HEADS
    v_block0 = 2 * GLA_HEADS * dk // dv
    state = pl.BlockSpec((2, None, dv, dk), lambda h, i: (0, h, 0, 0))
    in_specs, out_specs = [], []
    for d in range(2):
        idx = (lambda i: n_steps - 1 - i) if d else (lambda i: i)
        in_specs += [pl.BlockSpec((tm, dk), lambda h, i, idx=idx: (idx(i), h)),
                     pl.BlockSpec((tm, dk), lambda h, i, idx=idx: (idx(i), qk_blocks + h)),
                     pl.BlockSpec((tm, dv), lambda h, i, idx=idx: (idx(i), v_block0 + h)),
                     pl.BlockSpec((tm, LANES), lambda h, i, idx=idx: (idx(i), 0)),
                     pl.BlockSpec((None, LANES, dk), lambda h, i, d=d: (d, 0, h)),
                     pl.BlockSpec((None, 1, dk), lambda h, i, d=d: (d, 0, h))]
        out_specs.append(pl.BlockSpec((tm, dv), lambda h, i, idx=idx: (idx(i), h)))
    args = (p, p, p, pg, wg_pad, bg)
    return pl.pallas_call(
        functools.partial(_gla_kernel, n_steps=n_steps, dk=dk),
        grid=(GLA_HEADS, n_steps),
        in_specs=in_specs + [state],
        out_specs=out_specs + [state],
        out_shape=[jax.ShapeDtypeStruct((r, GLA_HEADS * dv), F32), jax.ShapeDtypeStruct((r, GLA_HEADS * dv), F32),
                   jax.ShapeDtypeStruct((2, GLA_HEADS, dv, dk), F32)],
        scratch_shapes=[pltpu.VMEM((2, dv, dk), F32), pltpu.VMEM((2, tm, dk), F32)],
        compiler_params=_params("parallel", "arbitrary"), name="gla_scan",
    )(*args, *args, s0)


def _when_block_used(nu_ref, o_ref, body):
    used = pl.program_id(1) < nu_ref[0]

    @pl.when(used)
    def _():
        body()

    @pl.when(jnp.logical_not(used))
    def _():
        o_ref[...] = jnp.zeros_like(o_ref)


FFN_TN = 1408


def _ffn1_kernel(be_ref, nu_ref, x_ref, wg_ref, wu_ref, o_ref, w_s):
    i = pl.program_id(1)
    tn = o_ref.shape[1]

    @pl.when((i == 0) | (be_ref[i] != be_ref[jnp.maximum(i - 1, 0)]))
    def _():
        w_s[:, :tn] = wg_ref[...]
        w_s[:, tn:] = wu_ref[...]

    def body():
        ab = jnp.dot(x_ref[...], w_s[...], preferred_element_type=F32)
        o_ref[...] = (_silu(ab[:, :tn]) * ab[:, tn:]).astype(o_ref.dtype)

    _when_block_used(nu_ref, o_ref, body)


def ffn_up(x, w_gate, w_up, block_expert, n_used, *, tm):
    r, d = x.shape
    hidden = w_gate.shape[2]
    tn = FFN_TN
    wspec = pl.BlockSpec((None, d, tn), lambda j, i, be, nu: (be[i], 0, j))
    return pl.pallas_call(
        _ffn1_kernel,
        grid_spec=pltpu.PrefetchScalarGridSpec(
            num_scalar_prefetch=2, grid=(hidden // tn, r // tm),
            in_specs=[pl.BlockSpec((tm, d), lambda j, i, be, nu: (i, 0)), wspec, wspec],
            out_specs=pl.BlockSpec((tm, tn), lambda j, i, be, nu: (i, j)),
            scratch_shapes=[pltpu.VMEM((d, 2 * tn), BF16)]),
        out_shape=jax.ShapeDtypeStruct((r, hidden), BF16),
        compiler_params=_params("parallel", "arbitrary"), name="ffn_up",
    )(block_expert, n_used, x, w_gate, w_up)


def _ffn2_res_kernel(be_ref, nu_ref, h_ref, w_ref, x_ref, gt_ref, o_ref):
    def body():
        y = jnp.dot(h_ref[...], w_ref[...], preferred_element_type=F32)
        o_ref[...] = x_ref[...] + gt_ref[...] * y

    _when_block_used(nu_ref, o_ref, body)


def _ffn2_scale_kernel(be_ref, nu_ref, h_ref, w_ref, sw_ref, o_ref):
    def body():
        y = jnp.dot(h_ref[...], w_ref[...], preferred_element_type=F32)
        o_ref[...] = (y * sw_ref[...]).astype(o_ref.dtype)

    _when_block_used(nu_ref, o_ref, body)


def ffn_down_residual(h, w_down, block_expert, n_used, x, gate, *, tm, tn=1024):
    r, hidden = h.shape
    d = w_down.shape[2]
    return pl.pallas_call(
        _ffn2_res_kernel,
        grid_spec=pltpu.PrefetchScalarGridSpec(
            num_scalar_prefetch=2, grid=(d // tn, r // tm),
            in_specs=[pl.BlockSpec((tm, hidden), lambda j, i, be, nu: (i, 0)),
                      pl.BlockSpec((None, hidden, tn), lambda j, i, be, nu: (be[i], 0, j)),
                      pl.BlockSpec((tm, tn), lambda j, i, be, nu: (i, j)),
                      pl.BlockSpec((1, tn), lambda j, i, be, nu: (0, j))],
            out_specs=pl.BlockSpec((tm, tn), lambda j, i, be, nu: (i, j))),
        out_shape=jax.ShapeDtypeStruct((r, d), F32),
        compiler_params=_params("parallel", "arbitrary"), name="ffn_down_residual",
    )(block_expert, n_used, h, w_down, x, gate.reshape(1, d))


def ffn_down_scaled(h, w_down, block_expert, n_used, slot_w, *, tn=1024):
    r, hidden = h.shape
    d = w_down.shape[2]
    tm = min(MOE_ROWS, r)
    return pl.pallas_call(
        _ffn2_scale_kernel,
        grid_spec=pltpu.PrefetchScalarGridSpec(
            num_scalar_prefetch=2, grid=(d // tn, r // tm),
            in_specs=[pl.BlockSpec((tm, hidden), lambda j, i, be, nu: (i, 0)),
                      pl.BlockSpec((None, hidden, tn), lambda j, i, be, nu: (be[i], 0, j)),
                      pl.BlockSpec((tm, 1), lambda j, i, be, nu: (i, 0))],
            out_specs=pl.BlockSpec((tm, tn), lambda j, i, be, nu: (i, j))),
        out_shape=jax.ShapeDtypeStruct((r, d), BF16),
        compiler_params=_params("parallel", "arbitrary"), name="ffn_down_scaled",
    )(block_expert, n_used, h, w_down, slot_w.reshape(r, 1))


def _router_kernel(x_ref, g_ref, sh_ref, sc_ref, rw_ref, rb_ref, h_ref, idx_ref, wt_ref, rank_ref, cnt_ref):
    x = x_ref[...]
    y = x * lax.rsqrt(jnp.mean(x * x, axis=-1, keepdims=True) + EPS)
    h = (y * g_ref[...]) * (1.0 + sc_ref[...]) + sh_ref[...]
    hb = h.astype(BF16)
    h_ref[...] = hb
    logits = jnp.dot(hb, rw_ref[...], preferred_element_type=F32) + rb_ref[...]
    lane = lax.broadcasted_iota(I32, logits.shape, 1)
    neg = jnp.float32(-jnp.inf)
    logits = jnp.where(lane < N_EXPERTS, logits, neg)
    m0 = jnp.max(logits, axis=-1, keepdims=True)
    i0 = jnp.min(jnp.where(logits == m0, lane, LANES), axis=-1, keepdims=True)
    rest = jnp.where(lane == i0, neg, logits)
    m1 = jnp.max(rest, axis=-1, keepdims=True)
    i1 = jnp.min(jnp.where(rest == m1, lane, LANES), axis=-1, keepdims=True)
    e1 = jnp.exp(m1 - m0)
    w0 = 1.0 / (1.0 + e1)
    idx_ref[...] = jnp.where(lane == 0, i0, jnp.where(lane == 1, i1, 0))
    wt_ref[...] = jnp.where(lane == 0, w0, jnp.where(lane == 1, e1 * w0, 0.0))
    tm = x.shape[0]
    hot = (lane == i0) | (lane == i1)
    earlier = (lax.broadcasted_iota(I32, (tm, tm), 1) < lax.broadcasted_iota(I32, (tm, tm), 0)).astype(BF16)
    before = jnp.dot(earlier, hot.astype(BF16), preferred_element_type=F32)
    r0 = jnp.sum(jnp.where(lane == i0, before, 0.0), axis=-1, keepdims=True)
    r1 = jnp.sum(jnp.where(lane == i1, before, 0.0), axis=-1, keepdims=True)
    rank_ref[...] = jnp.where(lane == 0, r0, jnp.where(lane == 1, r1, 0.0)).astype(I32)
    cnt_ref[...] = jnp.broadcast_to(jnp.sum(hot.astype(F32), axis=0, keepdims=True), cnt_ref.shape).astype(I32)


def route(x, g, sh, sc, router_w, router_b):
    t, d = x.shape
    tm = min(ROW_TILE, t)
    vec = pl.BlockSpec((1, d), lambda i: (0, 0))
    row = lambda w: pl.BlockSpec((tm, w), lambda i: (i, 0))
    rw = jnp.zeros((d, LANES), BF16).at[:, :N_EXPERTS].set(router_w.astype(BF16))
    rb = jnp.zeros((1, LANES), F32).at[0, :N_EXPERTS].set(router_b)
    h, idx, wt, rank, cnt = pl.pallas_call(
        _router_kernel, grid=(t // tm,),
        in_specs=[row(d), vec, vec, vec, pl.BlockSpec((d, LANES), lambda i: (0, 0)),
                  pl.BlockSpec((1, LANES), lambda i: (0, 0))],
        out_specs=[row(d), row(LANES), row(LANES), row(LANES), pl.BlockSpec((8, LANES), lambda i: (i, 0))],
        out_shape=[jax.ShapeDtypeStruct((t, d), BF16), jax.ShapeDtypeStruct((t, LANES), I32),
                   jax.ShapeDtypeStruct((t, LANES), F32), jax.ShapeDtypeStruct((t, LANES), I32),
                   jax.ShapeDtypeStruct((t // tm * 8, LANES), I32)],
        compiler_params=_params("parallel"), name="route",
    )(x, g.reshape(1, d), sh.reshape(1, d), sc.reshape(1, d), rw, rb)
    return h, idx[:, :2], wt[:, :2], rank[:, :2], cnt.reshape(t // tm, 8, LANES)[:, 0, :N_EXPERTS]


GATHER_WINDOW_TILES = 6


def _gather_kernel(pb_ref, ps_ref, pn_ref, pf_ref, h_ref, dest_ref, wt_ref, o_ref, sw_ref, acc_ref, swacc_ref):
    i = pl.program_id(0)
    flags = pf_ref[i]
    win = h_ref.shape[0]

    @pl.when((flags & 1) != 0)
    def _():
        acc_ref[...] = jnp.zeros_like(acc_ref)
        swacc_ref[...] = jnp.zeros_like(swacc_ref)

    @pl.when((flags & 4) != 0)
    def _():
        slot = pb_ref[i] * MOE_ROWS + lax.broadcasted_iota(I32, (MOE_ROWS, 1), 0)
        fresh = ps_ref[i] + lax.broadcasted_iota(I32, (1, win), 1) >= pn_ref[i]
        hit0 = (dest_ref[0:1, :] == slot) & fresh
        hit1 = (dest_ref[1:2, :] == slot) & fresh
        acc_ref[...] += jnp.dot((hit0 | hit1).astype(BF16), h_ref[...], preferred_element_type=F32)
        swacc_ref[...] += jnp.sum(jnp.where(hit0, wt_ref[0:1, :], 0.0) + jnp.where(hit1, wt_ref[1:2, :], 0.0),
                                  axis=1, keepdims=True)

    @pl.when((flags & 2) != 0)
    def _():
        o_ref[...] = acc_ref[...].astype(o_ref.dtype)
        sw_ref[...] = swacc_ref[...]


def moe_gather(h, dest_t, wt_t, n_slots, win, pair_block, pair_start, pair_new, pair_flags):
    t, d = h.shape
    n_pairs = pair_block.shape[0]
    start = lambda ps, i: pl.multiple_of(ps[i], ROW_TILE)
    tok = pl.BlockSpec((pl.Element(2), pl.Element(win)), lambda i, pb, ps, pn, pf: (0, start(ps, i)))
    return pl.pallas_call(
        _gather_kernel,
        grid_spec=pltpu.PrefetchScalarGridSpec(
            num_scalar_prefetch=4, grid=(n_pairs,),
            in_specs=[pl.BlockSpec((pl.Element(win), pl.Element(d)),
                                   lambda i, pb, ps, pn, pf: (start(ps, i), 0)), tok, tok],
            out_specs=[pl.BlockSpec((MOE_ROWS, d), lambda i, pb, ps, pn, pf: (pb[i], 0)),
                       pl.BlockSpec((MOE_ROWS, 1), lambda i, pb, ps, pn, pf: (pb[i], 0))],
            scratch_shapes=[pltpu.VMEM((MOE_ROWS, d), F32), pltpu.VMEM((MOE_ROWS, 1), F32)]),
        out_shape=[jax.ShapeDtypeStruct((n_slots, d), BF16), jax.ShapeDtypeStruct((n_slots, 1), F32)],
        compiler_params=_params("arbitrary"), name="moe_gather",
    )(pair_block, pair_start, pair_new, pair_flags, h, dest_t, wt_t)


def _combine_kernel(st_ref, lo_ref, cn_ref, *refs):
    yb_refs = refs[:N_EXPERTS]
    dest_ref, x_ref, gt_ref, ng_ref, o_ref = refs[N_EXPERTS:]
    i = pl.program_id(0)
    win = yb_refs[0].shape[0]
    d0 = dest_ref[:, 0:1]
    d1 = dest_ref[:, 1:2]
    acc = None
    for e in range(N_EXPERTS):
        g = i * N_EXPERTS + e
        slot = st_ref[g] + lax.broadcasted_iota(I32, (1, win), 1)
        mine = (slot >= lo_ref[g]) & (slot < lo_ref[g] + cn_ref[g])
        hit = ((d0 == slot) | (d1 == slot)) & mine
        part = jnp.dot(hit.astype(BF16), yb_refs[e][...], preferred_element_type=F32)
        acc = part if acc is None else acc + part
    y = x_ref[...] + gt_ref[...] * acc
    o_ref[...] = y * lax.rsqrt(jnp.mean(y * y, axis=-1, keepdims=True) + EPS) * ng_ref[...]


def moe_combine_norm(yb, dest, win_start, grp_lo, grp_cnt, x, gate, norm_g):
    t, d = x.shape
    tm = min(ROW_TILE, t)
    win = 2 * MOE_ROWS
    yb_spec = lambda e: pl.BlockSpec((pl.Element(win), pl.Element(d)),
                                     lambda i, st, lo, cn: (pl.multiple_of(st[i * N_EXPERTS + e], MOE_ROWS), 0))
    vec = pl.BlockSpec((1, d), lambda i, st, lo, cn: (0, 0))
    return pl.pallas_call(
        _combine_kernel,
        grid_spec=pltpu.PrefetchScalarGridSpec(
            num_scalar_prefetch=3, grid=(t // tm,),
            in_specs=[yb_spec(e) for e in range(N_EXPERTS)]
            + [pl.BlockSpec((tm, 2), lambda i, st, lo, cn: (i, 0)),
               pl.BlockSpec((tm, d), lambda i, st, lo, cn: (i, 0)), vec, vec],
            out_specs=pl.BlockSpec((tm, d), lambda i, st, lo, cn: (i, 0))),
        out_shape=jax.ShapeDtypeStruct((t, d), F32),
        compiler_params=_params("arbitrary"), name="moe_combine",
    )(win_start, grp_lo, grp_cnt, *([yb] * N_EXPERTS), dest, x, gate.reshape(1, d), norm_g.reshape(1, d))


def _pair_lists(lo, hi, nonempty, n_pairs):
    cnt = jnp.where(nonempty, hi - lo + 1, 1)
    end = jnp.cumsum(cnt)
    start = end - cnt
    i = jnp.arange(n_pairs, dtype=I32)
    ic = jnp.minimum(i, end[-1] - 1)
    grp = jnp.sum((end[None, :] <= ic[:, None]).astype(I32), axis=1)
    off = ic - start[grp]
    member = jnp.where(nonempty[grp], lo[grp] + off, 0).astype(I32)
    return grp, member, off == 0, off == cnt[grp] - 1, nonempty[grp], i < end[-1]


def _pair_flags(first, last, data, valid):
    flags = jnp.where(first, 1, 0) | jnp.where(last, 2, 0) | jnp.where(data, 4, 0)
    return jnp.where(valid, flags, 0).astype(I32)


def moe_plan(top_idx, rank, tile_cnt, n_tok):
    tm = min(ROW_TILE, n_tok)
    n_tiles = n_tok // tm
    n_blocks = 2 * n_tok // MOE_ROWS + N_EXPERTS
    experts = jnp.arange(N_EXPERTS, dtype=I32)
    tile_off = jnp.cumsum(tile_cnt, axis=0) - tile_cnt
    counts = jnp.sum(tile_cnt, axis=0)
    padded = (counts + MOE_ROWS - 1) // MOE_ROWS * MOE_ROWS
    pad_end = jnp.cumsum(padded)
    base = (pad_end - padded)[None, :] + tile_off
    hot = top_idx[:, :, None] == experts[None, None, :]
    dest = (jnp.sum(jnp.where(hot, jnp.repeat(base, tm, axis=0)[:, None, :], 0), axis=-1) + rank).astype(I32)
    blk = jnp.arange(n_blocks, dtype=I32)
    block_expert = jnp.minimum(jnp.sum((pad_end[None, :] <= (blk * MOE_ROWS)[:, None]).astype(I32), axis=1),
                               N_EXPERTS - 1)
    c_start = jnp.minimum(base // MOE_ROWS * MOE_ROWS, (n_blocks - 2) * MOE_ROWS)
    mine = block_expert[:, None, None] == experts[None, None, :]
    base_b = jnp.sum(jnp.where(mine, base[None], 0), axis=-1)
    cnt_b = jnp.sum(jnp.where(mine, tile_cnt[None], 0), axis=-1)
    sends = (cnt_b > 0) & (base_b + cnt_b > (blk * MOE_ROWS)[:, None]) & (base_b < ((blk + 1) * MOE_ROWS)[:, None])
    tiles = jnp.arange(n_tiles, dtype=I32)[None, :]
    t_lo = jnp.min(jnp.where(sends, tiles, n_tiles), axis=1)
    t_hi = jnp.max(jnp.where(sends, tiles, -1), axis=1)
    wt = min(GATHER_WINDOW_TILES, n_tiles)
    n_windows = n_blocks + (N_EXPERTS * n_tiles + n_blocks + wt - 1) // wt
    g_block, g_win, *g_bits = _pair_lists(jnp.zeros_like(t_lo), (t_hi - t_lo) // wt, jnp.any(sends, axis=1),
                                          n_windows)
    g_new = (t_lo[g_block] + g_win * wt) * tm
    g_start = jnp.minimum(g_new, (n_tiles - wt) * tm)
    return dict(dest=dest, block_expert=block_expert.astype(I32), n_slots=n_blocks * MOE_ROWS,
                n_used=(pad_end[-1:] // MOE_ROWS).astype(I32), window=wt * tm,
                gather=(g_block, g_start.astype(I32), g_new.astype(I32), _pair_flags(*g_bits)),
                combine=(c_start.reshape(-1).astype(I32), base.reshape(-1).astype(I32),
                         tile_cnt.reshape(-1).astype(I32)))


def _mods(mod_all, layer, row, d):
    m = mod_all[layer, row]
    return tuple(m[j * d:(j + 1) * d] for j in range(N_MOD))


def _even_layer(x_lat, x_ctx, mods_lat, mods_ctx, norm1_g, norm2_g, w):
    d = x_lat.shape[1]
    hw = DN_HEADS * DN_DH
    s_dn = jnp.zeros((2, DN_HEADS, DN_DH, DN_DH), F32)
    s_lru = jnp.zeros((2, 8, hw), F32)
    outs = []
    for x, mods in ((x_ctx, mods_ctx), (x_lat, mods_lat)):
        sh1, sc1, gt1, sh2, sc2, gt2 = mods
        p, pg = norm_proj(x, norm1_g, sh1, sc1, w["w_main"], w["w_gate"])
        q, k, v, xc, gx, gxt = even_prep(p, pg, w["conv_w"], w["conv_b"], w["a_log"], w["dt_bias"])
        u, wm, qd, kd, qk, gl = deltanet_prep(q, k, v, gx, gxt)
        o_f, o_b, s_dn = deltanet_scan(u, wm, qd, kd, qk, gl, s_dn)
        h_f, h_b, s_lru = lru_scan(xc, w["lru_wa"], w["lru_wx"], w["lru_ba"], w["lru_bx"], w["lru_lam"], s_lru)
        x = even_finish(o_f, o_b, h_f, h_b, p, w["dn_norm_g"], w["w_out"], x, gt1)
        h2 = norm_mod(x, norm2_g, sh2, sc2)
        tm = _dense_rows(h2.shape[0])
        be = jnp.zeros((h2.shape[0] // tm,), I32)
        every = jnp.full((1,), h2.shape[0] // tm, I32)
        hh = ffn_up(h2, w["ffn_gate"], w["ffn_up"], be, every, tm=tm)
        x = ffn_down_residual(hh, w["ffn_down"], be, every, x, gt2, tm=tm)
        outs.append(x)
    return outs[1], outs[0]


def _odd_layer_last(x_lat, x_ctx, mods_lat, mods_ctx, norm1_g, norm2_g, final_g, w):
    dk, dv = w["dk"], w["dv"]
    states = jnp.zeros((2, GLA_HEADS, dv, dk), F32)
    sh1, sc1 = mods_ctx[0], mods_ctx[1]
    p, pg = norm_proj(x_ctx, norm1_g, sh1, sc1, w["w_main"], w["w_gate"])
    _, _, states = gla_scan(p, pg, w["wg_pad"], w["bg"], states, dk=dk, dv=dv)
    sh1, sc1, gt1, sh2, sc2, gt2 = mods_lat
    p, pg = norm_proj(x_lat, norm1_g, sh1, sc1, w["w_main"], w["w_gate"], raster=True)
    o_f, o_b, _ = gla_scan(p, pg, w["wg_pad"], w["bg"], states, dk=dk, dv=dv)
    x = odd_finish(o_f, o_b, p, w["gla_norm_g"], w["w_out"], x_lat, gt1)
    n_tok = x.shape[0]
    h2, top_idx, top_w, rank, tile_cnt = route(x, norm2_g, sh2, sc2, w["router_w"], w["router_b"])
    plan = moe_plan(top_idx, rank, tile_cnt, n_tok)
    xb, slot_w = moe_gather(h2, plan["dest"].T, top_w.T, plan["n_slots"], plan["window"], *plan["gather"])
    hh = ffn_up(xb, w["exp_gate"], w["exp_up"], plan["block_expert"], plan["n_used"], tm=MOE_ROWS)
    yb = ffn_down_scaled(hh, w["exp_down"], plan["block_expert"], plan["n_used"], slot_w)
    return moe_combine_norm(yb, plan["dest"], *plan["combine"], x, gt2, final_g)


def kernel(x, c, ctx, c_ctx, mod_w, mod_b, norm1_g, norm2_g, ev_w_in, ev_conv_qkv, ev_dn_a_log, ev_dn_dt_bias,
           ev_dn_norm_g, ev_lru_conv_w, ev_lru_conv_b, ev_lru_wa, ev_lru_ba, ev_lru_wx, ev_lru_bx, ev_lru_lambda,
           ev_w_out, ev_ffn_w_gate, ev_ffn_w_up, ev_ffn_w_down, od_w_in, od_gla_wg2, od_gla_bg, od_gla_norm_g,
           od_w_out, od_router_w, od_router_b, od_exp_w_gate, od_exp_w_up, od_exp_w_down, final_norm_g):
    b_, length, d = x.shape
    assert b_ == 1 and mod_w.shape[0] == 2, "this kernel implements the batch-1, depth-2 configuration"
    hw = DN_HEADS * DN_DH
    x_lat, x_ctx = x[0], ctx[0]

    cond8 = jnp.zeros((8, d), F32).at[0].set(c[0]).at[1].set(c_ctx)
    mod_all = adaln_all(cond8, mod_w, mod_b)

    w_in = ev_w_in[0]
    qkv_w, z0, ab0, xr0, gr0 = 3 * hw, 3 * hw, 4 * hw, 4 * hw + 4 * DN_HEADS, 5 * hw + 4 * DN_HEADS
    w_main = jnp.concatenate([w_in[:, :qkv_w], w_in[:, xr0:xr0 + hw], w_in[:, z0:z0 + hw], w_in[:, gr0:gr0 + hw]],
                             axis=1).astype(BF16)
    w_gate = jnp.zeros((d, LANES), BF16).at[:, :4 * DN_HEADS].set(w_in[:, ab0:ab0 + 4 * DN_HEADS].astype(BF16))
    pad16 = lambda t: jnp.zeros((1, LANES), F32).at[0, :2 * DN_HEADS].set(t.reshape(-1))
    ev = dict(
        w_main=w_main, w_gate=w_gate,
        conv_w=jnp.concatenate([ev_conv_qkv[0], ev_lru_conv_w[0]], axis=1),
        conv_b=jnp.concatenate([jnp.zeros((qkv_w,), F32), ev_lru_conv_b[0]]).reshape(1, -1),
        a_log=pad16(ev_dn_a_log[0]), dt_bias=pad16(ev_dn_dt_bias[0]),
        dn_norm_g=ev_dn_norm_g[0],
        lru_wa=ev_lru_wa[0].astype(BF16), lru_wx=ev_lru_wx[0].astype(BF16),
        lru_ba=ev_lru_ba[0].reshape(2, 1, hw), lru_bx=ev_lru_bx[0].reshape(2, 1, hw),
        lru_lam=ev_lru_lambda[0].reshape(2, 1, hw),
        w_out=ev_w_out[0].astype(BF16),
        ffn_gate=ev_ffn_w_gate.astype(BF16), ffn_up=ev_ffn_w_up.astype(BF16), ffn_down=ev_ffn_w_down.astype(BF16),
    )
    x_lat, x_ctx = _even_layer(x_lat, x_ctx, _mods(mod_all, 0, 0, d), _mods(mod_all, 0, 1, d),
                               norm1_g[0], norm2_g[0], ev)

    w_in = od_w_in[0]
    qk_w = od_gla_wg2.shape[-1]
    v_w = od_w_out.shape[1]
    main_w = 2 * qk_w + 2 * v_w
    wg_pad = jnp.zeros((2, LANES, qk_w), F32)
    for dirn in range(2):
        wg_pad = wg_pad.at[dirn, dirn * GLA_RANK:(dirn + 1) * GLA_RANK].set(od_gla_wg2[0, dirn])
    od = dict(
        w_main=w_in[:, :main_w].astype(BF16),
        w_gate=jnp.zeros((d, LANES), BF16).at[:, :2 * GLA_RANK].set(w_in[:, main_w:].astype(BF16)),
        wg_pad=wg_pad, bg=od_gla_bg[0].reshape(2, 1, qk_w), gla_norm_g=od_gla_norm_g[0],
        w_out=od_w_out[0].astype(BF16), router_w=od_router_w[0], router_b=od_router_b[0],
        exp_gate=od_exp_w_gate[0].astype(BF16), exp_up=od_exp_w_up[0].astype(BF16),
        exp_down=od_exp_w_down[0].astype(BF16),
        dk=qk_w // GLA_HEADS, dv=v_w // GLA_HEADS,
    )
    return _odd_layer_last(x_lat, x_ctx, _mods(mod_all, 1, 0, d), _mods(mod_all, 1, 1, d),
                           norm1_g[1], norm2_g[1], final_norm_g, od)[None]
```

```python
import functools
import math

import jax
import jax.numpy as jnp
from jax import lax
from jax.experimental import pallas as pl
from jax.experimental.pallas import tpu as pltpu

F32 = jnp.float32
BF16 = jnp.bfloat16
I32 = jnp.int32
HI = lax.Precision.HIGHEST

EPS = 1e-6
N_MOD = 6
GRID_W = 64
CHUNK = 64
SUB = 8
LANES = 128
ROW_TILE = 256
DN_HEADS = 8
DN_DH = 128
LRU_BLOCKS = 8
LRU_BW = 128
LRU_C = 8.0
LRU_UNROLL = 4
GLA_HEADS = 4
GLA_RANK = 16
GLA_TAU = 16.0
GLA_SAFE_STEP = 7.5
N_EXPERTS = 8
MOE_ROWS = 256
VMEM_LIMIT = 56 * 1024 * 1024

NT_DIMS = (((1,), (1,)), ((), ()))
TN_DIMS = (((0,), (0,)), ((), ()))


def _params(*sem):
    return pltpu.CompilerParams(dimension_semantics=sem, vmem_limit_bytes=VMEM_LIMIT)


def _softplus(x):
    return jnp.maximum(x, 0.0) + jnp.log1p(jnp.exp(-jnp.abs(x)))


def _silu(x):
    return x * jax.nn.sigmoid(x)


def _bdot(a, b):
    return jnp.dot(a.astype(BF16), b.astype(BF16), preferred_element_type=F32)


def _adaln_kernel(cond_ref, w_ref, b_ref, o_ref):
    s = _silu(cond_ref[...])
    o_ref[...] = jnp.dot(s, w_ref[...], precision=HI, preferred_element_type=F32) + b_ref[...]


def adaln_all(cond8, mod_w, mod_b):
    n_layers, d, n6 = mod_w.shape
    tn = 1024
    return pl.pallas_call(
        _adaln_kernel,
        grid=(n_layers, n6 // tn),
        in_specs=[pl.BlockSpec((8, d), lambda l, j: (0, 0)),
                  pl.BlockSpec((None, d, tn), lambda l, j: (l, 0, j)),
                  pl.BlockSpec((None, 1, tn), lambda l, j: (l, 0, j))],
        out_specs=pl.BlockSpec((None, 8, tn), lambda l, j: (l, 0, j)),
        out_shape=jax.ShapeDtypeStruct((n_layers, 8, n6), F32),
        compiler_params=_params("arbitrary", "arbitrary"),
        name="adaln",
    )(cond8, mod_w, mod_b.reshape(n_layers, 1, n6))


def _norm_mod_kernel(x_ref, g_ref, sh_ref, sc_ref, o_ref):
    x = x_ref[...]
    y = x * lax.rsqrt(jnp.mean(x * x, axis=-1, keepdims=True) + EPS)
    o_ref[...] = ((y * g_ref[...]) * (1.0 + sc_ref[...]) + sh_ref[...]).astype(o_ref.dtype)


def _raster_spec(rows, d):
    return pl.BlockSpec((rows, d), lambda c: (0, c))


def norm_mod(x, g, sh, sc):
    n, d = x.shape
    tm = min(ROW_TILE, n)
    vec = pl.BlockSpec((1, d), lambda i: (0, 0))
    row = pl.BlockSpec((tm, d), lambda i: (i, 0))
    return pl.pallas_call(
        _norm_mod_kernel, grid=(n // tm,), in_specs=[row, vec, vec, vec], out_specs=row,
        out_shape=jax.ShapeDtypeStruct((n, d), BF16),
        compiler_params=_params("parallel"), name="norm_mod",
    )(x, g.reshape(1, d), sh.reshape(1, d), sc.reshape(1, d))


def _dense_rows(r):
    return 2 * ROW_TILE if r % (2 * ROW_TILE) == 0 else min(ROW_TILE, r)


def _norm_proj_kernel(x_ref, g_ref, sh_ref, sc_ref, w_ref, wg_ref, p_ref, pg_ref, h_s, *, d):
    @pl.when(pl.program_id(1) == 0)
    def _():
        rows = x_ref.shape[0]
        for col in range(x_ref.shape[1] // d):
            x = x_ref[:, col * d:(col + 1) * d]
            y = x * lax.rsqrt(jnp.mean(x * x, axis=-1, keepdims=True) + EPS)
            h_s[col * rows:(col + 1) * rows, :] = ((y * g_ref[...]) * (1.0 + sc_ref[...]) + sh_ref[...]).astype(BF16)
        pg_ref[...] = jnp.dot(h_s[...], wg_ref[...], preferred_element_type=F32)

    p_ref[...] = jnp.dot(h_s[...], w_ref[...], preferred_element_type=F32)


def norm_proj(x, g, sh, sc, w_main, w_gate, *, raster=False, tn=1024):
    n, d = x.shape
    n_main = w_main.shape[1]
    tm = next(t for t in (4 * ROW_TILE, 2 * ROW_TILE, min(ROW_TILE, n)) if n % t == 0)
    if raster:
        rows = n // GRID_W
        n_col = max(tm // rows, 1)
        tm = n_col * rows
        x_in, x_spec = x.reshape(rows, GRID_W * d), pl.BlockSpec((rows, n_col * d), lambda i, j: (0, i))
    else:
        x_in, x_spec = x, pl.BlockSpec((tm, d), lambda i, j: (i, 0))
    vec = pl.BlockSpec((1, d), lambda i, j: (0, 0))
    return pl.pallas_call(
        functools.partial(_norm_proj_kernel, d=d), grid=(n // tm, n_main // tn),
        in_specs=[x_spec, vec, vec, vec, pl.BlockSpec((d, tn), lambda i, j: (0, j)),
                  pl.BlockSpec((d, LANES), lambda i, j: (0, 0))],
        out_specs=[pl.BlockSpec((tm, tn), lambda i, j: (i, j)), pl.BlockSpec((tm, LANES), lambda i, j: (i, 0))],
        out_shape=[jax.ShapeDtypeStruct((n, n_main), F32), jax.ShapeDtypeStruct((n, LANES), F32)],
        scratch_shapes=[pltpu.VMEM((tm, d), BF16)],
        compiler_params=_params("parallel", "arbitrary"), name="norm_proj",
    )(x_in, g.reshape(1, d), sh.reshape(1, d), sc.reshape(1, d), w_main, w_gate)


def _evprep_kernel(p_ref, prev_ref, next_ref, pg_ref, cw_ref, cb_ref, alog_ref, dtb_ref,
                   q_ref, k_ref, v_ref, xc_ref, gx_ref, gxt_ref, ext_ref, *, n_tiles):
    i = pl.program_id(0)
    tm, width = p_ref.shape
    ext_ref[8:8 + tm, :] = p_ref[...]
    ext_ref[0:8, :] = jnp.where(i > 0, prev_ref[...], 0.0)
    ext_ref[8 + tm:16 + tm, :] = jnp.where(i < n_tiles - 1, next_ref[...], 0.0)
    acc = ext_ref[pl.ds(6, tm), :] * cw_ref[0:1, :]
    for j in range(1, 4):
        acc = acc + ext_ref[pl.ds(6 + j, tm), :] * cw_ref[j:j + 1, :]
    acc = acc + cb_ref[...]
    qk_w = DN_HEADS * DN_DH
    for hd in range(DN_HEADS):
        for part, ref, scale in ((0, q_ref, DN_DH ** -0.5), (1, k_ref, 1.0)):
            lo = part * qk_w + hd * DN_DH
            t = _silu(acc[:, lo:lo + DN_DH])
            t = t * lax.rsqrt(jnp.sum(t * t, axis=-1, keepdims=True) + EPS)
            ref[:, hd * DN_DH:(hd + 1) * DN_DH] = t * scale
    v_ref[...] = _silu(acc[:, 2 * qk_w:3 * qk_w])
    xc_ref[...] = acc[:, 3 * qk_w:]
    pg = pg_ref[...]
    lane = lax.broadcasted_iota(I32, pg.shape, 1)
    g = -jnp.exp(alog_ref[...]) * _softplus(pg + dtb_ref[...])
    ri = lax.broadcasted_iota(I32, (tm, tm), 0)
    ci = lax.broadcasted_iota(I32, (tm, tm), 1)
    same = (ri // CHUNK) == (ci // CHUNK)
    cum_f = _dot01((same & (ci <= ri)).astype(BF16), g)
    cum_b = _dot01((same & (ci >= ri)).astype(BF16), g)
    tot = _dot01(same.astype(BF16), g)
    gc = jnp.where(lane < DN_HEADS, cum_f, cum_b)
    gx = jnp.where(lane < 2 * DN_HEADS, gc,
                   jnp.where(lane < 4 * DN_HEADS, jax.nn.sigmoid(pg), pltpu.roll(tot, 4 * DN_HEADS, 1)))
    gx_ref[...] = gx
    gxt_ref[...] = gx.T


def _dot01(m01, x):
    x1 = x.astype(BF16)
    r1 = x - x1.astype(F32)
    x2 = r1.astype(BF16)
    x3 = (r1 - x2.astype(F32)).astype(BF16)
    dot = lambda t: jnp.dot(m01, t, preferred_element_type=F32)
    return dot(x1) + dot(x2) + dot(x3)


def even_prep(p, pg, conv_w, conv_b, a_log, dt_bias):
    r = p.shape[0]
    width = conv_w.shape[1]
    tm = min(ROW_TILE, r)
    n_tiles = r // tm
    hb = tm // 8
    out_w = DN_HEADS * DN_DH
    row = lambda w: pl.BlockSpec((tm, w), lambda i: (i, 0))
    vec = lambda w: pl.BlockSpec((1, w), lambda i: (0, 0))
    return pl.pallas_call(
        functools.partial(_evprep_kernel, n_tiles=n_tiles),
        grid=(n_tiles,),
        in_specs=[row(width),
                  pl.BlockSpec((8, width), lambda i: (jnp.maximum(i * hb - 1, 0), 0)),
                  pl.BlockSpec((8, width), lambda i: (jnp.minimum((i + 1) * hb, r // 8 - 1), 0)),
                  row(LANES),
                  pl.BlockSpec((4, width), lambda i: (0, 0)), vec(width), vec(LANES), vec(LANES)],
        out_specs=[row(out_w), row(out_w), row(out_w), row(out_w), row(LANES),
                   pl.BlockSpec((LANES, tm), lambda i: (0, i))],
        out_shape=[jax.ShapeDtypeStruct((r, out_w), F32)] * 4
        + [jax.ShapeDtypeStruct((r, LANES), F32), jax.ShapeDtypeStruct((LANES, r), F32)],
        scratch_shapes=[pltpu.VMEM((tm + 16, width), F32)],
        compiler_params=_params("parallel"), name="even_prep",
    )(p, p, p, pg, conv_w, conv_b, a_log, dt_bias)


DN_BASE = 16
DN_HEADS_PER_STEP = 4


def _dnprep_kernel(q_ref, k_ref, v_ref, gx_ref, gxt_ref, u_ref, w_ref, qd_ref, kd_ref, qk_ref, gl_ref, *, n_chunks):
    c = CHUNK
    tm = q_ref.shape[0]
    ri = lax.broadcasted_iota(I32, (tm, tm), 0)
    ci = lax.broadcasted_iota(I32, (tm, tm), 1)
    same = lambda s: (ri // s) == (ci // s)
    eye = (ri == ci).astype(F32)
    lane = lax.broadcasted_iota(I32, (tm, LANES), 1)
    gx = gx_ref[...]
    pick = lambda idx: jnp.sum(jnp.where(lane == idx, gx, 0.0), axis=1, keepdims=True)
    chains = []
    for hh in range(DN_HEADS_PER_STEP):
        hd = pl.program_id(1) * DN_HEADS_PER_STEP + hh
        cols = slice(hh * DN_DH, (hh + 1) * DN_DH)
        q = q_ref[:, cols]
        k = k_ref[:, cols]
        kb16 = k.astype(BF16)
        gram_k = lax.dot_general(kb16, kb16, NT_DIMS, preferred_element_type=F32)
        gram_q = lax.dot_general(q.astype(BF16), kb16, NT_DIMS, preferred_element_type=F32)
        for d in range(2):
            incl = same(c) & ((ci >= ri) if d else (ci <= ri))
            gcol = pick(d * DN_HEADS + hd)
            bcol = pick((2 + d) * DN_HEADS + hd)
            tcol = pick((4 + d) * DN_HEADS + hd)
            grow = gxt_ref[pl.ds(d * DN_HEADS + hd, 1), :]
            decay = jnp.where(incl, jnp.exp(gcol - grow), 0.0)
            a = jnp.where(ri == ci, 0.0, gram_k * bcol * decay)
            eg = jnp.exp(gcol)
            qk = (gram_q * decay).astype(BF16)
            for n in range(n_chunks):
                qk_ref[d, hh, n * c:(n + 1) * c, :] = qk[n * c:(n + 1) * c, n * c:(n + 1) * c]
                gl_ref[d, hh, n] = jnp.broadcast_to(jnp.exp(tcol[n * c:n * c + 1, :]), (8, LANES))
            qd_ref[d, :, cols] = (q * eg).astype(BF16)
            kd_ref[d, :, cols] = (k * jnp.exp(tcol - gcol)).astype(BF16)
            diag = jnp.where(same(DN_BASE), a, 0.0)
            chains.append(dict(d=d, cols=cols, a=a, t=eye - diag, p=diag, scale=bcol, scale_k=bcol * eg))
    size = 2
    while size < DN_BASE:
        for ch in chains:
            ch["p"] = _bdot(ch["p"], ch["p"])
        for ch in chains:
            ch["t"] = ch["t"] + _bdot(ch["t"], ch["p"])
        size *= 2
    size = DN_BASE
    while size < c:
        couple = same(2 * size) & ~same(size)
        for ch in chains:
            ch["et"] = _bdot(jnp.where(couple, ch["a"], 0.0), ch["t"])
        for ch in chains:
            ch["t"] = ch["t"] - _bdot(ch["t"], ch["et"])
        size *= 2
    for ch in chains:
        d, cols = ch["d"], ch["cols"]
        x = _bdot(ch["t"], jnp.concatenate([v_ref[:, cols] * ch["scale"], k_ref[:, cols] * ch["scale_k"]], axis=1))
        u_ref[d, :, cols] = x[:, :DN_DH]
        w_ref[d, :, cols] = x[:, DN_DH:].astype(BF16)


def deltanet_prep(q, k, v, gx, gxt):
    r = q.shape[0]
    tm = min(ROW_TILE, r)
    n_chunks = tm // CHUNK
    hw = DN_HEADS * DN_DH
    hps = DN_HEADS_PER_STEP
    head = pl.BlockSpec((tm, hps * DN_DH), lambda i, h: (i, h))
    dhead = pl.BlockSpec((2, tm, hps * DN_DH), lambda i, h: (0, i, h))
    return pl.pallas_call(
        functools.partial(_dnprep_kernel, n_chunks=n_chunks),
        grid=(r // tm, DN_HEADS // hps),
        in_specs=[head, head, head, pl.BlockSpec((tm, LANES), lambda i, h: (i, 0)),
                  pl.BlockSpec((LANES, tm), lambda i, h: (0, i))],
        out_specs=[dhead, dhead, dhead, dhead,
                   pl.BlockSpec((2, hps, tm, CHUNK), lambda i, h: (0, h, i, 0)),
                   pl.BlockSpec((2, hps, n_chunks, 8, LANES), lambda i, h: (0, h, i, 0, 0))],
        out_shape=[jax.ShapeDtypeStruct((2, r, hw), F32),
                   jax.ShapeDtypeStruct((2, r, hw), BF16),
                   jax.ShapeDtypeStruct((2, r, hw), BF16),
                   jax.ShapeDtypeStruct((2, r, hw), BF16),
                   jax.ShapeDtypeStruct((2, DN_HEADS, r, CHUNK), BF16),
                   jax.ShapeDtypeStruct((2, DN_HEADS, r // CHUNK, 8, LANES), F32)],
        compiler_params=_params("parallel", "parallel"), name="deltanet_prep",
    )(q, k, v, gx, gxt)


def _dnscan_kernel(*refs, n_steps):
    ins = (refs[0:6], refs[6:12])
    s0_ref, o_refs, sf_ref, s_ref = refs[12], refs[13:15], refs[15], refs[16]
    step = pl.program_id(0)

    @pl.when(step == 0)
    def _():
        s_ref[...] = s0_ref[...]

    c = CHUNK
    n_sub = ins[0][0].shape[0] // c
    col = lambda hd: slice(hd * DN_DH, (hd + 1) * DN_DH)
    dot = lambda a, b: jnp.dot(a, b, preferred_element_type=F32)
    for t in range(n_sub):
        chains = []
        for d in range(2):
            u_ref, w_ref, qd_ref, kd_ref, qk_ref, gl_ref = ins[d]
            n = n_sub - 1 - t if d else t
            rows = slice(n * c, (n + 1) * c)
            for hd in range(DN_HEADS):
                chains.append(dict(d=d, hd=hd, rows=rows, u=u_ref.at[rows, col(hd)], w=w_ref.at[rows, col(hd)],
                                   qd=qd_ref.at[rows, col(hd)], kd=kd_ref.at[rows, col(hd)],
                                   qk=qk_ref.at[hd, rows, :], gl=gl_ref.at[hd, n, 0:1, :]))
        for ch in chains:
            ch["s"] = s_ref[ch["d"], ch["hd"]]
            ch["sb"] = ch["s"].astype(BF16)
        for ch in chains:
            ch["ws"] = dot(ch["w"][...], ch["sb"])
        for ch in chains:
            ch["qs"] = dot(ch["qd"][...], ch["sb"])
        for ch in chains:
            ch["vb"] = (ch["u"][...] - ch["ws"]).astype(BF16)
        for ch in chains:
            o_refs[ch["d"]][ch["rows"], col(ch["hd"])] = ch["qs"] + dot(ch["qk"][...], ch["vb"])
        for ch in chains:
            ch["ds"] = lax.dot_general(ch["kd"][...], ch["vb"], TN_DIMS, preferred_element_type=F32)
        for ch in chains:
            s_ref[ch["d"], ch["hd"]] = ch["s"] * ch["gl"][...] + ch["ds"]

    @pl.when(step == n_steps - 1)
    def _():
        sf_ref[...] = s_ref[...]


def deltanet_scan(u, w, qd, kd, qk, gl, s0):
    r = u.shape[1]
    hw = DN_HEADS * DN_DH
    tm = min(ROW_TILE, r)
    n_steps = r // tm
    state = pl.BlockSpec((2, DN_HEADS, DN_DH, DN_DH), lambda i: (0, 0, 0, 0))
    in_specs, out_specs = [], []
    for d in range(2):
        idx = (lambda i: n_steps - 1 - i) if d else (lambda i: i)
        big = pl.BlockSpec((None, tm, hw), lambda i, d=d, idx=idx: (d, idx(i), 0))
        in_specs += [big, big, big, big,
                     pl.BlockSpec((None, DN_HEADS, tm, CHUNK), lambda i, d=d, idx=idx: (d, 0, idx(i), 0)),
                     pl.BlockSpec((None, DN_HEADS, tm // CHUNK, 8, LANES), lambda i, d=d, idx=idx: (d, 0, idx(i), 0, 0))]
        out_specs.append(pl.BlockSpec((tm, hw), lambda i, idx=idx: (idx(i), 0)))
    return pl.pallas_call(
        functools.partial(_dnscan_kernel, n_steps=n_steps),
        grid=(n_steps,),
        in_specs=in_specs + [state],
        out_specs=out_specs + [state],
        out_shape=[jax.ShapeDtypeStruct((r, hw), F32), jax.ShapeDtypeStruct((r, hw), F32),
                   jax.ShapeDtypeStruct((2, DN_HEADS, DN_DH, DN_DH), F32)],
        scratch_shapes=[pltpu.VMEM((2, DN_HEADS, DN_DH, DN_DH), F32)],
        compiler_params=_params("arbitrary"), name="deltanet_scan",
    )(u, w, qd, kd, qk, gl, u, w, qd, kd, qk, gl, s0)


def _lru_kernel(xf_ref, xb_ref, wa_ref, wx_ref, ba_ref, bx_ref, lam_ref, h0_ref, hf_ref, hb_ref, hl_ref,
                a_s, b_s, carry_s, *, n_steps):
    step = pl.program_id(0)

    @pl.when(step == 0)
    def _():
        carry_s[...] = h0_ref[...]

    tm = xf_ref.shape[0]
    for d, xc_ref in enumerate((xf_ref, xb_ref)):
        sp = _softplus(-lam_ref[d])
        for n in range(LRU_BLOCKS):
            cols = slice(n * LRU_BW, (n + 1) * LRU_BW)
            xb = xc_ref[:, cols]
            xbb = xb.astype(BF16)
            r = jax.nn.sigmoid(jnp.dot(xbb, wa_ref[d, n], preferred_element_type=F32) + ba_ref[d, :, cols])
            gi = jax.nn.sigmoid(jnp.dot(xbb, wx_ref[d, n], preferred_element_type=F32) + bx_ref[d, :, cols])
            log_a = -LRU_C * r * sp[:, cols]
            a = jnp.exp(log_a)
            a_s[d, :, cols] = a
            b_s[d, :, cols] = jnp.sqrt(-jnp.tanh(log_a) * (a * a + 1.0)) * (gi * xb)

    rid = lax.broadcasted_iota(I32, (8, a_s.shape[2]), 0)
    n_groups = tm // 8
    h_refs = (hf_ref, hb_ref)

    def group(gidx, carries):
        out = []
        for d in range(2):
            g = (n_groups - 1 - gidx) if d else gidx
            base = pl.multiple_of(g * 8, 8)
            a = a_s[d, pl.ds(base, 8), :]
            b = b_s[d, pl.ds(base, 8), :]
            for sh in (1, 2, 4):
                if d:
                    keep = rid < 8 - sh
                    a_n = jnp.where(keep, pltpu.roll(a, 8 - sh, 0), 1.0)
                    b_n = jnp.where(keep, pltpu.roll(b, 8 - sh, 0), 0.0)
                else:
                    keep = rid >= sh
                    a_n = jnp.where(keep, pltpu.roll(a, sh, 0), 1.0)
                    b_n = jnp.where(keep, pltpu.roll(b, sh, 0), 0.0)
                b = a * b_n + b
                a = a * a_n
            h = a * carries[d] + b
            h_refs[d][pl.ds(base, 8), :] = h
            edge = h[0:1, :] if d else h[7:8, :]
            out.append(jnp.broadcast_to(edge, h.shape))
        return tuple(out)

    carries = lax.fori_loop(0, n_groups, group, (carry_s[0], carry_s[1]), unroll=LRU_UNROLL)
    carry_s[0] = carries[0]
    carry_s[1] = carries[1]

    @pl.when(step == n_steps - 1)
    def _():
        hl_ref[...] = carry_s[...]


def lru_scan(xc, wa, wx, ba, bx, lam, h0):
    r, width = xc.shape
    tm = min(ROW_TILE, r)
    n_steps = r // tm
    vec = pl.BlockSpec((2, 1, width), lambda i: (0, 0, 0))
    wspec = pl.BlockSpec((2, LRU_BLOCKS, LRU_BW, LRU_BW), lambda i: (0, 0, 0, 0))
    st = pl.BlockSpec((2, 8, width), lambda i: (0, 0, 0))
    fwd = pl.BlockSpec((tm, width), lambda i: (i, 0))
    bwd = pl.BlockSpec((tm, width), lambda i: (n_steps - 1 - i, 0))
    return pl.pallas_call(
        functools.partial(_lru_kernel, n_steps=n_steps),
        grid=(n_steps,),
        in_specs=[fwd, bwd, wspec, wspec, vec, vec, vec, st],
        out_specs=[fwd, bwd, st],
        out_shape=[jax.ShapeDtypeStruct((r, width), F32), jax.ShapeDtypeStruct((r, width), F32),
                   jax.ShapeDtypeStruct((2, 8, width), F32)],
        scratch_shapes=[pltpu.VMEM((2, tm, width), F32), pltpu.VMEM((2, tm, width), F32),
                        pltpu.VMEM((2, 8, width), F32)],
        compiler_params=_params("arbitrary"), name="lru_scan",
    )(xc, xc, wa, wx, ba, bx, lam, h0)


def _gelu_tanh(x):
    return 0.5 * x * (1.0 + jnp.tanh(math.sqrt(2.0 / math.pi) * (x + 0.044715 * (x * x * x))))


def _evfin_kernel(of_ref, ob_ref, hf_ref, hb_ref, zg_ref, ng_ref, wout_ref, x_ref, gt_ref, g2_ref, sh2_ref, sc2_ref,
                  o_ref, h2_ref, mix_s):
    hw = DN_HEADS * DN_DH
    for hd in range(DN_HEADS):
        cols = slice(hd * DN_DH, (hd + 1) * DN_DH)
        o = of_ref[:, cols] + ob_ref[:, cols]
        y = o * lax.rsqrt(jnp.mean(o * o, axis=-1, keepdims=True) + EPS) * ng_ref[...]
        mix_s[:, cols] = (y * _silu(zg_ref[:, cols])).astype(BF16)
    mix_s[:, hw:] = ((hf_ref[...] + hb_ref[...]) * _gelu_tanh(zg_ref[:, hw:])).astype(BF16)
    y = jnp.dot(mix_s[...], wout_ref[...], preferred_element_type=F32)
    x_new = x_ref[...] + gt_ref[...] * y
    o_ref[...] = x_new
    n = x_new * lax.rsqrt(jnp.mean(x_new * x_new, axis=-1, keepdims=True) + EPS)
    h2_ref[...] = ((n * g2_ref[...]) * (1.0 + sc2_ref[...]) + sh2_ref[...]).astype(BF16)


def even_finish(o_f, o_b, h_f, h_b, p, norm_g, w_out, x, gate, norm2_g, sh2, sc2):
    r, d = x.shape
    tm = min(ROW_TILE, r)
    hw = DN_HEADS * DN_DH
    row = lambda w: pl.BlockSpec((tm, w), lambda i: (i, 0))
    return pl.pallas_call(
        _evfin_kernel, grid=(r // tm,),
        in_specs=[row(hw), row(hw), row(hw), row(hw),
                  pl.BlockSpec((tm, 2 * hw), lambda i: (i, 2)),
                  pl.BlockSpec((1, DN_DH), lambda i: (0, 0)),
                  pl.BlockSpec(w_out.shape, lambda i: (0, 0)),
                  row(d)] + [pl.BlockSpec((1, d), lambda i: (0, 0))] * 4,
        out_specs=[row(d), row(d)],
        out_shape=[jax.ShapeDtypeStruct((r, d), F32), jax.ShapeDtypeStruct((r, d), BF16)],
        scratch_shapes=[pltpu.VMEM((tm, 2 * hw), BF16)],
        compiler_params=_params("parallel"), name="even_finish",
    )(o_f, o_b, h_f, h_b, p, norm_g.reshape(1, DN_DH), w_out, x, gate.reshape(1, d),
      norm2_g.reshape(1, d), sh2.reshape(1, d), sc2.reshape(1, d))


def _odfin_kernel(of_ref, ob_ref, go_ref, ng_ref, wout_ref, x_ref, gt_ref, o_ref, mix_s, *, dv):
    for hd in range(GLA_HEADS):
        cols = slice(hd * dv, (hd + 1) * dv)
        o = of_ref[:, cols] + ob_ref[:, cols]
        y = o * lax.rsqrt(jnp.mean(o * o, axis=-1, keepdims=True) + EPS) * ng_ref[...]
        mix_s[:, cols] = (y * _silu(go_ref[:, cols])).astype(BF16)
    y = jnp.dot(mix_s[...], wout_ref[...], preferred_element_type=F32)
    o_ref[...] = x_ref[...] + gt_ref[...] * y


def odd_finish(o_f, o_b, p, norm_g, w_out, x, gate):
    r, d = x.shape
    rows = r // GRID_W
    vw = o_f.shape[1]
    dv = vw // GLA_HEADS
    row = lambda w: pl.BlockSpec((rows, w), lambda c: (c, 0))
    return pl.pallas_call(
        functools.partial(_odfin_kernel, dv=dv), grid=(GRID_W,),
        in_specs=[row(vw), row(vw),
                  pl.BlockSpec((rows, vw), lambda c: (c, 2)),
                  pl.BlockSpec((1, dv), lambda c: (0, 0)),
                  pl.BlockSpec(w_out.shape, lambda c: (0, 0)),
                  _raster_spec(rows, d), pl.BlockSpec((1, d), lambda c: (0, 0))],
        out_specs=_raster_spec(rows, d),
        out_shape=jax.ShapeDtypeStruct((rows, GRID_W * d), F32),
        scratch_shapes=[pltpu.VMEM((rows, vw), BF16)],
        compiler_params=_params("parallel"), name="odd_finish",
    )(o_f, o_b, p, norm_g.reshape(1, dv), w_out, x.reshape(rows, GRID_W * d), gate.reshape(1, d)).reshape(r, d)


def _gla_kernel(*refs, n_steps, dk):
    ins = (refs[0:6], refs[6:12])
    s0_ref, o_refs, sf_ref, s_ref, gc_s = refs[12], refs[13:15], refs[15], refs[16], refs[17]
    step = pl.program_id(1)

    @pl.when(step == 0)
    def _():
        s_ref[...] = s0_ref[...]

    c = CHUNK
    tm = ins[0][0].shape[0]
    n_chunks = tm // c
    n_sub = c // SUB
    ri = lax.broadcasted_iota(I32, (tm, tm), 0)
    ci = lax.broadcasted_iota(I32, (tm, tm), 1)
    r64 = lax.broadcasted_iota(I32, (c, c), 0)
    c64 = lax.broadcasted_iota(I32, (c, c), 1)
    rr = lax.broadcasted_iota(I32, (c, 1), 0) % SUB
    units = []
    g_min = None
    for d in range(2):
        q_ref, k_ref, v_ref, gd_ref, wg_ref, bg_ref = ins[d]
        tri = (((ri // c) == (ci // c)) & ((ci >= ri) if d else (ci <= ri))).astype(BF16)
        logit = _bdot(gd_ref[...], wg_ref[...]) + bg_ref[...]
        g = -_softplus(-logit) * (1.0 / GLA_TAU)
        gc = _dot01(tri, g)
        gc_s[d] = gc
        g_min = jnp.min(g) if g_min is None else jnp.minimum(g_min, jnp.min(g))
        q = q_ref[...] * dk ** -0.5
        for n in range(n_chunks):
            rows = slice(n * c, (n + 1) * c)
            units.append(dict(d=d, n=n, rows=rows, q=q[rows], k=k_ref[rows, :], g=g[rows], gc=gc[rows],
                              att=jnp.zeros((c, c), F32)))
    size = c // 2
    while size >= SUB:
        r_hi, c_hi = (r64 & size) != 0, (c64 & size) != 0
        same = (r64 // (2 * size)) == (c64 // (2 * size))
        pair = (same & r_hi & ~c_hi, same & ~r_hi & c_hi)
        for u in units:
            gc = u["gc"]
            pieces = []
            for b in range(c // (2 * size)):
                mid = b * 2 * size + size
                edge = gc[mid:mid + 1] if u["d"] else gc[mid - 1:mid]
                pieces.append(jnp.broadcast_to(edge, (2 * size, dk)))
            edge = pieces[0] if len(pieces) == 1 else jnp.concatenate(pieces, axis=0)
            qs = (u["q"] * jnp.exp(gc - edge)).astype(BF16)
            ks = (u["k"] * jnp.exp(edge - gc)).astype(BF16)
            u["att"] = u["att"] + jnp.where(pair[u["d"]],
                                            lax.dot_general(qs, ks, NT_DIMS, preferred_element_type=F32), 0.0)
        size //= 2

    def diag_direct():
        blocks = [jnp.zeros((c, c), F32) for _ in units]
        for jj in range(SUB):
            here = c64 == (r64 // SUB) * SUB + jj
            seen = (here & (rr >= jj), here & (rr <= jj))
            for i, u in enumerate(units):
                rep = lambda ref: jnp.concatenate(
                    [jnp.broadcast_to(ref[pl.ds(u["n"] * c + b * SUB + jj, 1), :], (SUB, dk)) for b in range(n_sub)],
                    axis=0)
                col = jnp.sum(u["q"] * rep(ins[u["d"]][1]) * jnp.exp(u["gc"] - rep(gc_s.at[u["d"]])),
                              axis=-1, keepdims=True)
                blocks[i] = jnp.where(seen[u["d"]], col, blocks[i])
        return tuple(blocks)

    def diag_factored():
        blocks = []
        same = (r64 // SUB) == (c64 // SUB)
        inside = (same & (c64 <= r64), same & (c64 >= r64))
        for u in units:
            before = (u["gc"] - u["g"]).reshape(n_sub, SUB, dk)
            edge = before[:, SUB - 1:SUB, :] if u["d"] else before[:, 0:1, :]
            edge = jnp.broadcast_to(edge, before.shape).reshape(c, dk)
            qs = (u["q"] * jnp.exp(u["gc"] - edge)).astype(BF16)
            ks = (u["k"] * jnp.exp(edge - u["gc"])).astype(BF16)
            blocks.append(jnp.where(inside[u["d"]],
                                    lax.dot_general(qs, ks, NT_DIMS, preferred_element_type=F32), 0.0))
        return tuple(blocks)

    diag = lax.cond(g_min >= -GLA_SAFE_STEP, diag_factored, diag_direct)
    for u in units:
        u["vb"] = ins[u["d"]][2][u["rows"], :].astype(BF16)
        u["tot"] = u["gc"][0:1] if u["d"] else u["gc"][c - 1:c]
    for i, u in enumerate(units):
        u["o"] = jnp.dot((u["att"] + diag[i]).astype(BF16), u["vb"], preferred_element_type=F32)
    for u in units:
        u["qd"] = (u["q"] * jnp.exp(u["gc"])).astype(BF16)
        u["ds"] = lax.dot_general(u["vb"], (u["k"] * jnp.exp(u["tot"] - u["gc"])).astype(BF16), TN_DIMS,
                                  preferred_element_type=F32)
    for t in range(n_chunks):
        for d in range(2):
            u = units[d * n_chunks + (n_chunks - 1 - t if d else t)]
            s = s_ref[d]
            o_refs[d][u["rows"], :] = u["o"] + lax.dot_general(u["qd"], s.astype(BF16), NT_DIMS,
                                                               preferred_element_type=F32)
            s_ref[d] = s * jnp.exp(u["tot"]) + u["ds"]

    @pl.when(step == n_steps - 1)
    def _():
        sf_ref[...] = s_ref[...]


def gla_scan(p, pg, wg_pad, bg, s0, *, dk, dv):
    r = p.shape[0]
    tm = min(ROW_TILE, r)
    n_steps = r // tm
    qk_blocks = GLA_HEADS
    v_block0 = 2 * GLA_HEADS * dk // dv
    state = pl.BlockSpec((2, None, dv, dk), lambda h, i: (0, h, 0, 0))
    in_specs, out_specs = [], []
    for d in range(2):
        idx = (lambda i: n_steps - 1 - i) if d else (lambda i: i)
        in_specs += [pl.BlockSpec((tm, dk), lambda h, i, idx=idx: (idx(i), h)),
                     pl.BlockSpec((tm, dk), lambda h, i, idx=idx: (idx(i), qk_blocks + h)),
                     pl.BlockSpec((tm, dv), lambda h, i, idx=idx: (idx(i), v_block0 + h)),
                     pl.BlockSpec((tm, LANES), lambda h, i, idx=idx: (idx(i), 0)),
                     pl.BlockSpec((None, LANES, dk), lambda h, i, d=d: (d, 0, h)),
                     pl.BlockSpec((None, 1, dk), lambda h, i, d=d: (d, 0, h))]
        out_specs.append(pl.BlockSpec((tm, dv), lambda h, i, idx=idx: (idx(i), h)))
    args = (p, p, p, pg, wg_pad, bg)
    return pl.pallas_call(
        functools.partial(_gla_kernel, n_steps=n_steps, dk=dk),
        grid=(GLA_HEADS, n_steps),
        in_specs=in_specs + [state],
        out_specs=out_specs + [state],
        out_shape=[jax.ShapeDtypeStruct((r, GLA_HEADS * dv), F32), jax.ShapeDtypeStruct((r, GLA_HEADS * dv), F32),
                   jax.ShapeDtypeStruct((2, GLA_HEADS, dv, dk), F32)],
        scratch_shapes=[pltpu.VMEM((2, dv, dk), F32), pltpu.VMEM((2, tm, dk), F32)],
        compiler_params=_params("parallel", "arbitrary"), name="gla_scan",
    )(*args, *args, s0)


def _when_block_used(nu_ref, o_ref, body):
    used = pl.program_id(1) < nu_ref[0]

    @pl.when(used)
    def _():
        body()

    @pl.when(jnp.logical_not(used))
    def _():
        o_ref[...] = jnp.zeros_like(o_ref)


FFN_TN = 1408


def _ffn1_kernel(be_ref, nu_ref, x_ref, wg_ref, wu_ref, o_ref, w_s):
    i = pl.program_id(1)
    tn = o_ref.shape[1]

    @pl.when((i == 0) | (be_ref[i] != be_ref[jnp.maximum(i - 1, 0)]))
    def _():
        w_s[:, :tn] = wg_ref[...]
        w_s[:, tn:] = wu_ref[...]

    def body():
        ab = jnp.dot(x_ref[...], w_s[...], preferred_element_type=F32)
        o_ref[...] = (_silu(ab[:, :tn]) * ab[:, tn:]).astype(o_ref.dtype)

    _when_block_used(nu_ref, o_ref, body)


def ffn_up(x, w_gate, w_up, block_expert, n_used, *, tm):
    r, d = x.shape
    hidden = w_gate.shape[2]
    tn = FFN_TN
    wspec = pl.BlockSpec((None, d, tn), lambda j, i, be, nu: (be[i], 0, j))
    return pl.pallas_call(
        _ffn1_kernel,
        grid_spec=pltpu.PrefetchScalarGridSpec(
            num_scalar_prefetch=2, grid=(hidden // tn, r // tm),
            in_specs=[pl.BlockSpec((tm, d), lambda j, i, be, nu: (i, 0)), wspec, wspec],
            out_specs=pl.BlockSpec((tm, tn), lambda j, i, be, nu: (i, j)),
            scratch_shapes=[pltpu.VMEM((d, 2 * tn), BF16)]),
        out_shape=jax.ShapeDtypeStruct((r, hidden), BF16),
        compiler_params=_params("parallel", "arbitrary"), name="ffn_up",
    )(block_expert, n_used, x, w_gate, w_up)


def _ffn2_res_kernel(be_ref, nu_ref, h_ref, w_ref, x_ref, gt_ref, o_ref):
    def body():
        y = jnp.dot(h_ref[...], w_ref[...], preferred_element_type=F32)
        o_ref[...] = x_ref[...] + gt_ref[...] * y

    _when_block_used(nu_ref, o_ref, body)


def _ffn2_scale_kernel(be_ref, nu_ref, h_ref, w_ref, sw_ref, o_ref):
    def body():
        y = jnp.dot(h_ref[...], w_ref[...], preferred_element_type=F32)
        o_ref[...] = (y * sw_ref[...]).astype(o_ref.dtype)

    _when_block_used(nu_ref, o_ref, body)


def ffn_down_residual(h, w_down, block_expert, n_used, x, gate, *, tm, tn=1024):
    r, hidden = h.shape
    d = w_down.shape[2]
    return pl.pallas_call(
        _ffn2_res_kernel,
        grid_spec=pltpu.PrefetchScalarGridSpec(
            num_scalar_prefetch=2, grid=(d // tn, r // tm),
            in_specs=[pl.BlockSpec((tm, hidden), lambda j, i, be, nu: (i, 0)),
                      pl.BlockSpec((None, hidden, tn), lambda j, i, be, nu: (be[i], 0, j)),
                      pl.BlockSpec((tm, tn), lambda j, i, be, nu: (i, j)),
                      pl.BlockSpec((1, tn), lambda j, i, be, nu: (0, j))],
            out_specs=pl.BlockSpec((tm, tn), lambda j, i, be, nu: (i, j))),
        out_shape=jax.ShapeDtypeStruct((r, d), F32),
        compiler_params=_params("parallel", "arbitrary"), name="ffn_down_residual",
    )(block_expert, n_used, h, w_down, x, gate.reshape(1, d))


def ffn_down_scaled(h, w_down, block_expert, n_used, slot_w, *, tn=1024):
    r, hidden = h.shape
    d = w_down.shape[2]
    tm = min(MOE_ROWS, r)
    return pl.pallas_call(
        _ffn2_scale_kernel,
        grid_spec=pltpu.PrefetchScalarGridSpec(
            num_scalar_prefetch=2, grid=(d // tn, r // tm),
            in_specs=[pl.BlockSpec((tm, hidden), lambda j, i, be, nu: (i, 0)),
                      pl.BlockSpec((None, hidden, tn), lambda j, i, be, nu: (be[i], 0, j)),
                      pl.BlockSpec((tm, 1), lambda j, i, be, nu: (i, 0))],
            out_specs=pl.BlockSpec((tm, tn), lambda j, i, be, nu: (i, j))),
        out_shape=jax.ShapeDtypeStruct((r, d), BF16),
        compiler_params=_params("parallel", "arbitrary"), name="ffn_down_scaled",
    )(block_expert, n_used, h, w_down, slot_w.reshape(r, 1))


def _router_kernel(x_ref, g_ref, sh_ref, sc_ref, rw_ref, rb_ref, h_ref, idx_ref, wt_ref, rank_ref, cnt_ref):
    x = x_ref[...]
    y = x * lax.rsqrt(jnp.mean(x * x, axis=-1, keepdims=True) + EPS)
    h = (y * g_ref[...]) * (1.0 + sc_ref[...]) + sh_ref[...]
    hb = h.astype(BF16)
    h_ref[...] = hb
    logits = jnp.dot(hb, rw_ref[...], preferred_element_type=F32) + rb_ref[...]
    lane = lax.broadcasted_iota(I32, logits.shape, 1)
    neg = jnp.float32(-jnp.inf)
    logits = jnp.where(lane < N_EXPERTS, logits, neg)
    m0 = jnp.max(logits, axis=-1, keepdims=True)
    i0 = jnp.min(jnp.where(logits == m0, lane, LANES), axis=-1, keepdims=True)
    rest = jnp.where(lane == i0, neg, logits)
    m1 = jnp.max(rest, axis=-1, keepdims=True)
    i1 = jnp.min(jnp.where(rest == m1, lane, LANES), axis=-1, keepdims=True)
    e1 = jnp.exp(m1 - m0)
    w0 = 1.0 / (1.0 + e1)
    idx_ref[...] = jnp.where(lane == 0, i0, jnp.where(lane == 1, i1, 0))
    wt_ref[...] = jnp.where(lane == 0, w0, jnp.where(lane == 1, e1 * w0, 0.0))
    tm = x.shape[0]
    hot = (lane == i0) | (lane == i1)
    earlier = (lax.broadcasted_iota(I32, (tm, tm), 1) < lax.broadcasted_iota(I32, (tm, tm), 0)).astype(BF16)
    before = jnp.dot(earlier, hot.astype(BF16), preferred_element_type=F32)
    r0 = jnp.sum(jnp.where(lane == i0, before, 0.0), axis=-1, keepdims=True)
    r1 = jnp.sum(jnp.where(lane == i1, before, 0.0), axis=-1, keepdims=True)
    rank_ref[...] = jnp.where(lane == 0, r0, jnp.where(lane == 1, r1, 0.0)).astype(I32)
    cnt_ref[...] = jnp.broadcast_to(jnp.sum(hot.astype(F32), axis=0, keepdims=True), cnt_ref.shape).astype(I32)


def route(x, g, sh, sc, router_w, router_b):
    t, d = x.shape
    tm = min(ROW_TILE, t)
    vec = pl.BlockSpec((1, d), lambda i: (0, 0))
    row = lambda w: pl.BlockSpec((tm, w), lambda i: (i, 0))
    rw = jnp.zeros((d, LANES), BF16).at[:, :N_EXPERTS].set(router_w.astype(BF16))
    rb = jnp.zeros((1, LANES), F32).at[0, :N_EXPERTS].set(router_b)
    h, idx, wt, rank, cnt = pl.pallas_call(
        _router_kernel, grid=(t // tm,),
        in_specs=[row(d), vec, vec, vec, pl.BlockSpec((d, LANES), lambda i: (0, 0)),
                  pl.BlockSpec((1, LANES), lambda i: (0, 0))],
        out_specs=[row(d), row(LANES), row(LANES), row(LANES), pl.BlockSpec((8, LANES), lambda i: (i, 0))],
        out_shape=[jax.ShapeDtypeStruct((t, d), BF16), jax.ShapeDtypeStruct((t, LANES), I32),
                   jax.ShapeDtypeStruct((t, LANES), F32), jax.ShapeDtypeStruct((t, LANES), I32),
                   jax.ShapeDtypeStruct((t // tm * 8, LANES), I32)],
        compiler_params=_params("parallel"), name="route",
    )(x, g.reshape(1, d), sh.reshape(1, d), sc.reshape(1, d), rw, rb)
    return h, idx[:, :2], wt[:, :2], rank[:, :2], cnt.reshape(t // tm, 8, LANES)[:, 0, :N_EXPERTS]


GATHER_WINDOW_TILES = 6


def _gather_kernel(pb_ref, ps_ref, pn_ref, pf_ref, h_ref, dest_ref, wt_ref, o_ref, sw_ref, acc_ref, swacc_ref):
    i = pl.program_id(0)
    flags = pf_ref[i]
    win = h_ref.shape[0]

    @pl.when((flags & 1) != 0)
    def _():
        acc_ref[...] = jnp.zeros_like(acc_ref)
        swacc_ref[...] = jnp.zeros_like(swacc_ref)

    @pl.when((flags & 4) != 0)
    def _():
        slot = pb_ref[i] * MOE_ROWS + lax.broadcasted_iota(I32, (MOE_ROWS, 1), 0)
        fresh = ps_ref[i] + lax.broadcasted_iota(I32, (1, win), 1) >= pn_ref[i]
        hit0 = (dest_ref[0:1, :] == slot) & fresh
        hit1 = (dest_ref[1:2, :] == slot) & fresh
        acc_ref[...] += jnp.dot((hit0 | hit1).astype(BF16), h_ref[...], preferred_element_type=F32)
        swacc_ref[...] += jnp.sum(jnp.where(hit0, wt_ref[0:1, :], 0.0) + jnp.where(hit1, wt_ref[1:2, :], 0.0),
                                  axis=1, keepdims=True)

    @pl.when((flags & 2) != 0)
    def _():
        o_ref[...] = acc_ref[...].astype(o_ref.dtype)
        sw_ref[...] = swacc_ref[...]


def moe_gather(h, dest_t, wt_t, n_slots, win, pair_block, pair_start, pair_new, pair_flags):
    t, d = h.shape
    n_pairs = pair_block.shape[0]
    start = lambda ps, i: pl.multiple_of(ps[i], ROW_TILE)
    tok = pl.BlockSpec((pl.Element(2), pl.Element(win)), lambda i, pb, ps, pn, pf: (0, start(ps, i)))
    return pl.pallas_call(
        _gather_kernel,
        grid_spec=pltpu.PrefetchScalarGridSpec(
            num_scalar_prefetch=4, grid=(n_pairs,),
            in_specs=[pl.BlockSpec((pl.Element(win), pl.Element(d)),
                                   lambda i, pb, ps, pn, pf: (start(ps, i), 0)), tok, tok],
            out_specs=[pl.BlockSpec((MOE_ROWS, d), lambda i, pb, ps, pn, pf: (pb[i], 0)),
                       pl.BlockSpec((MOE_ROWS, 1), lambda i, pb, ps, pn, pf: (pb[i], 0))],
            scratch_shapes=[pltpu.VMEM((MOE_ROWS, d), F32), pltpu.VMEM((MOE_ROWS, 1), F32)]),
        out_shape=[jax.ShapeDtypeStruct((n_slots, d), BF16), jax.ShapeDtypeStruct((n_slots, 1), F32)],
        compiler_params=_params("arbitrary"), name="moe_gather",
    )(pair_block, pair_start, pair_new, pair_flags, h, dest_t, wt_t)


def _combine_kernel(st_ref, lo_ref, cn_ref, *refs):
    yb_refs = refs[:N_EXPERTS]
    dest_ref, x_ref, gt_ref, ng_ref, o_ref = refs[N_EXPERTS:]
    i = pl.program_id(0)
    win = yb_refs[0].shape[0]
    d0 = dest_ref[:, 0:1]
    d1 = dest_ref[:, 1:2]
    acc = None
    for e in range(N_EXPERTS):
        g = i * N_EXPERTS + e
        slot = st_ref[g] + lax.broadcasted_iota(I32, (1, win), 1)
        mine = (slot >= lo_ref[g]) & (slot < lo_ref[g] + cn_ref[g])
        hit = ((d0 == slot) | (d1 == slot)) & mine
        part = jnp.dot(hit.astype(BF16), yb_refs[e][...], preferred_element_type=F32)
        acc = part if acc is None else acc + part
    y = x_ref[...] + gt_ref[...] * acc
    o_ref[...] = y * lax.rsqrt(jnp.mean(y * y, axis=-1, keepdims=True) + EPS) * ng_ref[...]


def moe_combine_norm(yb, dest, win_start, grp_lo, grp_cnt, x, gate, norm_g):
    t, d = x.shape
    tm = min(ROW_TILE, t)
    win = 2 * MOE_ROWS
    yb_spec = lambda e: pl.BlockSpec((pl.Element(win), pl.Element(d)),
                                     lambda i, st, lo, cn: (pl.multiple_of(st[i * N_EXPERTS + e], MOE_ROWS), 0))
    vec = pl.BlockSpec((1, d), lambda i, st, lo, cn: (0, 0))
    return pl.pallas_call(
        _combine_kernel,
        grid_spec=pltpu.PrefetchScalarGridSpec(
            num_scalar_prefetch=3, grid=(t // tm,),
            in_specs=[yb_spec(e) for e in range(N_EXPERTS)]
            + [pl.BlockSpec((tm, 2), lambda i, st, lo, cn: (i, 0)),
               pl.BlockSpec((tm, d), lambda i, st, lo, cn: (i, 0)), vec, vec],
            out_specs=pl.BlockSpec((tm, d), lambda i, st, lo, cn: (i, 0))),
        out_shape=jax.ShapeDtypeStruct((t, d), F32),
        compiler_params=_params("arbitrary"), name="moe_combine",
    )(win_start, grp_lo, grp_cnt, *([yb] * N_EXPERTS), dest, x, gate.reshape(1, d), norm_g.reshape(1, d))


def _pair_lists(lo, hi, nonempty, n_pairs):
    cnt = jnp.where(nonempty, hi - lo + 1, 1)
    end = jnp.cumsum(cnt)
    start = end - cnt
    i = jnp.arange(n_pairs, dtype=I32)
    ic = jnp.minimum(i, end[-1] - 1)
    grp = jnp.sum((end[None, :] <= ic[:, None]).astype(I32), axis=1)
    off = ic - start[grp]
    member = jnp.where(nonempty[grp], lo[grp] + off, 0).astype(I32)
    return grp, member, off == 0, off == cnt[grp] - 1, nonempty[grp], i < end[-1]


def _pair_flags(first, last, data, valid):
    flags = jnp.where(first, 1, 0) | jnp.where(last, 2, 0) | jnp.where(data, 4, 0)
    return jnp.where(valid, flags, 0).astype(I32)


def moe_plan(top_idx, rank, tile_cnt, n_tok):
    tm = min(ROW_TILE, n_tok)
    n_tiles = n_tok // tm
    n_blocks = 2 * n_tok // MOE_ROWS + N_EXPERTS
    experts = jnp.arange(N_EXPERTS, dtype=I32)
    tile_off = jnp.cumsum(tile_cnt, axis=0) - tile_cnt
    counts = jnp.sum(tile_cnt, axis=0)
    padded = (counts + MOE_ROWS - 1) // MOE_ROWS * MOE_ROWS
    pad_end = jnp.cumsum(padded)
    base = (pad_end - padded)[None, :] + tile_off
    hot = top_idx[:, :, None] == experts[None, None, :]
    dest = (jnp.sum(jnp.where(hot, jnp.repeat(base, tm, axis=0)[:, None, :], 0), axis=-1) + rank).astype(I32)
    blk = jnp.arange(n_blocks, dtype=I32)
    block_expert = jnp.minimum(jnp.sum((pad_end[None, :] <= (blk * MOE_ROWS)[:, None]).astype(I32), axis=1),
                               N_EXPERTS - 1)
    c_start = jnp.minimum(base // MOE_ROWS * MOE_ROWS, (n_blocks - 2) * MOE_ROWS)
    mine = block_expert[:, None, None] == experts[None, None, :]
    base_b = jnp.sum(jnp.where(mine, base[None], 0), axis=-1)
    cnt_b = jnp.sum(jnp.where(mine, tile_cnt[None], 0), axis=-1)
    sends = (cnt_b > 0) & (base_b + cnt_b > (blk * MOE_ROWS)[:, None]) & (base_b < ((blk + 1) * MOE_ROWS)[:, None])
    tiles = jnp.arange(n_tiles, dtype=I32)[None, :]
    t_lo = jnp.min(jnp.where(sends, tiles, n_tiles), axis=1)
    t_hi = jnp.max(jnp.where(sends, tiles, -1), axis=1)
    wt = min(GATHER_WINDOW_TILES, n_tiles)
    n_windows = n_blocks + (N_EXPERTS * n_tiles + n_blocks + wt - 1) // wt
    g_block, g_win, *g_bits = _pair_lists(jnp.zeros_like(t_lo), (t_hi - t_lo) // wt, jnp.any(sends, axis=1),
                                          n_windows)
    g_new = (t_lo[g_block] + g_win * wt) * tm
    g_start = jnp.minimum(g_new, (n_tiles - wt) * tm)
    return dict(dest=dest, block_expert=block_expert.astype(I32), n_slots=n_blocks * MOE_ROWS,
                n_used=(pad_end[-1:] // MOE_ROWS).astype(I32), window=wt * tm,
                gather=(g_block, g_start.astype(I32), g_new.astype(I32), _pair_flags(*g_bits)),
                combine=(c_start.reshape(-1).astype(I32), base.reshape(-1).astype(I32),
                         tile_cnt.reshape(-1).astype(I32)))


def _mods(mod_all, layer, row, d):
    m = mod_all[layer, row]
    return tuple(m[j * d:(j + 1) * d] for j in range(N_MOD))


def _even_layer(x_lat, x_ctx, mods_lat, mods_ctx, norm1_g, norm2_g, w):
    d = x_lat.shape[1]
    hw = DN_HEADS * DN_DH
    s_dn = jnp.zeros((2, DN_HEADS, DN_DH, DN_DH), F32)
    s_lru = jnp.zeros((2, 8, hw), F32)
    outs = []
    for x, mods in ((x_ctx, mods_ctx), (x_lat, mods_lat)):
        sh1, sc1, gt1, sh2, sc2, gt2 = mods
        p, pg = norm_proj(x, norm1_g, sh1, sc1, w["w_main"], w["w_gate"])
        q, k, v, xc, gx, gxt = even_prep(p, pg, w["conv_w"], w["conv_b"], w["a_log"], w["dt_bias"])
        u, wm, qd, kd, qk, gl = deltanet_prep(q, k, v, gx, gxt)
        o_f, o_b, s_dn = deltanet_scan(u, wm, qd, kd, qk, gl, s_dn)
        h_f, h_b, s_lru = lru_scan(xc, w["lru_wa"], w["lru_wx"], w["lru_ba"], w["lru_bx"], w["lru_lam"], s_lru)
        x, h2 = even_finish(o_f, o_b, h_f, h_b, p, w["dn_norm_g"], w["w_out"], x, gt1, norm2_g, sh2, sc2)
        tm = _dense_rows(h2.shape[0])
        be = jnp.zeros((h2.shape[0] // tm,), I32)
        every = jnp.full((1,), h2.shape[0] // tm, I32)
        hh = ffn_up(h2, w["ffn_gate"], w["ffn_up"], be, every, tm=tm)
        x = ffn_down_residual(hh, w["ffn_down"], be, every, x, gt2, tm=tm)
        outs.append(x)
    return outs[1], outs[0]


def _odd_layer_last(x_lat, x_ctx, mods_lat, mods_ctx, norm1_g, norm2_g, final_g, w):
    dk, dv = w["dk"], w["dv"]
    states = jnp.zeros((2, GLA_HEADS, dv, dk), F32)
    sh1, sc1 = mods_ctx[0], mods_ctx[1]
    p, pg = norm_proj(x_ctx, norm1_g, sh1, sc1, w["w_main"], w["w_gate"])
    _, _, states = gla_scan(p, pg, w["wg_pad"], w["bg"], states, dk=dk, dv=dv)
    sh1, sc1, gt1, sh2, sc2, gt2 = mods_lat
    p, pg = norm_proj(x_lat, norm1_g, sh1, sc1, w["w_main"], w["w_gate"], raster=True)
    o_f, o_b, _ = gla_scan(p, pg, w["wg_pad"], w["bg"], states, dk=dk, dv=dv)
    x = odd_finish(o_f, o_b, p, w["gla_norm_g"], w["w_out"], x_lat, gt1)
    n_tok = x.shape[0]
    h2, top_idx, top_w, rank, tile_cnt = route(x, norm2_g, sh2, sc2, w["router_w"], w["router_b"])
    plan = moe_plan(top_idx, rank, tile_cnt, n_tok)
    xb, slot_w = moe_gather(h2, plan["dest"].T, top_w.T, plan["n_slots"], plan["window"], *plan["gather"])
    hh = ffn_up(xb, w["exp_gate"], w["exp_up"], plan["block_expert"], plan["n_used"], tm=MOE_ROWS)
    yb = ffn_down_scaled(hh, w["exp_down"], plan["block_expert"], plan["n_used"], slot_w)
    return moe_combine_norm(yb, plan["dest"], *plan["combine"], x, gt2, final_g)


def kernel(x, c, ctx, c_ctx, mod_w, mod_b, norm1_g, norm2_g, ev_w_in, ev_conv_qkv, ev_dn_a_log, ev_dn_dt_bias,
           ev_dn_norm_g, ev_lru_conv_w, ev_lru_conv_b, ev_lru_wa, ev_lru_ba, ev_lru_wx, ev_lru_bx, ev_lru_lambda,
           ev_w_out, ev_ffn_w_gate, ev_ffn_w_up, ev_ffn_w_down, od_w_in, od_gla_wg2, od_gla_bg, od_gla_norm_g,
           od_w_out, od_router_w, od_router_b, od_exp_w_gate, od_exp_w_up, od_exp_w_down, final_norm_g):
    b_, length, d = x.shape
    assert b_ == 1 and mod_w.shape[0] == 2, "this kernel implements the batch-1, depth-2 configuration"
    hw = DN_HEADS * DN_DH
    x_lat, x_ctx = x[0], ctx[0]

    cond8 = jnp.zeros((8, d), F32).at[0].set(c[0]).at[1].set(c_ctx)
    mod_all = adaln_all(cond8, mod_w, mod_b)

    w_in = ev_w_in[0]
    qkv_w, z0, ab0, xr0, gr0 = 3 * hw, 3 * hw, 4 * hw, 4 * hw + 4 * DN_HEADS, 5 * hw + 4 * DN_HEADS
    w_main = jnp.concatenate([w_in[:, :qkv_w], w_in[:, xr0:xr0 + hw], w_in[:, z0:z0 + hw], w_in[:, gr0:gr0 + hw]],
                             axis=1).astype(BF16)
    w_gate = jnp.zeros((d, LANES), BF16).at[:, :4 * DN_HEADS].set(w_in[:, ab0:ab0 + 4 * DN_HEADS].astype(BF16))
    pad16 = lambda t: jnp.zeros((1, LANES), F32).at[0, :2 * DN_HEADS].set(t.reshape(-1))
    ev = dict(
        w_main=w_main, w_gate=w_gate,
        conv_w=jnp.concatenate([ev_conv_qkv[0], ev_lru_conv_w[0]], axis=1),
        conv_b=jnp.concatenate([jnp.zeros((qkv_w,), F32), ev_lru_conv_b[0]]).reshape(1, -1),
        a_log=pad16(ev_dn_a_log[0]), dt_bias=pad16(ev_dn_dt_bias[0]),
        dn_norm_g=ev_dn_norm_g[0],
        lru_wa=ev_lru_wa[0].astype(BF16), lru_wx=ev_lru_wx[0].astype(BF16),
        lru_ba=ev_lru_ba[0].reshape(2, 1, hw), lru_bx=ev_lru_bx[0].reshape(2, 1, hw),
        lru_lam=ev_lru_lambda[0].reshape(2, 1, hw),
        w_out=ev_w_out[0].astype(BF16),
        ffn_gate=ev_ffn_w_gate.astype(BF16), ffn_up=ev_ffn_w_up.astype(BF16), ffn_down=ev_ffn_w_down.astype(BF16),
    )
    x_lat, x_ctx = _even_layer(x_lat, x_ctx, _mods(mod_all, 0, 0, d), _mods(mod_all, 0, 1, d),
                               norm1_g[0], norm2_g[0], ev)

    w_in = od_w_in[0]
    qk_w = od_gla_wg2.shape[-1]
    v_w = od_w_out.shape[1]
    main_w = 2 * qk_w + 2 * v_w
    wg_pad = jnp.zeros((2, LANES, qk_w), F32)
    for dirn in range(2):
        wg_pad = wg_pad.at[dirn, dirn * GLA_RANK:(dirn + 1) * GLA_RANK].set(od_gla_wg2[0, dirn])
    od = dict(
        w_main=w_in[:, :main_w].astype(BF16),
        w_gate=jnp.zeros((d, LANES), BF16).at[:, :2 * GLA_RANK].set(w_in[:, main_w:].astype(BF16)),
        wg_pad=wg_pad, bg=od_gla_bg[0].reshape(2, 1, qk_w), gla_norm_g=od_gla_norm_g[0],
        w_out=od_w_out[0].astype(BF16), router_w=od_router_w[0], router_b=od_router_b[0],
        exp_gate=od_exp_w_gate[0].astype(BF16), exp_up=od_exp_w_up[0].astype(BF16),
        exp_down=od_exp_w_down[0].astype(BF16),
        dk=qk_w // GLA_HEADS, dv=v_w // GLA_HEADS,
    )
    return _odd_layer_last(x_lat, x_ctx, _mods(mod_all, 1, 0, d), _mods(mod_all, 1, 1, d),
                           norm1_g[1], norm2_g[1], final_norm_g, od)[None]
```
